```python
import math
import jax
import jax.numpy as jnp
from jax import lax
import numpy as np

D_MODEL = 1024
BATCH = 8
SEQ = 4096
DEPTH = 4

GRID_W = 64
CTX_LEN = 256
N_MIXERS = 4
W_GROUP = D_MODEL // N_MIXERS
HEAD_DIM = 64
SHORT_CONV = 3
GDN_HEADS = W_GROUP // HEAD_DIM
GDN_CHUNK = 64
HY_BANDS = 16
HY_EMB = 2 * HY_BANDS + 1
HY_HIDDEN = 64
HY_FILTER_INIT = 0.005
HG_HEADS = W_GROUP // HEAD_DIM
HG_DK = HEAD_DIM
HG_CHUNK = 64
DA_HEADS = W_GROUP // HEAD_DIM
DA_DK = HEAD_DIM // 2
Q_BLOCK = 128
ROPE_THETA = 10000.0
D_FF = 4 * D_MODEL
EPS = 1e-6

GDN_COLS = 4 * W_GROUP + 4 * GDN_HEADS
HY_COLS = 3 * W_GROUP
HG_COLS = 5 * W_GROUP
DA_COLS = 3 * W_GROUP
GROUP_COLS = (GDN_COLS, HY_COLS, HG_COLS, DA_COLS)
P_IN = GDN_COLS + HY_COLS + HG_COLS + DA_COLS

kernel_name = 'hybrid_parallel_heads_flow_block'


def _split(t, sizes):
    return jnp.split(t, np.cumsum(sizes)[:-1].tolist(), axis=-1)


def rms_norm(x, w):
    xf = x.astype(jnp.float32)
    y = xf * lax.rsqrt(jnp.mean(xf * xf, axis=-1, keepdims=True) + EPS)
    return (y * w.astype(jnp.float32)).astype(x.dtype)


def l2_norm(x):
    xf = x.astype(jnp.float32)
    return xf * lax.rsqrt(jnp.sum(xf * xf, axis=-1, keepdims=True) + EPS)


def modulate(x, norm_w, shift, scale):
    return rms_norm(x, norm_w) * (1.0 + scale) + shift


def conv_centred(u, w):
    k, ch = w.shape
    return lax.conv_general_dilated(u, w.reshape(k, 1, ch).astype(u.dtype), window_strides=(1,),
                                    padding=[(k // 2, k // 2)], dimension_numbers=('NWC', 'WIO', 'NWC'),
                                    feature_group_count=ch)


def _flip(t, d):
    return t if d == 0 else jnp.flip(t, axis=1)


def _to_chunks(t, size):
    b, n, h = t.shape[:3]
    t = t.reshape(b, n // size, size, h, *t.shape[3:])
    return jnp.moveaxis(t, 3, 1)


def _from_chunks(t):
    t = jnp.moveaxis(t, 1, 3)
    b, n, c, h = t.shape[:4]
    return t.reshape(b, n * c, h, *t.shape[4:])


def gated_delta_chunked(q, k, v, g, beta, s0):
    dk = q.shape[-1]
    dv = v.shape[-1]
    f32 = jnp.float32
    q, k, v = [_to_chunks(t.astype(f32), GDN_CHUNK) for t in (q, k, v)]
    g, beta = [_to_chunks(t.astype(f32), GDN_CHUNK) for t in (g, beta)]
    q = q * dk ** -0.5
    gc = jnp.cumsum(g, axis=-1)
    tril = jnp.tril(jnp.ones((GDN_CHUNK, GDN_CHUNK), bool))
    stril = jnp.tril(jnp.ones((GDN_CHUNK, GDN_CHUNK), bool), -1)
    decay = jnp.exp(jnp.where(tril, gc[..., :, None] - gc[..., None, :], -jnp.inf))
    kk = jnp.einsum('bhnid,bhnjd->bhnij', k, k)
    t_mat = jnp.where(stril, kk * beta[..., :, None] * decay, 0.0) + jnp.eye(GDN_CHUNK, dtype=f32)
    rhs = jnp.concatenate([v * beta[..., None], k * (beta * jnp.exp(gc))[..., None]], axis=-1)
    sol = lax.linalg.triangular_solve(t_mat, rhs, left_side=True, lower=True)
    u, w = sol[..., :dv], sol[..., dv:]
    attn = jnp.einsum('bhnid,bhnjd->bhnij', q, k) * decay
    qg = q * jnp.exp(gc)[..., None]
    kg = k * jnp.exp(gc[..., -1:] - gc)[..., None]
    glast = jnp.exp(gc[..., -1])
    xs = [jnp.moveaxis(t, 2, 0) for t in (u, w, attn, qg, kg, glast)]

    def step(s, inp):
        u_i, w_i, a_i, qg_i, kg_i, gl_i = inp
        v_new = u_i - jnp.einsum('bhck,bhkv->bhcv', w_i, s)
        o = jnp.einsum('bhck,bhkv->bhcv', qg_i, s) + jnp.einsum('bhcs,bhsv->bhcv', a_i, v_new)
        s = s * gl_i[..., None, None] + jnp.einsum('bhck,bhcv->bhkv', kg_i, v_new)
        return s, o

    s_fin, o = lax.scan(step, s0, xs)
    return _from_chunks(jnp.moveaxis(o, 0, 2)), s_fin


def gla_chunked(q, k, v, logf, s0):
    f32 = jnp.float32
    xs = [jnp.moveaxis(_to_chunks(t.astype(f32), HG_CHUNK), 2, 0) for t in (q, k, v, logf)]
    tril = jnp.tril(jnp.ones((HG_CHUNK, HG_CHUNK), bool))[:, :, None]

    def step(s, inp):
        q_i, k_i, v_i, lf_i = inp
        gcum = jnp.cumsum(lf_i, axis=2)
        dec = jnp.exp(jnp.where(tril, gcum[:, :, :, None, :] - gcum[:, :, None, :, :], -jnp.inf))
        a = jnp.einsum('bhtd,bhsd,bhtsd->bhts', q_i, k_i, dec)
        o = jnp.einsum('bhtd,bhdv->bhtv', q_i * jnp.exp(gcum), s) + jnp.einsum('bhts,bhsv->bhtv', a, v_i)
        g_end = gcum[:, :, -1:, :]
        s = s * jnp.exp(g_end)[:, :, 0, :, None] + jnp.einsum('bhsd,bhsv->bhdv', k_i * jnp.exp(g_end - gcum), v_i)
        return s, o

    s_fin, o = lax.scan(step, s0, xs)
    return _from_chunks(jnp.moveaxis(o, 0, 2)), s_fin


def gdn_mixer(p_lat, p_ctx, conv_w, a_log, dt_bias, norm_w, want_ctx):
    def prep(t):
        b, n = t.shape[:2]
        qkv, z, a, bt = _split(t, [3 * W_GROUP, W_GROUP, 2 * GDN_HEADS, 2 * GDN_HEADS])
        q, k, v = _split(jax.nn.silu(conv_centred(qkv, conv_w)), [W_GROUP] * 3)
        hs = (b, n, GDN_HEADS, HEAD_DIM)
        a = a.astype(jnp.float32).reshape(b, n, 2, GDN_HEADS)
        g = -jnp.exp(a_log.astype(jnp.float32)) * jax.nn.softplus(a + dt_bias.astype(jnp.float32))
        beta = jax.nn.sigmoid(bt.astype(jnp.float32).reshape(b, n, 2, GDN_HEADS))
        return l2_norm(q.reshape(hs)), l2_norm(k.reshape(hs)), v.reshape(hs), z, g, beta

    def run(pk, d, s_init):
        q, k, v, _, g, beta = pk
        o, s = gated_delta_chunked(_flip(q, d), _flip(k, d), _flip(v, d), _flip(g[:, :, d], d),
                                   _flip(beta[:, :, d], d), s_init)
        return _flip(o, d), s

    def finish(o, z):
        b, n = z.shape[:2]
        gate = jax.nn.silu(z.astype(jnp.float32)).reshape(b, n, GDN_HEADS, HEAD_DIM)
        return (rms_norm(o, norm_w) * gate).reshape(b, n, W_GROUP).astype(z.dtype)

    pk_lat, pk_ctx = prep(p_lat), prep(p_ctx)
    s0 = jnp.zeros((p_lat.shape[0], GDN_HEADS, HEAD_DIM, HEAD_DIM), jnp.float32)
    o_lat, o_ctx = 0.0, 0.0
    for d in range(2):
        oc, s_ctx = run(pk_ctx, d, s0)
        ol, _ = run(pk_lat, d, s_ctx)
        o_lat = o_lat + ol
        o_ctx = o_ctx + oc
    return finish(o_lat, pk_lat[3]), (finish(o_ctx, pk_ctx[3]) if want_ctx else None)


def hyena_filters(n, w1, b1, f1, w2, b2, f2, w3, decay):
    f32 = jnp.float32
    pos = jnp.arange(n, dtype=f32)
    t01 = pos / max(n - 1, 1)
    bands = jnp.linspace(1e-4, HY_BANDS - 1, HY_BANDS, dtype=f32)
    ang = (2.0 * math.pi / n) * pos[:, None] * bands
    z = jnp.concatenate([t01[:, None], jnp.cos(ang), -jnp.sin(ang)], axis=-1)
    h = jnp.sin(f1.astype(f32) * (z @ w1.astype(f32) + b1.astype(f32)))
    h = jnp.sin(f2.astype(f32) * (h @ w2.astype(f32) + b2.astype(f32)))
    h = (h @ w3.astype(f32)).reshape(n, 2, 2, W_GROUP)
    window = jnp.exp(-t01[:, None, None] * jnp.abs(decay.astype(f32)))
    return h * window[:, None]


def long_conv_bidir(u, h_fwd, h_bwd, skip):
    n, ch = h_fwd.shape
    kern = jnp.concatenate([h_fwd, jnp.zeros((1, ch), jnp.float32), jnp.flip(h_bwd[1:], axis=0)], axis=0)
    kf = jnp.fft.rfft(kern, n=2 * n, axis=0)
    uf = jnp.fft.rfft(u.astype(jnp.float32), n=2 * n, axis=1)
    y = jnp.fft.irfft(uf * kf, n=2 * n, axis=1)[:, :n]
    return (y + u.astype(jnp.float32) * skip.astype(jnp.float32)).astype(u.dtype)


def hyena_mixer(p_lat, p_ctx, conv_w, filt_params, decay, skip, want_ctx):
    def run(t):
        v, x1, x2 = _split(conv_centred(t, conv_w), [W_GROUP] * 3)
        h = hyena_filters(t.shape[1], *filt_params, decay)
        z = x1 * long_conv_bidir(v, h[:, 0, 0], h[:, 1, 0], skip[0])
        return x2 * long_conv_bidir(z, h[:, 0, 1], h[:, 1, 1], skip[1])
    return run(p_lat), (run(p_ctx) if want_ctx else None)


def hgrn2_mixer(p_lat, p_ctx, lb, norm_w, want_ctx):
    lbh = lb.reshape(HG_HEADS, HG_DK)

    def prep(t):
        b, n = t.shape[:2]
        q, i, f_fwd, f_bwd, og = _split(t, [W_GROUP] * 5)
        fl = jnp.stack([f_fwd, f_bwd], axis=2).astype(jnp.float32).reshape(b, n, 2, HG_HEADS, HG_DK)
        logf = jnp.logaddexp(jnp.log(lbh), jnp.log1p(-lbh) + jax.nn.log_sigmoid(fl))
        key = (1.0 - lbh) * jax.nn.sigmoid(-fl)
        return (jax.nn.silu(q).reshape(b, n, HG_HEADS, HG_DK), i.reshape(b, n, HG_HEADS, HEAD_DIM), logf, key, og)

    def run(pk, d, s_init):
        q, i, logf, key, _ = pk
        o, s = gla_chunked(_flip(q, d), _flip(key[:, :, d], d), _flip(i, d), _flip(logf[:, :, d], d), s_init)
        return _flip(o, d), s

    def finish(o, og):
        b, n = og.shape[:2]
        gate = jax.nn.silu(og.astype(jnp.float32)).reshape(b, n, HG_HEADS, HEAD_DIM)
        return (rms_norm(o, norm_w) * gate).reshape(b, n, W_GROUP).astype(og.dtype)

    pk_lat, pk_ctx = prep(p_lat), prep(p_ctx)
    s0 = jnp.zeros((p_lat.shape[0], HG_HEADS, HG_DK, HEAD_DIM), jnp.float32)
    o_lat, o_ctx = 0.0, 0.0
    for d in range(2):
        oc, s_ctx = run(pk_ctx, d, s0)
        ol, _ = run(pk_lat, d, s_ctx)
        o_lat = o_lat + ol
        o_ctx = o_ctx + oc
    return finish(o_lat, pk_lat[4]), (finish(o_ctx, pk_ctx[4]) if want_ctx else None)


def axial_rope_tables(rows):
    n_freq = DA_DK // 4
    inv = ROPE_THETA ** (-jnp.arange(n_freq, dtype=jnp.float32) / n_freq)
    r = jnp.repeat(jnp.arange(rows, dtype=jnp.float32), GRID_W)
    col = jnp.tile(jnp.arange(GRID_W, dtype=jnp.float32), rows)
    ang_r = r[:, None] * inv
    ang_c = col[:, None] * inv
    return tuple(t[None, :, None, None, :] for t in (jnp.cos(ang_r), jnp.sin(ang_r), jnp.cos(ang_c), jnp.sin(ang_c)))


def _rotate(x, cos, sin):
    x1, x2 = jnp.split(x, 2, axis=-1)
    return jnp.concatenate([x1 * cos - x2 * sin, x2 * cos + x1 * sin], axis=-1)


def apply_axial_rope(x, rope):
    cos_r, sin_r, cos_c, sin_c = rope
    xr, xcol = jnp.split(x.astype(jnp.float32), 2, axis=-1)
    return jnp.concatenate([_rotate(xr, cos_r, sin_r), _rotate(xcol, cos_c, sin_c)], axis=-1).astype(x.dtype)


def diff_attn_mixer(p_lat, p_ctx, rope, qn_w, kn_w, lam_p, subln_w, lam_init, want_ctx):
    def prep(t):
        b, n = t.shape[:2]
        q, k, v = _split(t, [W_GROUP] * 3)
        hs = (b, n, DA_HEADS, 2, DA_DK)
        return rms_norm(q.reshape(hs), qn_w), rms_norm(k.reshape(hs), kn_w), v.reshape(b, n, DA_HEADS, 2 * DA_DK)

    q, k, v = prep(p_lat)
    qc, kc, vc = prep(p_ctx)
    q = apply_axial_rope(q, rope)
    k = apply_axial_rope(k, rope)
    lp = lam_p.astype(jnp.float32)
    lam = jnp.exp(jnp.sum(lp[0] * lp[1])) - jnp.exp(jnp.sum(lp[2] * lp[3])) + lam_init
    scale = DA_DK ** -0.5

    def attend(qb, kk, vv):
        s = jnp.einsum('bqhjd,bkhjd->bhjqk', qb, kk).astype(jnp.float32) * scale
        pr = jax.nn.softmax(s, axis=-1)
        w = (pr[:, :, 0] - lam * pr[:, :, 1]).astype(vv.dtype)
        return jnp.einsum('bhqk,bkhe->bqhe', w, vv)

    def finish(o):
        b, n = o.shape[:2]
        return (rms_norm(o, subln_w) * (1.0 - lam_init)).reshape(b, n, W_GROUP)

    b, n = q.shape[:2]
    k_all = jnp.concatenate([k, kc], axis=1)
    v_all = jnp.concatenate([v, vc], axis=1)
    q_blocks = jnp.moveaxis(q.reshape(b, n // Q_BLOCK, Q_BLOCK, DA_HEADS, 2, DA_DK), 1, 0)
    o = lax.map(lambda qb: attend(qb, k_all, v_all), q_blocks)
    o = jnp.moveaxis(o, 0, 1).reshape(b, n, DA_HEADS, 2 * DA_DK)
    o_ctx = finish(attend(qc, kc, vc)) if want_ctx else None
    return finish(o), o_ctx


def sq_relu_mlp(h, w1, w2):
    return jnp.square(jax.nn.relu(h @ w1)) @ w2


def setup_inputs(seed: int = 0) -> dict:
    key = jax.random.key(seed)
    ks = iter(jax.random.split(key, 40))
    f32 = jnp.float32

    def nrm(shape, s):
        return jax.random.normal(next(ks), shape, f32) * s

    d = D_MODEL
    x = nrm((BATCH, SEQ, d), 1.0)
    c = nrm((BATCH, d), 1.0)
    ctx = nrm((BATCH, CTX_LEN, d), 1.0)
    c_ctx = nrm((d,), 1.0)
    mod_w = nrm((DEPTH, d, 6 * d), 0.5 * d ** -0.5)
    mod_b = nrm((DEPTH, 6 * d), 0.02)
    ln1_w = 1.0 + nrm((DEPTH, d), 0.05)
    ln2_w = 1.0 + nrm((DEPTH, d), 0.05)
    w_in = nrm((DEPTH, d, P_IN), d ** -0.5)
    gdn_conv_w = nrm((DEPTH, SHORT_CONV, 3 * W_GROUP), SHORT_CONV ** -0.5)
    gdn_a_log = jnp.log(jax.random.uniform(next(ks), (DEPTH, 2, GDN_HEADS), f32, 1.0, 16.0))
    dt = jnp.exp(jax.random.uniform(next(ks), (DEPTH, 2, GDN_HEADS), f32, math.log(1e-3), math.log(1e-1)))
    gdn_dt_bias = dt + jnp.log(-jnp.expm1(-dt))
    gdn_norm_w = 1.0 + nrm((DEPTH, HEAD_DIM), 0.05)
    hy_conv_w = nrm((DEPTH, SHORT_CONV, 3 * W_GROUP), SHORT_CONV ** -0.5)
    hy_w1 = nrm((DEPTH, HY_EMB, HY_HIDDEN), HY_EMB ** -0.5)
    hy_b1 = nrm((DEPTH, HY_HIDDEN), 0.1)
    hy_f1 = 1.0 + nrm((DEPTH, HY_HIDDEN), 0.05)
    hy_w2 = nrm((DEPTH, HY_HIDDEN, HY_HIDDEN), HY_HIDDEN ** -0.5)
    hy_b2 = nrm((DEPTH, HY_HIDDEN), 0.1)
    hy_f2 = 1.0 + nrm((DEPTH, HY_HIDDEN), 0.05)
    hy_w3 = nrm((DEPTH, HY_HIDDEN, 4 * W_GROUP), HY_FILTER_INIT)
    base = jnp.linspace(abs(math.log(1e-2)) / 1.5, abs(math.log(1e-2)) / 0.3, W_GROUP, dtype=f32)
    hy_decay = base + nrm((DEPTH, 2, W_GROUP), 0.1)
    hy_bias = nrm((DEPTH, 2, W_GROUP), 0.3)
    hg_lb_raw = nrm((DEPTH, W_GROUP), 0.5)
    hg_norm_w = 1.0 + nrm((DEPTH, HEAD_DIM), 0.05)
    da_q_norm = 1.0 + nrm((DEPTH, DA_DK), 0.05)
    da_k_norm = 1.0 + nrm((DEPTH, DA_DK), 0.05)
    da_lam = nrm((DEPTH, 4, DA_DK), 0.1)
    da_subln = 1.0 + nrm((DEPTH, 2 * DA_DK), 0.05)
    w_out = nrm((DEPTH, d, d), d ** -0.5)
    mlp_w1 = nrm((DEPTH, d, D_FF), d ** -0.5)
    mlp_w2 = nrm((DEPTH, D_FF, d), D_FF ** -0.5)
    return {'x': x, 'c': c, 'ctx': ctx, 'c_ctx': c_ctx, 'mod_w': mod_w, 'mod_b': mod_b,
            'ln1_w': ln1_w, 'ln2_w': ln2_w, 'w_in': w_in, 'gdn_conv_w': gdn_conv_w,
            'gdn_a_log': gdn_a_log, 'gdn_dt_bias': gdn_dt_bias, 'gdn_norm_w': gdn_norm_w,
            'hy_conv_w': hy_conv_w, 'hy_w1': hy_w1, 'hy_b1': hy_b1, 'hy_f1': hy_f1,
            'hy_w2': hy_w2, 'hy_b2': hy_b2, 'hy_f2': hy_f2, 'hy_w3': hy_w3,
            'hy_decay': hy_decay, 'hy_bias': hy_bias, 'hg_lb_raw': hg_lb_raw,
            'hg_norm_w': hg_norm_w, 'da_q_norm': da_q_norm, 'da_k_norm': da_k_norm,
            'da_lam': da_lam, 'da_subln': da_subln, 'w_out': w_out, 'mlp_w1': mlp_w1,
            'mlp_w2': mlp_w2}


def reference(x, c, ctx, c_ctx, mod_w, mod_b, ln1_w, ln2_w, w_in, gdn_conv_w, gdn_a_log,
              gdn_dt_bias, gdn_norm_w, hy_conv_w, hy_w1, hy_b1, hy_f1, hy_w2, hy_b2, hy_f2,
              hy_w3, hy_decay, hy_bias, hg_lb_raw, hg_norm_w, da_q_norm, da_k_norm, da_lam,
              da_subln, w_out, mlp_w1, mlp_w2):
    rows = x.shape[1] // GRID_W
    rope = axial_rope_tables(rows)
    lb_all = jnp.cumsum(jax.nn.softmax(hg_lb_raw.astype(jnp.float32), axis=0), axis=0)
    lb_all = lb_all - lb_all[0]
    s_lat = jax.nn.silu(c)
    s_ctx = jax.nn.silu(c_ctx)
    xc = ctx
    for l in range(DEPTH):
        want_ctx = l < DEPTH - 1
        sh1, s1, g1, sh2, s2, g2 = jnp.split((s_lat @ mod_w[l] + mod_b[l])[:, None, :], 6, axis=-1)
        ch1, cs1, cg1, ch2, cs2, cg2 = jnp.split(s_ctx @ mod_w[l] + mod_b[l], 6, axis=-1)
        lat_gdn, lat_hy, lat_hg, lat_da = _split(modulate(x, ln1_w[l], sh1, s1) @ w_in[l], GROUP_COLS)
        ctx_gdn, ctx_hy, ctx_hg, ctx_da = _split(modulate(xc, ln1_w[l], ch1, cs1) @ w_in[l], GROUP_COLS)
        o_a, o_a_c = gdn_mixer(lat_gdn, ctx_gdn, gdn_conv_w[l], gdn_a_log[l], gdn_dt_bias[l],
                               gdn_norm_w[l], want_ctx)
        o_b, o_b_c = hyena_mixer(lat_hy, ctx_hy, hy_conv_w[l],
                                 (hy_w1[l], hy_b1[l], hy_f1[l], hy_w2[l], hy_b2[l], hy_f2[l], hy_w3[l]),
                                 hy_decay[l], hy_bias[l], want_ctx)
        o_c, o_c_c = hgrn2_mixer(lat_hg, ctx_hg, lb_all[l], hg_norm_w[l], want_ctx)
        lam_init = 0.8 - 0.6 * math.exp(-0.3 * l)
        o_d, o_d_c = diff_attn_mixer(lat_da, ctx_da, rope, da_q_norm[l], da_k_norm[l], da_lam[l],
                                     da_subln[l], lam_init, want_ctx)
        x = x + g1 * (jnp.concatenate([o_a, o_b, o_c, o_d], axis=-1) @ w_out[l])
        x = x + g2 * sq_relu_mlp(modulate(x, ln2_w[l], sh2, s2), mlp_w1[l], mlp_w2[l])
        if want_ctx:
            xc = xc + cg1 * (jnp.concatenate([o_a_c, o_b_c, o_c_c, o_d_c], axis=-1) @ w_out[l])
            xc = xc + cg2 * sq_relu_mlp(modulate(xc, ln2_w[l], ch2, cs2), mlp_w1[l], mlp_w2[l])
    return x
```

```python
import functools
import math

import jax
import jax.numpy as jnp
from jax import lax
from jax.experimental import pallas as pl
from jax.experimental.pallas import tpu as pltpu

F32 = jnp.float32
BF16 = jnp.bfloat16

EPS = 1e-6
TOKEN_BLOCK = 256
CHUNK = 64
HEAD_DIM = 64
N_HEADS = 4
W_GROUP = 256
DA_DK = 32
GRID_W = 64
ROPE_THETA = 10000.0
HY_BANDS = 16
FFT_N2_LAT = 128
FFT_N2_CTX = 32
GDN_PAD_COLS = 1152
VMEM_LIMIT_BYTES = 56 * 1024 * 1024


def _cparams(n_axes):
    return pltpu.CompilerParams(dimension_semantics=("arbitrary",) * n_axes,
                                vmem_limit_bytes=VMEM_LIMIT_BYTES)


def _sigmoid(x):
    return 1.0 / (1.0 + jnp.exp(-x))


def _silu(x):
    return x * _sigmoid(x)


def _softplus(x):
    return jnp.maximum(x, 0.0) + jnp.log(1.0 + jnp.exp(-jnp.abs(x)))


def _log_sigmoid(x):
    return jnp.minimum(x, 0.0) - jnp.log(1.0 + jnp.exp(-jnp.abs(x)))


def _bdot(a, b):
    return jnp.dot(a.astype(BF16), b.astype(BF16), preferred_element_type=F32)


def _bdot_nt(a, b):
    return lax.dot_general(a.astype(BF16), b.astype(BF16), (((1,), (1,)), ((), ())),
                           preferred_element_type=F32)


def _split3(x):
    hi = x.astype(BF16)
    r = x - hi.astype(F32)
    mid = r.astype(BF16)
    lo = (r - mid.astype(F32)).astype(BF16)
    return hi, mid, lo


def _exact_left(m_bf16, x):
    hi, mid, lo = _split3(x)
    d = lambda p: jnp.dot(m_bf16, p, preferred_element_type=F32)
    return d(hi) + d(mid) + d(lo)


def _exact_right(x, m_bf16):
    hi, mid, lo = _split3(x)
    d = lambda p: jnp.dot(p, m_bf16, preferred_element_type=F32)
    return d(hi) + d(mid) + d(lo)


def _hp_dot(a, b):
    ah = a.astype(BF16)
    al = (a - ah.astype(F32)).astype(BF16)
    bh = b.astype(BF16)
    bl = (b - bh.astype(F32)).astype(BF16)
    d = lambda p, q: jnp.dot(p, q, preferred_element_type=F32)
    return d(ah, bh) + d(ah, bl) + d(al, bh)


def _group_ones(width, group):
    r = lax.broadcasted_iota(jnp.int32, (width, width), 0) // group
    c = lax.broadcasted_iota(jnp.int32, (width, width), 1) // group
    return (r == c).astype(BF16)


def _mod_kernel(s_ref, w_ref, b_ref, o_ref):
    s = _silu(s_ref[...])
    o_ref[0] = _bdot(s, w_ref[0]) + b_ref[0]


def _modulation(cond, mod_w, mod_b):
    depth, d, n = mod_w.shape
    rows = cond.shape[0]
    tn = 1536
    return pl.pallas_call(
        _mod_kernel,
        out_shape=jax.ShapeDtypeStruct((depth, rows, n), F32),
        grid=(depth, n // tn),
        in_specs=[pl.BlockSpec((rows, d), lambda l, j: (0, 0)),
                  pl.BlockSpec((1, d, tn), lambda l, j: (l, 0, j)),
                  pl.BlockSpec((1, 1, tn), lambda l, j: (l, 0, j))],
        out_specs=pl.BlockSpec((1, rows, tn), lambda l, j: (l, 0, j)),
        compiler_params=_cparams(2), name="modulation",
    )(cond, mod_w, mod_b.reshape(depth, 1, n))


def _adaln_proj_kernel(x_ref, mod_ref, lnw_ref, w_ref, *out_refs, shift_row, scale_row, splits, sq_relu):
    x = x_ref[0]
    ms = jnp.mean(x * x, axis=-1, keepdims=True)
    y = x * lax.rsqrt(ms + EPS) * lnw_ref[...]
    mod = mod_ref[0]
    y = y * (1.0 + mod[scale_row:scale_row + 1]) + mod[shift_row:shift_row + 1]
    h = _bdot(y, w_ref[...])
    if sq_relu:
        h = jnp.square(jnp.maximum(h, 0.0))
    off = 0
    for o_ref, n in zip(out_refs, splits):
        o_ref[0] = h[:, off:off + n].astype(o_ref.dtype)
        off += n


def _mod_index(nbl, batch):
    return lambda b, t: (jnp.where(t == nbl, batch, b), 0, 0)


def _adaln_proj(xs, modtab, ln_w, w, splits, out_dtypes, shift_row, scale_row, sq_relu, nbl, name):
    b, lt, d = xs.shape
    nb = lt // TOKEN_BLOCK
    n = w.shape[1]
    kern = functools.partial(_adaln_proj_kernel, shift_row=shift_row, scale_row=scale_row,
                             splits=splits, sq_relu=sq_relu)
    return pl.pallas_call(
        kern,
        out_shape=[jax.ShapeDtypeStruct((b, lt, s), dt) for s, dt in zip(splits, out_dtypes)],
        grid=(b, nb),
        in_specs=[pl.BlockSpec((1, TOKEN_BLOCK, d), lambda i, t: (i, t, 0)),
                  pl.BlockSpec((1, 8, d), _mod_index(nbl, b)),
                  pl.BlockSpec((1, d), lambda i, t: (0, 0)),
                  pl.BlockSpec((d, n), lambda i, t: (0, 0))],
        out_specs=[pl.BlockSpec((1, TOKEN_BLOCK, s), lambda i, t: (i, t, 0)) for s in splits],
        compiler_params=_cparams(2), name=name,
    )(xs, modtab, ln_w.reshape(1, d), w)


def _proj_residual_kernel(x_ref, mod_ref, w_ref, *rest, gate_row):
    a_refs, o_ref = rest[:-1], rest[-1]
    a = jnp.concatenate([r[0].astype(BF16) for r in a_refs], axis=1) if len(a_refs) > 1 else a_refs[0][0]
    gate = mod_ref[0][gate_row:gate_row + 1]
    o_ref[0] = x_ref[0] + gate * _bdot(a, w_ref[...])


def _proj_residual(xs, modtab, w, acts, gate_row, n_blocks, nbl, name):
    b, lt, d = xs.shape
    kern = functools.partial(_proj_residual_kernel, gate_row=gate_row)
    return pl.pallas_call(
        kern,
        out_shape=jax.ShapeDtypeStruct((b, n_blocks * TOKEN_BLOCK, d), F32),
        grid=(b, n_blocks),
        in_specs=[pl.BlockSpec((1, TOKEN_BLOCK, d), lambda i, t: (i, t, 0)),
                  pl.BlockSpec((1, 8, d), _mod_index(nbl, b)),
                  pl.BlockSpec(w.shape, lambda i, t: (0, 0))]
                 + [pl.BlockSpec((1, TOKEN_BLOCK, a.shape[2]), lambda i, t: (i, t, 0)) for a in acts],
        out_specs=pl.BlockSpec((1, TOKEN_BLOCK, d), lambda i, t: (i, t, 0)),
        compiler_params=_cparams(2), name=name,
    )(xs, modtab, w, *acts)


def _scan_block(step, nbl, direction):
    lat = step - 1 if direction == 0 else nbl - step
    return jnp.where(step == 0, nbl, lat)


def _halo_specs(width, nbl, block_of_step, lt):
    rows8 = TOKEN_BLOCK // 8

    def prev_map(b, s):
        return (b, jnp.maximum(block_of_step(s) * rows8 - 1, 0), 0)

    def next_map(b, s):
        return (b, jnp.minimum((block_of_step(s) + 1) * rows8, lt // 8 - 1), 0)

    return [pl.BlockSpec((1, 8, width), prev_map), pl.BlockSpec((1, 8, width), next_map)]


def _short_conv(cur, hp_ref, hn_ref, cw_ref, xp_ref, blk, nbl, width):
    seg_first = jnp.logical_or(blk == 0, blk == nbl)
    seg_last = jnp.logical_or(blk == nbl - 1, blk == nbl)
    xp_ref[0:8, :] = jnp.where(seg_first, 0.0, hp_ref[0][:, :width])
    xp_ref[8:8 + TOKEN_BLOCK, :] = cur
    xp_ref[8 + TOKEN_BLOCK:16 + TOKEN_BLOCK, :] = jnp.where(seg_last, 0.0, hn_ref[0][:, :width])
    cw = cw_ref[...]
    return (cw[0:1] * xp_ref[7:7 + TOKEN_BLOCK, :] + cw[1:2] * xp_ref[8:8 + TOKEN_BLOCK, :]
            + cw[2:3] * xp_ref[9:9 + TOKEN_BLOCK, :])


def _positions(direction, chunk):
    i = lax.broadcasted_iota(jnp.int32, (TOKEN_BLOCK, TOKEN_BLOCK), 0)
    j = lax.broadcasted_iota(jnp.int32, (TOKEN_BLOCK, TOKEN_BLOCK), 1)
    same = (i // chunk) == (j // chunk)
    if direction == 0:
        pi, pj = i % chunk, j % chunk
    else:
        pi, pj = chunk - 1 - i % chunk, chunk - 1 - j % chunk
    return same, pi, pj, i == j


def _half_mask(same, pi, pj, m):
    return same & ((pi // (2 * m)) == (pj // (2 * m))) & ((pi // m) % 2 == 1) & ((pj // m) % 2 == 0)


def _unit_triangular_inverse(n, same, pi, pj, eye, chunk):
    inv = jnp.where(eye, 1.0, 0.0) - jnp.where(_half_mask(same, pi, pj, 1), n, 0.0)
    m = 2
    while m < chunk:
        off = jnp.where(_half_mask(same, pi, pj, m), n, 0.0)
        inv = inv - _bdot(_bdot(inv, off), inv)
        m *= 2
    return inv


def _group_rms(o, gmat, group, weight):
    ms = _exact_right(o * o, gmat) * (1.0 / group)
    return o * lax.rsqrt(ms + EPS) * weight


def _gdn_kernel(*refs, direction, finish, nbl):
    if finish:
        p_ref, hp_ref, hn_ref, cw_ref, alog_ref, dtb_ref, nw_ref, other_ref, o_ref, s_ref, xp_ref, ob_ref = refs
    else:
        p_ref, hp_ref, hn_ref, cw_ref, alog_ref, dtb_ref, nw_ref, o_ref, s_ref, xp_ref, ob_ref = refs
    step = pl.program_id(1)
    blk = _scan_block(step, nbl, direction)

    @pl.when(step == 0)
    def _():
        s_ref[...] = jnp.zeros_like(s_ref)

    p = p_ref[0]
    xc = _silu(_short_conv(p[:, :3 * W_GROUP], hp_ref, hn_ref, cw_ref, xp_ref, blk, nbl, 3 * W_GROUP))
    g64 = _group_ones(W_GROUP, HEAD_DIM)
    q = xc[:, :W_GROUP]
    k = xc[:, W_GROUP:2 * W_GROUP]
    v = xc[:, 2 * W_GROUP:]
    q = q * lax.rsqrt(_exact_right(q * q, g64) + EPS) * (HEAD_DIM ** -0.5)
    k = k * lax.rsqrt(_exact_right(k * k, g64) + EPS)
    k_t = k.T

    ab = p[:, 4 * W_GROUP:4 * W_GROUP + 128]
    lane = lax.broadcasted_iota(jnp.int32, ab.shape, 1)
    g_all = -jnp.exp(alog_ref[...]) * _softplus(ab + dtb_ref[...])
    w_all = jnp.where(lane < 2 * N_HEADS, g_all, _sigmoid(ab))
    same, pi, pj, eye = _positions(direction, CHUNK)
    incl = same & (pj <= pi)
    strict = same & (pj < pi)
    z_col = jnp.where(lane < 2 * N_HEADS, _exact_left(incl.astype(BF16), w_all), w_all)
    z_row = z_col.T

    last = CHUNK - 1 if direction == 0 else 0
    n_chunks = TOKEN_BLOCK // CHUNK
    for h in range(N_HEADS):
        gi = N_HEADS * direction + h
        bi = 2 * N_HEADS + gi
        ls = slice(h * HEAD_DIM, (h + 1) * HEAD_DIM)
        qh, kh, vh = q[:, ls], k[:, ls], v[:, ls]
        kh_t = k_t[ls, :]
        gc_c = z_col[:, gi:gi + 1]
        gc_r = z_row[gi:gi + 1, :]
        beta = z_col[:, bi:bi + 1]
        decay = jnp.exp(jnp.where(incl, gc_c - gc_r, -jnp.inf))
        kk = _bdot_nt(kh, kh)
        qk = _bdot_nt(qh, kh)
        t_inv = _unit_triangular_inverse(jnp.where(strict, kk * beta * decay, 0.0), same, pi, pj, eye, CHUNK)
        eg = jnp.exp(gc_c)
        uw = _bdot(t_inv, jnp.concatenate([vh * beta, kh * (beta * eg)], axis=1))
        attn = qk * decay
        qg = qh * eg
        s_h = s_ref[h]
        for ci in range(n_chunks):
            c = ci if direction == 0 else n_chunks - 1 - ci
            rs = slice(c * CHUNK, (c + 1) * CHUNK)
            g_last = z_col[c * CHUNK + last:c * CHUNK + last + 1, gi:gi + 1]
            v_new = uw[rs, :HEAD_DIM] - _bdot(uw[rs, HEAD_DIM:], s_h)
            ob_ref[rs, ls] = _bdot(qg[rs], s_h) + _bdot(attn[rs, rs], v_new)
            kg_t = kh_t[:, rs] * jnp.exp(g_last - gc_r[:, rs])
            s_h = s_h * jnp.exp(g_last) + _bdot(kg_t, v_new)
        s_ref[h] = s_h

    if finish:
        o = ob_ref[...] + other_ref[0]
        gate = _silu(p[:, 3 * W_GROUP:4 * W_GROUP])
        o_ref[0] = _group_rms(o, g64, HEAD_DIM, nw_ref[...]) * gate
    else:
        o_ref[0] = ob_ref[...]


def _gdn_direction(p_gdn, conv_w, alog_row, dtb_row, nw_row, other, direction, nbl):
    b, lt, width = p_gdn.shape
    nb = nbl + 1
    blk_of = lambda s: _scan_block(s, nbl, direction)
    finish = other is not None
    kern = functools.partial(_gdn_kernel, direction=direction, finish=finish, nbl=nbl)
    in_specs = ([pl.BlockSpec((1, TOKEN_BLOCK, width), lambda i, s: (i, blk_of(s), 0))]
                + _halo_specs(3 * W_GROUP, nbl, blk_of, lt)
                + [pl.BlockSpec((3, 3 * W_GROUP), lambda i, s: (0, 0)),
                   pl.BlockSpec((1, 128), lambda i, s: (0, 0)),
                   pl.BlockSpec((1, 128), lambda i, s: (0, 0)),
                   pl.BlockSpec((1, W_GROUP), lambda i, s: (0, 0))])
    args = [p_gdn, p_gdn, p_gdn, conv_w, alog_row, dtb_row, nw_row]
    if finish:
        in_specs.append(pl.BlockSpec((1, TOKEN_BLOCK, W_GROUP), lambda i, s: (i, blk_of(s), 0)))
        args.append(other)
    return pl.pallas_call(
        kern,
        out_shape=jax.ShapeDtypeStruct((b, lt, W_GROUP), F32),
        grid=(b, nb),
        in_specs=in_specs,
        out_specs=pl.BlockSpec((1, TOKEN_BLOCK, W_GROUP), lambda i, s: (i, blk_of(s), 0)),
        scratch_shapes=[pltpu.VMEM((N_HEADS, HEAD_DIM, HEAD_DIM), F32),
                        pltpu.VMEM((TOKEN_BLOCK + 16, 3 * W_GROUP), F32),
                        pltpu.VMEM((TOKEN_BLOCK, W_GROUP), F32)],
        compiler_params=_cparams(2), name=f"gdn_dir{direction}",
    )(*args)


def _hgrn_kernel(*refs, direction, finish, nbl):
    if finish:
        p_ref, lb_ref, nw_ref, other_ref, o_ref, s_ref, ob_ref = refs
    else:
        p_ref, lb_ref, nw_ref, o_ref, s_ref, ob_ref = refs
    step = pl.program_id(1)

    @pl.when(step == 0)
    def _():
        s_ref[...] = jnp.zeros_like(s_ref)

    p = p_ref[0]
    lb = lb_ref[...]
    q = _silu(p[:, :W_GROUP])
    v = p[:, W_GROUP:2 * W_GROUP]
    fl = p[:, (2 + direction) * W_GROUP:(3 + direction) * W_GROUP]
    a = jnp.log(lb)
    bb = jnp.log(1.0 - lb) + _log_sigmoid(fl)
    logf = jnp.maximum(a, bb) + jnp.log(1.0 + jnp.exp(-jnp.abs(a - bb)))
    key = (1.0 - lb) * _sigmoid(-fl)
    same, pi, pj, eye = _positions(direction, TOKEN_BLOCK)
    gcum = _exact_left((pj <= pi).astype(BF16), logf)
    levels = []
    m = TOKEN_BLOCK // 2
    while m >= 1:
        boundary = (pi // (2 * m)) * (2 * m) + m - 1
        diff_mat = jnp.where(eye, 1.0, 0.0) - jnp.where(pj == boundary, 1.0, 0.0)
        levels.append((jnp.exp(-jnp.abs(_exact_left(diff_mat.astype(BF16), gcum))), _half_mask(same, pi, pj, m)))
        m //= 2
    last = TOKEN_BLOCK - 1 if direction == 0 else 0
    g_end = gcum[last:last + 1, :]
    qg = q * jnp.exp(gcum)
    kdec = key * jnp.exp(g_end - gcum)
    decay_end = jnp.exp(g_end)
    v_t = v.T

    for h in range(N_HEADS):
        ls = slice(h * HEAD_DIM, (h + 1) * HEAD_DIM)
        qh, kh = q[:, ls], key[:, ls]
        amat = jnp.where(eye, _bdot_nt(qh, kh), 0.0)
        for e, mask in levels:
            eh = e[:, ls]
            amat = amat + jnp.where(mask, _bdot_nt(qh * eh, kh * eh), 0.0)
        st = s_ref[h]
        ob_ref[:, ls] = _bdot_nt(qg[:, ls], st) + _bdot(amat, v[:, ls])
        s_ref[h] = st * decay_end[:, ls] + _bdot(v_t[ls, :], kdec[:, ls])

    if finish:
        o = ob_ref[...] + other_ref[0]
        gate = _silu(p[:, 4 * W_GROUP:5 * W_GROUP])
        o_ref[0] = _group_rms(o, _group_ones(W_GROUP, HEAD_DIM), HEAD_DIM, nw_ref[...]) * gate
    else:
        o_ref[0] = ob_ref[...]


def _hgrn_direction(p_hg, lb_row, nw_row, other, direction, nbl):
    b, lt, width = p_hg.shape
    nb = nbl + 1
    blk_of = lambda s: _scan_block(s, nbl, direction)
    finish = other is not None
    kern = functools.partial(_hgrn_kernel, direction=direction, finish=finish, nbl=nbl)
    in_specs = [pl.BlockSpec((1, TOKEN_BLOCK, width), lambda i, s: (i, blk_of(s), 0)),
                pl.BlockSpec((1, W_GROUP), lambda i, s: (0, 0)),
                pl.BlockSpec((1, W_GROUP), lambda i, s: (0, 0))]
    args = [p_hg, lb_row, nw_row]
    if finish:
        in_specs.append(pl.BlockSpec((1, TOKEN_BLOCK, W_GROUP), lambda i, s: (i, blk_of(s), 0)))
        args.append(other)
    return pl.pallas_call(
        kern,
        out_shape=jax.ShapeDtypeStruct((b, lt, W_GROUP), F32),
        grid=(b, nb),
        in_specs=in_specs,
        out_specs=pl.BlockSpec((1, TOKEN_BLOCK, W_GROUP), lambda i, s: (i, blk_of(s), 0)),
        scratch_shapes=[pltpu.VMEM((N_HEADS, HEAD_DIM, HEAD_DIM), F32),
                        pltpu.VMEM((TOKEN_BLOCK, W_GROUP), F32)],
        compiler_params=_cparams(2), name=f"hgrn_dir{direction}",
    )(*args)


def _da_prep_kernel(p_ref, cos_ref, sin_ref, qw_ref, kw_ref, q_ref, k_ref, v_ref):
    p = p_ref[0]
    g32 = _group_ones(W_GROUP, DA_DK)
    lane = lax.broadcasted_iota(jnp.int32, (TOKEN_BLOCK, W_GROUP), 1)
    first_half = (lane % 16) < 8
    cos, sin = cos_ref[...], sin_ref[...]

    def norm_rope(x, w):
        ms = _exact_right(x * x, g32) * (1.0 / DA_DK)
        y = x * lax.rsqrt(ms + EPS) * w
        partner = jnp.where(first_half, pltpu.roll(y, W_GROUP - 8, axis=1), pltpu.roll(y, 8, axis=1))
        return y * cos + partner * sin

    q_ref[0] = (norm_rope(p[:, :W_GROUP], qw_ref[...]) * (DA_DK ** -0.5)).astype(BF16)
    k_ref[0] = norm_rope(p[:, W_GROUP:2 * W_GROUP], kw_ref[...]).astype(BF16)
    v_ref[0] = p[:, 2 * W_GROUP:].astype(BF16)


def _da_prep(p_da, cos_tab, sin_tab, qw_row, kw_row):
    b, lt, width = p_da.shape
    nb = lt // TOKEN_BLOCK
    blk = lambda: pl.BlockSpec((1, TOKEN_BLOCK, W_GROUP), lambda i, t: (i, t, 0))
    return pl.pallas_call(
        _da_prep_kernel,
        out_shape=[jax.ShapeDtypeStruct((b, lt, W_GROUP), BF16)] * 3,
        grid=(b, nb),
        in_specs=[pl.BlockSpec((1, TOKEN_BLOCK, width), lambda i, t: (i, t, 0)),
                  pl.BlockSpec((TOKEN_BLOCK, W_GROUP), lambda i, t: (t, 0)),
                  pl.BlockSpec((TOKEN_BLOCK, W_GROUP), lambda i, t: (t, 0)),
                  pl.BlockSpec((1, W_GROUP), lambda i, t: (0, 0)),
                  pl.BlockSpec((1, W_GROUP), lambda i, t: (0, 0))],
        out_specs=[blk(), blk(), blk()],
        compiler_params=_cparams(2), name="da_prep",
    )(p_da, cos_tab, sin_tab, qw_row, kw_row)


def _da_attn_kernel(q_ref, k_ref, v_ref, lam_ref, sw_ref, o_ref, *, lam_init):
    q = q_ref[0]
    k = k_ref[0]
    lp = lam_ref[...]
    lam = (jnp.exp(jnp.sum(lp[0:1] * lp[1:2], axis=1, keepdims=True))
           - jnp.exp(jnp.sum(lp[2:3] * lp[3:4], axis=1, keepdims=True)) + lam_init)
    lane = lax.broadcasted_iota(jnp.int32, q.shape, 1)
    lane128 = lax.broadcasted_iota(jnp.int32, (q.shape[0], 128), 1)
    pairs = []
    for pair in range(N_HEADS // 2):
        v_pair = v_ref[0, :, pair * 128:(pair + 1) * 128]
        outs = []
        for hh in range(2):
            h = 2 * pair + hh
            maps = []
            for j in range(2):
                lo = h * HEAD_DIM + j * DA_DK
                qm = jnp.where(jnp.logical_and(lane >= lo, lane < lo + DA_DK), q, jnp.zeros_like(q))
                s = lax.dot_general(qm, k, (((1,), (1,)), ((), ())), preferred_element_type=F32)
                m = jnp.max(s, axis=1, keepdims=True)
                e = jnp.exp(s - m)
                denom = jnp.sum(e, axis=1, keepdims=True)
                acc = jnp.dot(e.astype(BF16), v_pair, preferred_element_type=F32)
                maps.append(acc / denom)
            outs.append(maps[0] - lam * maps[1])
        pairs.append(jnp.where(lane128 < HEAD_DIM, outs[0], outs[1]))
    o = jnp.concatenate(pairs, axis=1)
    g64 = _group_ones(W_GROUP, HEAD_DIM)
    o_ref[0] = _group_rms(o, g64, HEAD_DIM, sw_ref[...]) * (1.0 - lam_init)


def _da_attention(qh, kh, vh, lam_p, sw_row, lam_init, q_blocks, q_off, k_rows, k_blk, name):
    b, lt, _ = qh.shape
    kern = functools.partial(_da_attn_kernel, lam_init=lam_init)
    return pl.pallas_call(
        kern,
        out_shape=jax.ShapeDtypeStruct((b, q_blocks * TOKEN_BLOCK, W_GROUP), F32),
        grid=(b, q_blocks),
        in_specs=[pl.BlockSpec((1, TOKEN_BLOCK, W_GROUP), lambda i, t: (i, t + q_off, 0)),
                  pl.BlockSpec((1, k_rows, W_GROUP), lambda i, t: (i, k_blk, 0)),
                  pl.BlockSpec((1, k_rows, W_GROUP), lambda i, t: (i, k_blk, 0)),
                  pl.BlockSpec((4, DA_DK), lambda i, t: (0, 0)),
                  pl.BlockSpec((1, W_GROUP), lambda i, t: (0, 0))],
        out_specs=pl.BlockSpec((1, TOKEN_BLOCK, W_GROUP), lambda i, t: (i, t, 0)),
        compiler_params=_cparams(2), name=name,
    )(qh, kh, vh, lam_p, sw_row)


def _hy_conv_kernel(p_ref, hp_ref, hn_ref, cw_ref, v_ref, x1_ref, x2_ref, xp_ref, *, nbl):
    blk = pl.program_id(1)
    xc = _short_conv(p_ref[0], hp_ref, hn_ref, cw_ref, xp_ref, blk, nbl, 3 * W_GROUP)
    v_ref[0] = xc[:, :W_GROUP]
    x1_ref[0] = xc[:, W_GROUP:2 * W_GROUP]
    x2_ref[0] = xc[:, 2 * W_GROUP:]


def _hy_short_conv(p_hy, conv_w, nbl):
    b, lt, width = p_hy.shape
    nb = nbl + 1
    blk = lambda: pl.BlockSpec((1, TOKEN_BLOCK, W_GROUP), lambda i, t: (i, t, 0))
    return pl.pallas_call(
        functools.partial(_hy_conv_kernel, nbl=nbl),
        out_shape=[jax.ShapeDtypeStruct((b, lt, W_GROUP), F32)] * 3,
        grid=(b, nb),
        in_specs=[pl.BlockSpec((1, TOKEN_BLOCK, width), lambda i, t: (i, t, 0))]
                 + _halo_specs(width, nbl, lambda s: s, lt)
                 + [pl.BlockSpec((3, width), lambda i, t: (0, 0))],
        out_specs=[blk(), blk(), blk()],
        scratch_shapes=[pltpu.VMEM((TOKEN_BLOCK + 16, width), F32)],
        compiler_params=_cparams(2), name="hy_short_conv",
    )(p_hy, p_hy, p_hy, conv_w)


def _hy_filter_kernel(z_ref, w1_ref, b1_ref, f1_ref, w2_ref, b2_ref, f2_ref, w3_ref, dec_ref, o_ref):
    z = z_ref[...]
    h = jnp.sin(f1_ref[...] * (_hp_dot(z, w1_ref[...]) + b1_ref[...]))
    h = jnp.sin(f2_ref[...] * (_hp_dot(h, w2_ref[...]) + b2_ref[...]))
    h = _hp_dot(h, w3_ref[...])
    o_ref[...] = h * jnp.exp(-z[:, 0:1] * jnp.abs(dec_ref[...]))


def _hy_filters(zfeat, w1p, b1, f1, w2, b2, f2, w3, dec_row):
    n = zfeat.shape[0]
    tn = min(n, 512)
    full = lambda a: pl.BlockSpec(a.shape, lambda i: (0,) * a.ndim)
    args = [w1p, b1, f1, w2, b2, f2, w3, dec_row]
    return pl.pallas_call(
        _hy_filter_kernel,
        out_shape=jax.ShapeDtypeStruct((n, 4 * W_GROUP), F32),
        grid=(n // tn,),
        in_specs=[pl.BlockSpec((tn, 128), lambda i: (i, 0))] + [full(a) for a in args],
        out_specs=pl.BlockSpec((tn, 4 * W_GROUP), lambda i: (i, 0)),
        compiler_params=_cparams(1), name="hy_filters",
    )(zfeat, *args)


def _fft_stage1_kernel(a_ref, u_ref, o_ref):
    o_ref[0] = _hp_dot(a_ref[...], u_ref[0])


def _fft_stage1(a_mat, u_view, row_blk, col_tile):
    nbatch, _, cols = u_view.shape
    m, kdim = a_mat.shape
    return pl.pallas_call(
        _fft_stage1_kernel,
        out_shape=jax.ShapeDtypeStruct((nbatch, m, cols), F32),
        grid=(nbatch, cols // col_tile),
        in_specs=[pl.BlockSpec((m, kdim), lambda i, j: (0, 0)),
                  pl.BlockSpec((1, kdim, col_tile), lambda i, j: (i, row_blk, j))],
        out_specs=pl.BlockSpec((1, m, col_tile), lambda i, j: (i, 0, j)),
        compiler_params=_cparams(2), name="fft_stage1",
    )(a_mat, u_view)


def _fft_mid_kernel(*refs, n2, multiply):
    if multiply:
        y_ref, g_ref, kf_ref, gi_ref, o_ref = refs
    else:
        y_ref, g_ref, o_ref = refs
    y = y_ref[0].reshape(2 * n2, W_GROUP)
    z = _hp_dot(g_ref[0], y)
    if multiply:
        zr, zi = z[:n2], z[n2:]
        kr, ki = kf_ref[0, 0, 0], kf_ref[0, 1, 0]
        pr = zr * kr - zi * ki
        pi = zr * ki + zi * kr
        z = _hp_dot(gi_ref[0], jnp.concatenate([pr, pi], axis=0))
    o_ref[0] = z.reshape(2, 1, n2, W_GROUP)


def _fft_mid(y1, g_fwd, kf, g_inv, n1, n2):
    nbatch = y1.shape[0]
    multiply = kf is not None
    in_specs = [pl.BlockSpec((1, 2, 1, n2, W_GROUP), lambda k, i: (i, 0, k, 0, 0)),
                pl.BlockSpec((1, 2 * n2, 2 * n2), lambda k, i: (k, 0, 0))]
    args = [y1, g_fwd]
    if multiply:
        in_specs += [pl.BlockSpec((1, 2, 1, n2, W_GROUP), lambda k, i: (0, 0, k, 0, 0)),
                     pl.BlockSpec((1, 2 * n2, 2 * n2), lambda k, i: (k, 0, 0))]
        args += [kf, g_inv]
    return pl.pallas_call(
        functools.partial(_fft_mid_kernel, n2=n2, multiply=multiply),
        out_shape=jax.ShapeDtypeStruct((nbatch, 2, n1, n2, W_GROUP), F32),
        grid=(n1, nbatch),
        in_specs=in_specs,
        out_specs=pl.BlockSpec((1, 2, 1, n2, W_GROUP), lambda k, i: (i, 0, k, 0, 0)),
        compiler_params=_cparams(2), name="fft_mid_mul" if multiply else "fft_mid",
    )(*args)


def _fft_stage4_kernel(a_ref, v_ref, u_ref, gate_ref, skip_ref, o_ref):
    y = _hp_dot(a_ref[...], v_ref[0])
    o_ref[0] = gate_ref[0] * (y + skip_ref[...] * u_ref[0])


def _fft_stage4(a_mat, v_view, u_view, u_blk, gate_view, gate_blk, skip_row, col_tile):
    nbatch, kdim, cols = v_view.shape
    m = a_mat.shape[0]
    return pl.pallas_call(
        _fft_stage4_kernel,
        out_shape=jax.ShapeDtypeStruct((nbatch, m, cols), F32),
        grid=(nbatch, cols // col_tile),
        in_specs=[pl.BlockSpec((m, kdim), lambda i, j: (0, 0)),
                  pl.BlockSpec((1, kdim, col_tile), lambda i, j: (i, 0, j)),
                  pl.BlockSpec((1, m, col_tile), lambda i, j: (i, u_blk, j)),
                  pl.BlockSpec((1, m, col_tile), lambda i, j: (i, gate_blk, j)),
                  pl.BlockSpec((1, col_tile), lambda i, j: (0, j))],
        out_specs=pl.BlockSpec((1, m, col_tile), lambda i, j: (i, 0, j)),
        compiler_params=_cparams(2), name="fft_stage4",
    )(a_mat, v_view, u_view, gate_view, skip_row)


def _fft_tables(n, n2):
    big = 2 * n
    n1 = big // n2
    two_pi = 2.0 * math.pi

    def cs(num, den):
        ang = (num % den).astype(F32) * (two_pi / den)
        return jnp.cos(ang), jnp.sin(ang)

    k1 = jnp.arange(n1, dtype=jnp.int32)
    c, s = cs(k1[:, None] * k1[None, :], n1)
    a1_full = jnp.concatenate([c, -s], axis=0)
    k2 = jnp.arange(n2, dtype=jnp.int32)
    num = (k2[None, :, None] * k2[None, None, :] * n1 + k1[:, None, None] * k2[None, None, :]) % big
    c, s = cs(num, big)
    g_fwd = jnp.concatenate([jnp.concatenate([c, s], axis=2), jnp.concatenate([-s, c], axis=2)], axis=1)
    num = (k2[None, :, None] * k2[None, None, :] * n1 + k1[:, None, None] * k2[None, :, None]) % big
    c, s = cs(num, big)
    g_inv = jnp.concatenate([jnp.concatenate([c, -s], axis=2), jnp.concatenate([s, c], axis=2)], axis=1)
    t1 = jnp.arange(n1 // 2, dtype=jnp.int32)
    c, s = cs(t1[:, None] * k1[None, :], n1)
    a4 = jnp.concatenate([c, -s], axis=1) * (1.0 / big)
    return n1, a1_full, g_fwd, g_inv, a4


def _hy_positional(n):
    pos = jnp.arange(n, dtype=F32)
    t01 = pos / max(n - 1, 1)
    bands = jnp.linspace(1e-4, HY_BANDS - 1, HY_BANDS, dtype=F32)
    ang = (2.0 * math.pi / n) * pos[:, None] * bands
    z = jnp.concatenate([t01[:, None], jnp.cos(ang), -jnp.sin(ang)], axis=-1)
    return jnp.pad(z, ((0, 0), (0, 128 - z.shape[1])))


def _hyena_segment(v_all, x1_all, x2_all, seg_start, n, n2, filt_args, skip):
    b, lt, c = v_all.shape
    n1, a1_full, g_fwd, g_inv, a4 = _fft_tables(n, n2)
    a1_half = a1_full[:, :n1 // 2]
    cols = n2 * c
    col_tile = min(cols, 8192)
    seg_blk = seg_start // n
    view = lambda a: a.reshape(b, lt // n2, cols)

    h = _hy_filters(_hy_positional(n), *filt_args)
    h = h.reshape(n, 2, 2, c)
    kern = jnp.concatenate([h[:, 0], jnp.zeros((1, 2, c), F32), jnp.flip(h[1:, 1], axis=0)], axis=0)
    kern = jnp.moveaxis(kern, 1, 0)
    ky = _fft_stage1(a1_full, kern.reshape(2, n1, cols), 0, col_tile)
    kf = _fft_mid(ky.reshape(2, 2, n1, n2, c), g_fwd, None, None, n1, n2)

    def long_conv(u_view, u_blk, gate_all, filt):
        y1 = _fft_stage1(a1_half, u_view, u_blk, col_tile)
        vmid = _fft_mid(y1.reshape(b, 2, n1, n2, c), g_fwd, kf[filt:filt + 1], g_inv, n1, n2)
        skip_row = jnp.tile(skip[filt], n2).reshape(1, cols)
        return _fft_stage4(a4, vmid.reshape(b, 2 * n1, cols), u_view, u_blk, view(gate_all), seg_blk,
                           skip_row, col_tile)

    z = long_conv(view(v_all), seg_blk, x1_all, 0)
    out = long_conv(z, 0, x2_all, 1)
    return out.reshape(b, n, c)


def _rope_tables(rows, lt):
    n_freq = DA_DK // 4
    inv = ROPE_THETA ** (-jnp.arange(n_freq, dtype=F32) / n_freq)
    r = jnp.repeat(jnp.arange(rows, dtype=F32), GRID_W)
    col = jnp.tile(jnp.arange(GRID_W, dtype=F32), rows)
    ang_r, ang_c = r[:, None] * inv, col[:, None] * inv
    cos32 = jnp.concatenate([jnp.cos(ang_r)] * 2 + [jnp.cos(ang_c)] * 2, axis=-1)
    sin32 = jnp.concatenate([-jnp.sin(ang_r), jnp.sin(ang_r), -jnp.sin(ang_c), jnp.sin(ang_c)], axis=-1)
    n = rows * GRID_W
    pad = lt - n
    cos = jnp.concatenate([jnp.tile(cos32, (1, W_GROUP // DA_DK)), jnp.ones((pad, W_GROUP), F32)], axis=0)
    sin = jnp.concatenate([jnp.tile(sin32, (1, W_GROUP // DA_DK)), jnp.zeros((pad, W_GROUP), F32)], axis=0)
    return cos, sin


def kernel(x, c, ctx, c_ctx, mod_w, mod_b, ln1_w, ln2_w, w_in, gdn_conv_w, gdn_a_log, gdn_dt_bias, gdn_norm_w, hy_conv_w, hy_w1, hy_b1, hy_f1, hy_w2, hy_b2, hy_f2, hy_w3, hy_decay, hy_bias, hg_lb_raw, hg_norm_w, da_q_norm, da_k_norm, da_lam, da_subln, w_out, mlp_w1, mlp_w2):
    batch, seq, d = x.shape
    n_ctx = ctx.shape[1]
    depth = mod_w.shape[0]
    assert n_ctx == TOKEN_BLOCK and seq % TOKEN_BLOCK == 0 and d == 4 * W_GROUP
    nbl = seq // TOKEN_BLOCK
    lt = seq + n_ctx

    xs = jnp.concatenate([x, ctx], axis=1)
    cond = jnp.concatenate([c, c_ctx[None, :], jnp.zeros((-(batch + 1) % 8, d), F32)], axis=0)
    mods = _modulation(cond, mod_w, mod_b)[:, :batch + 1].reshape(depth, batch + 1, 6, d)
    mods = jnp.pad(mods, ((0, 0), (0, 0), (0, 2), (0, 0)))

    lb_all = jnp.cumsum(jax.nn.softmax(hg_lb_raw.astype(F32), axis=0), axis=0)
    lb_all = lb_all - lb_all[0]
    cos_tab, sin_tab = _rope_tables(seq // GRID_W, lt)

    gdn_cols = 4 * W_GROUP + 4 * N_HEADS
    hy_cols, hg_cols, da_cols = 3 * W_GROUP, 5 * W_GROUP, 3 * W_GROUP
    splits = (GDN_PAD_COLS, hy_cols, hg_cols, da_cols)

    for l in range(depth):
        want_ctx = l < depth - 1
        modtab = mods[l]
        wl = w_in[l]
        w_pad = jnp.concatenate([wl[:, :gdn_cols], jnp.zeros((d, GDN_PAD_COLS - gdn_cols), F32),
                                 wl[:, gdn_cols:]], axis=1).astype(BF16)
        p_gdn, p_hy, p_hg, p_da = _adaln_proj(xs, modtab, ln1_w[l], w_pad, splits, (F32,) * 4, 0, 1, False,
                                              nbl, "in_proj")

        pad16 = lambda a: jnp.pad(a.reshape(1, 2 * N_HEADS), ((0, 0), (0, 128 - 2 * N_HEADS)))
        alog_row, dtb_row = pad16(gdn_a_log[l]), pad16(gdn_dt_bias[l])
        gnw = jnp.tile(gdn_norm_w[l], N_HEADS).reshape(1, W_GROUP)
        o_b = _gdn_direction(p_gdn, gdn_conv_w[l], alog_row, dtb_row, gnw, None, 1, nbl)
        o_gdn = _gdn_direction(p_gdn, gdn_conv_w[l], alog_row, dtb_row, gnw, o_b, 0, nbl)

        hv, hx1, hx2 = _hy_short_conv(p_hy, hy_conv_w[l], nbl)
        w1p = jnp.pad(hy_w1[l], ((0, 128 - hy_w1.shape[1]), (0, 0)))
        row = lambda a: a.reshape(1, -1)
        dec_row = jnp.tile(hy_decay[l].reshape(1, 2 * W_GROUP), (1, 2))
        filt_args = (w1p, row(hy_b1[l]), row(hy_f1[l]), hy_w2[l], row(hy_b2[l]), row(hy_f2[l]), hy_w3[l], dec_row)
        o_hy_lat = _hyena_segment(hv, hx1, hx2, 0, seq, FFT_N2_LAT, filt_args, hy_bias[l])
        if want_ctx:
            o_hy_ctx = _hyena_segment(hv, hx1, hx2, seq, n_ctx, FFT_N2_CTX, filt_args, hy_bias[l])
        else:
            o_hy_ctx = jnp.zeros((batch, n_ctx, W_GROUP), F32)
        o_hy = jnp.concatenate([o_hy_lat, o_hy_ctx], axis=1)

        lb_row = lb_all[l].reshape(1, W_GROUP)
        hnw = jnp.tile(hg_norm_w[l], N_HEADS).reshape(1, W_GROUP)
        o_hb = _hgrn_direction(p_hg, lb_row, hnw, None, 1, nbl)
        o_hg = _hgrn_direction(p_hg, lb_row, hnw, o_hb, 0, nbl)

        lam_init = 0.8 - 0.6 * math.exp(-0.3 * l)
        qw = jnp.tile(da_q_norm[l], W_GROUP // DA_DK).reshape(1, W_GROUP)
        kw = jnp.tile(da_k_norm[l], W_GROUP // DA_DK).reshape(1, W_GROUP)
        sw = jnp.tile(da_subln[l], N_HEADS).reshape(1, W_GROUP)
        qh, kh, vh = _da_prep(p_da, cos_tab, sin_tab, qw, kw)
        o_da = _da_attention(qh, kh, vh, da_lam[l], sw, lam_init, nbl, 0, lt, 0, "da_attn_lat")
        if want_ctx:
            o_da_ctx = _da_attention(qh, kh, vh, da_lam[l], sw, lam_init, 1, nbl, TOKEN_BLOCK, nbl, "da_attn_ctx")
        else:
            o_da_ctx = jnp.zeros((batch, n_ctx, W_GROUP), F32)
        o_da = jnp.concatenate([o_da, o_da_ctx], axis=1)

        n_blocks = nbl + 1 if want_ctx else nbl
        xs_mid = _proj_residual(xs, modtab, w_out[l].astype(BF16), [o_gdn, o_hy, o_hg, o_da], 2, n_blocks,
                                nbl, "out_proj")
        (hid,) = _adaln_proj(xs_mid, modtab, ln2_w[l], mlp_w1[l].astype(BF16), (mlp_w1.shape[2],), (BF16,),
                             3, 4, True, nbl, "mlp_up")
        xs = _proj_residual(xs_mid, modtab, mlp_w2[l].astype(BF16), [hid], 5, n_blocks, nbl, "mlp_down")
    return xs
```

```python
import functools
import math

import jax
import jax.numpy as jnp
from jax import lax
from jax.experimental import pallas as pl
from jax.experimental.pallas import tpu as pltpu

F32 = jnp.float32
BF16 = jnp.bfloat16

EPS = 1e-6
TOKEN_BLOCK = 256
CHUNK = 64
HEAD_DIM = 64
N_HEADS = 4
W_GROUP = 256
DA_DK = 32
GRID_W = 64
ROPE_THETA = 10000.0
HY_BANDS = 16
FFT_N2_LAT = 128
FFT_N2_CTX = 32
GDN_PAD_COLS = 1152
VMEM_LIMIT_BYTES = 56 * 1024 * 1024


def _cparams(n_axes):
    return pltpu.CompilerParams(dimension_semantics=("arbitrary",) * n_axes,
                                vmem_limit_bytes=VMEM_LIMIT_BYTES)


def _sigmoid(x):
    return 1.0 / (1.0 + jnp.exp(-x))


def _silu(x):
    return x * _sigmoid(x)


def _softplus(x):
    return jnp.maximum(x, 0.0) + jnp.log(1.0 + jnp.exp(-jnp.abs(x)))


def _log_sigmoid(x):
    return jnp.minimum(x, 0.0) - jnp.log(1.0 + jnp.exp(-jnp.abs(x)))


def _bdot(a, b):
    return jnp.dot(a.astype(BF16), b.astype(BF16), preferred_element_type=F32)


def _bdot_nt(a, b):
    return lax.dot_general(a.astype(BF16), b.astype(BF16), (((1,), (1,)), ((), ())),
                           preferred_element_type=F32)


def _split3(x):
    hi = x.astype(BF16)
    r = x - hi.astype(F32)
    mid = r.astype(BF16)
    lo = (r - mid.astype(F32)).astype(BF16)
    return hi, mid, lo


def _exact_left(m_bf16, x):
    hi, mid, lo = _split3(x)
    d = lambda p: jnp.dot(m_bf16, p, preferred_element_type=F32)
    return d(hi) + d(mid) + d(lo)


def _exact_right(x, m_bf16):
    hi, mid, lo = _split3(x)
    d = lambda p: jnp.dot(p, m_bf16, preferred_element_type=F32)
    return d(hi) + d(mid) + d(lo)


def _hp_dot(a, b):
    ah = a.astype(BF16)
    al = (a - ah.astype(F32)).astype(BF16)
    bh = b.astype(BF16)
    bl = (b - bh.astype(F32)).astype(BF16)
    d = lambda p, q: jnp.dot(p, q, preferred_element_type=F32)
    return d(ah, bh) + d(ah, bl) + d(al, bh)


def _group_ones(width, group):
    r = lax.broadcasted_iota(jnp.int32, (width, width), 0) // group
    c = lax.broadcasted_iota(jnp.int32, (width, width), 1) // group
    return (r == c).astype(BF16)


def _mod_kernel(s_ref, w_ref, b_ref, o_ref):
    s = _silu(s_ref[...])
    o_ref[0] = _bdot(s, w_ref[0]) + b_ref[0]


def _modulation(cond, mod_w, mod_b):
    depth, d, n = mod_w.shape
    rows = cond.shape[0]
    tn = 1536
    return pl.pallas_call(
        _mod_kernel,
        out_shape=jax.ShapeDtypeStruct((depth, rows, n), F32),
        grid=(depth, n // tn),
        in_specs=[pl.BlockSpec((rows, d), lambda l, j: (0, 0)),
                  pl.BlockSpec((1, d, tn), lambda l, j: (l, 0, j)),
                  pl.BlockSpec((1, 1, tn), lambda l, j: (l, 0, j))],
        out_specs=pl.BlockSpec((1, rows, tn), lambda l, j: (l, 0, j)),
        compiler_params=_cparams(2), name="modulation",
    )(cond, mod_w, mod_b.reshape(depth, 1, n))


def _adaln_proj_kernel(x_ref, mod_ref, lnw_ref, w_ref, *out_refs, shift_row, scale_row, splits, sq_relu):
    x = x_ref[0]
    ms = jnp.mean(x * x, axis=-1, keepdims=True)
    y = x * lax.rsqrt(ms + EPS) * lnw_ref[...]
    mod = mod_ref[0]
    y = y * (1.0 + mod[scale_row:scale_row + 1]) + mod[shift_row:shift_row + 1]
    h = _bdot(y, w_ref[...])
    if sq_relu:
        h = jnp.square(jnp.maximum(h, 0.0))
    off = 0
    for o_ref, n in zip(out_refs, splits):
        o_ref[0] = h[:, off:off + n].astype(o_ref.dtype)
        off += n


def _mod_index(nbl, batch):
    return lambda b, t: (jnp.where(t == nbl, batch, b), 0, 0)


def _adaln_proj(xs, modtab, ln_w, w, splits, out_dtypes, shift_row, scale_row, sq_relu, nbl, name):
    b, lt, d = xs.shape
    nb = lt // TOKEN_BLOCK
    n = w.shape[1]
    kern = functools.partial(_adaln_proj_kernel, shift_row=shift_row, scale_row=scale_row,
                             splits=splits, sq_relu=sq_relu)
    return pl.pallas_call(
        kern,
        out_shape=[jax.ShapeDtypeStruct((b, lt, s), dt) for s, dt in zip(splits, out_dtypes)],
        grid=(b, nb),
        in_specs=[pl.BlockSpec((1, TOKEN_BLOCK, d), lambda i, t: (i, t, 0)),
                  pl.BlockSpec((1, 8, d), _mod_index(nbl, b)),
                  pl.BlockSpec((1, d), lambda i, t: (0, 0)),
                  pl.BlockSpec((d, n), lambda i, t: (0, 0))],
        out_specs=[pl.BlockSpec((1, TOKEN_BLOCK, s), lambda i, t: (i, t, 0)) for s in splits],
        compiler_params=_cparams(2), name=name,
    )(xs, modtab, ln_w.reshape(1, d), w)


def _proj_residual_kernel(x_ref, mod_ref, w_ref, *rest, gate_row):
    a_refs, o_ref = rest[:-1], rest[-1]
    a = jnp.concatenate([r[0].astype(BF16) for r in a_refs], axis=1) if len(a_refs) > 1 else a_refs[0][0]
    gate = mod_ref[0][gate_row:gate_row + 1]
    o_ref[0] = x_ref[0] + gate * _bdot(a, w_ref[...])


def _proj_residual(xs, modtab, w, acts, gate_row, n_blocks, nbl, name):
    b, lt, d = xs.shape
    kern = functools.partial(_proj_residual_kernel, gate_row=gate_row)
    return pl.pallas_call(
        kern,
        out_shape=jax.ShapeDtypeStruct((b, n_blocks * TOKEN_BLOCK, d), F32),
        grid=(b, n_blocks),
        in_specs=[pl.BlockSpec((1, TOKEN_BLOCK, d), lambda i, t: (i, t, 0)),
                  pl.BlockSpec((1, 8, d), _mod_index(nbl, b)),
                  pl.BlockSpec(w.shape, lambda i, t: (0, 0))]
                 + [pl.BlockSpec((1, TOKEN_BLOCK, a.shape[2]), lambda i, t: (i, t, 0)) for a in acts],
        out_specs=pl.BlockSpec((1, TOKEN_BLOCK, d), lambda i, t: (i, t, 0)),
        compiler_params=_cparams(2), name=name,
    )(xs, modtab, w, *acts)


def _scan_block(step, nbl, direction):
    lat = step - 1 if direction == 0 else nbl - step
    return jnp.where(step == 0, nbl, lat)


def _halo_specs(width, nbl, block_of_step, lt):
    rows8 = TOKEN_BLOCK // 8

    def prev_map(b, s):
        return (b, jnp.maximum(block_of_step(s) * rows8 - 1, 0), 0)

    def next_map(b, s):
        return (b, jnp.minimum((block_of_step(s) + 1) * rows8, lt // 8 - 1), 0)

    return [pl.BlockSpec((1, 8, width), prev_map), pl.BlockSpec((1, 8, width), next_map)]


def _short_conv(cur, hp_ref, hn_ref, cw_ref, xp_ref, blk, nbl, width):
    seg_first = jnp.logical_or(blk == 0, blk == nbl)
    seg_last = jnp.logical_or(blk == nbl - 1, blk == nbl)
    xp_ref[0:8, :] = jnp.where(seg_first, 0.0, hp_ref[0][:, :width])
    xp_ref[8:8 + TOKEN_BLOCK, :] = cur
    xp_ref[8 + TOKEN_BLOCK:16 + TOKEN_BLOCK, :] = jnp.where(seg_last, 0.0, hn_ref[0][:, :width])
    cw = cw_ref[...]
    return (cw[0:1] * xp_ref[7:7 + TOKEN_BLOCK, :] + cw[1:2] * xp_ref[8:8 + TOKEN_BLOCK, :]
            + cw[2:3] * xp_ref[9:9 + TOKEN_BLOCK, :])


def _scan_tables(direction, chunk):
    i = jnp.arange(TOKEN_BLOCK, dtype=jnp.int32)[:, None]
    j = jnp.arange(TOKEN_BLOCK, dtype=jnp.int32)[None, :]
    same = (i // chunk) == (j // chunk)
    pos = (lambda t: t % chunk) if direction == 0 else (lambda t: chunk - 1 - t % chunk)
    pi, pj = pos(i), pos(j)
    eye = i == j
    fl, bl = [same & (pj <= pi), same & (pj < pi), eye], [same & (pj <= pi), (i // HEAD_DIM) == (j // HEAD_DIM)]
    m = 1
    while m < chunk:
        fl.append(same & ((pi // (2 * m)) == (pj // (2 * m))) & ((pi // m) % 2 == 1) & ((pj // m) % 2 == 0))
        boundary = (pi // (2 * m)) * (2 * m) + m - 1
        bl.append(eye.astype(F32) - (same & (pj == boundary)).astype(F32))
        m *= 2
    return jnp.stack([a.astype(F32) for a in fl]), jnp.stack([a.astype(BF16) for a in bl])


def _group_rms(o, gmat, group, weight):
    ms = jnp.dot((o * o).astype(BF16), gmat, preferred_element_type=F32) * (1.0 / group)
    return o * lax.rsqrt(ms + EPS) * weight


def _gdn_kernel(*refs, direction, finish, nbl):
    if finish:
        (p_ref, hp_ref, hn_ref, cw_ref, alog_ref, dtb_ref, nw_ref, fm_ref, bm_ref, other_ref,
         o_ref, s_ref, xp_ref, ob_ref) = refs
    else:
        p_ref, hp_ref, hn_ref, cw_ref, alog_ref, dtb_ref, nw_ref, fm_ref, bm_ref, o_ref, s_ref, xp_ref, ob_ref = refs
    step = pl.program_id(1)
    blk = _scan_block(step, nbl, direction)

    @pl.when(step == 0)
    def _():
        s_ref[...] = jnp.zeros_like(s_ref)

    p = p_ref[0]
    xc = _silu(_short_conv(p[:, :3 * W_GROUP], hp_ref, hn_ref, cw_ref, xp_ref, blk, nbl, 3 * W_GROUP))
    g64 = bm_ref[1]
    q = xc[:, :W_GROUP]
    k = xc[:, W_GROUP:2 * W_GROUP]
    v = xc[:, 2 * W_GROUP:]
    q = q * lax.rsqrt(_bdot(q * q, g64) + EPS) * (HEAD_DIM ** -0.5)
    k = k * lax.rsqrt(_bdot(k * k, g64) + EPS)
    k_t = k.T

    ab = p[:, 4 * W_GROUP:4 * W_GROUP + 128]
    lane = lax.broadcasted_iota(jnp.int32, ab.shape, 1)
    g_all = -jnp.exp(alog_ref[...]) * _softplus(ab + dtb_ref[...])
    w_all = jnp.where(lane < 2 * N_HEADS, g_all, _sigmoid(ab))
    z_col = jnp.where(lane < 2 * N_HEADS, _exact_left(bm_ref[0], w_all), w_all)
    z_row = z_col.T

    heads = range(N_HEADS)
    lanes = [slice(h * HEAD_DIM, (h + 1) * HEAD_DIM) for h in heads]
    gidx = [N_HEADS * direction + h for h in heads]
    incl = fm_ref[0] > 0.5
    gc_r = [z_row[gi:gi + 1, :] for gi in gidx]
    beta = [z_col[:, 2 * N_HEADS + gi:2 * N_HEADS + gi + 1] for gi in gidx]
    eg = [jnp.exp(z_col[:, gi:gi + 1]) for gi in gidx]
    decay = [jnp.exp(jnp.where(incl, z_col[:, gi:gi + 1] - gc_r[h], -jnp.inf)) for h, gi in enumerate(gidx)]
    kk = [_bdot_nt(k[:, ls], k[:, ls]) for ls in lanes]
    qk = [_bdot_nt(q[:, ls], k[:, ls]) for ls in lanes]
    n = [kk[h] * beta[h] * decay[h] * fm_ref[1] for h in heads]
    inv = [fm_ref[2] - n[h] * fm_ref[3] for h in heads]
    for lev in range(1, 6):
        half = fm_ref[3 + lev]
        x = [_bdot(inv[h], n[h] * half) for h in heads]
        inv = [inv[h] - _bdot(x[h], inv[h]) for h in heads]
    uw = [_bdot(inv[h], jnp.concatenate([v[:, ls] * beta[h], k[:, ls] * (beta[h] * eg[h])], axis=1))
          for h, ls in enumerate(lanes)]
    attn = [qk[h] * decay[h] for h in heads]
    qg = [q[:, ls] * eg[h] for h, ls in enumerate(lanes)]
    state = [s_ref[h] for h in heads]

    last = CHUNK - 1 if direction == 0 else 0
    n_chunks = TOKEN_BLOCK // CHUNK
    for ci in range(n_chunks):
        c = ci if direction == 0 else n_chunks - 1 - ci
        rs = slice(c * CHUNK, (c + 1) * CHUNK)
        g_last = [z_col[c * CHUNK + last:c * CHUNK + last + 1, gi:gi + 1] for gi in gidx]
        v_new = [uw[h][rs, :HEAD_DIM] - _bdot(uw[h][rs, HEAD_DIM:], state[h]) for h in heads]
        for h, ls in enumerate(lanes):
            ob_ref[rs, ls] = _bdot(qg[h][rs], state[h]) + _bdot(attn[h][rs, rs], v_new[h])
        state = [state[h] * jnp.exp(g_last[h])
                 + _bdot(k_t[ls, rs] * jnp.exp(g_last[h] - gc_r[h][:, rs]), v_new[h]) for h, ls in enumerate(lanes)]
    for h in heads:
        s_ref[h] = state[h]

    if finish:
        o = ob_ref[...] + other_ref[0]
        gate = _silu(p[:, 3 * W_GROUP:4 * W_GROUP])
        o_ref[0] = _group_rms(o, g64, HEAD_DIM, nw_ref[...]) * gate
    else:
        o_ref[0] = ob_ref[...]


def _gdn_direction(p_gdn, conv_w, alog_row, dtb_row, nw_row, other, direction, nbl):
    b, lt, width = p_gdn.shape
    nb = nbl + 1
    blk_of = lambda s: _scan_block(s, nbl, direction)
    finish = other is not None
    fmask, bmask = _scan_tables(direction, CHUNK)
    bmask = bmask[:2]
    kern = functools.partial(_gdn_kernel, direction=direction, finish=finish, nbl=nbl)
    in_specs = ([pl.BlockSpec((1, TOKEN_BLOCK, width), lambda i, s: (i, blk_of(s), 0))]
                + _halo_specs(3 * W_GROUP, nbl, blk_of, lt)
                + [pl.BlockSpec((3, 3 * W_GROUP), lambda i, s: (0, 0)),
                   pl.BlockSpec((1, 128), lambda i, s: (0, 0)),
                   pl.BlockSpec((1, 128), lambda i, s: (0, 0)),
                   pl.BlockSpec((1, W_GROUP), lambda i, s: (0, 0)),
                   pl.BlockSpec(fmask.shape, lambda i, s: (0, 0, 0)),
                   pl.BlockSpec(bmask.shape, lambda i, s: (0, 0, 0))])
    args = [p_gdn, p_gdn, p_gdn, conv_w, alog_row, dtb_row, nw_row, fmask, bmask]
    if finish:
        in_specs.append(pl.BlockSpec((1, TOKEN_BLOCK, W_GROUP), lambda i, s: (i, blk_of(s), 0)))
        args.append(other)
    return pl.pallas_call(
        kern,
        out_shape=jax.ShapeDtypeStruct((b, lt, W_GROUP), F32),
        grid=(b, nb),
        in_specs=in_specs,
        out_specs=pl.BlockSpec((1, TOKEN_BLOCK, W_GROUP), lambda i, s: (i, blk_of(s), 0)),
        scratch_shapes=[pltpu.VMEM((N_HEADS, HEAD_DIM, HEAD_DIM), F32),
                        pltpu.VMEM((TOKEN_BLOCK + 16, 3 * W_GROUP), F32),
                        pltpu.VMEM((TOKEN_BLOCK, W_GROUP), F32)],
        compiler_params=_cparams(2), name=f"gdn_dir{direction}",
    )(*args)


def _hgrn_kernel(*refs, direction, finish, nbl):
    if finish:
        p_ref, lb_ref, nw_ref, fm_ref, bm_ref, other_ref, o_ref, s_ref, ob_ref = refs
    else:
        p_ref, lb_ref, nw_ref, fm_ref, bm_ref, o_ref, s_ref, ob_ref = refs
    step = pl.program_id(1)

    @pl.when(step == 0)
    def _():
        s_ref[...] = jnp.zeros_like(s_ref)

    p = p_ref[0]
    lb = lb_ref[...]
    q = _silu(p[:, :W_GROUP])
    v = p[:, W_GROUP:2 * W_GROUP]
    fl = p[:, (2 + direction) * W_GROUP:(3 + direction) * W_GROUP]
    a = jnp.log(lb)
    bb = jnp.log(1.0 - lb) + _log_sigmoid(fl)
    logf = jnp.maximum(a, bb) + jnp.log(1.0 + jnp.exp(-jnp.abs(a - bb)))
    key = (1.0 - lb) * _sigmoid(-fl)
    gcum = _exact_left(bm_ref[0], logf)
    pieces = _split3(gcum)
    n_levels = fm_ref.shape[0] - 3
    lev_e = []
    for lev in range(n_levels):
        dm = bm_ref[2 + lev]
        d = sum(jnp.dot(dm, piece, preferred_element_type=F32) for piece in pieces)
        lev_e.append(jnp.exp(-jnp.abs(d)))
    last = TOKEN_BLOCK - 1 if direction == 0 else 0
    g_end = gcum[last:last + 1, :]
    qg = q * jnp.exp(gcum)
    kdec = key * jnp.exp(g_end - gcum)
    decay_end = jnp.exp(g_end)
    v_t = v.T

    lanes = [slice(h * HEAD_DIM, (h + 1) * HEAD_DIM) for h in range(N_HEADS)]
    amat = [_bdot_nt(q[:, ls], key[:, ls]) * fm_ref[2] for ls in lanes]
    for lev in range(n_levels):
        qe, ke = q * lev_e[lev], key * lev_e[lev]
        half = fm_ref[3 + lev]
        amat = [amat[h] + half * _bdot_nt(qe[:, ls], ke[:, ls]) for h, ls in enumerate(lanes)]
    for h, ls in enumerate(lanes):
        st = s_ref[h]
        ob_ref[:, ls] = _bdot_nt(qg[:, ls], st) + _bdot(amat[h], v[:, ls])
        s_ref[h] = st * decay_end[:, ls] + _bdot(v_t[ls, :], kdec[:, ls])

    if finish:
        o = ob_ref[...] + other_ref[0]
        gate = _silu(p[:, 4 * W_GROUP:5 * W_GROUP])
        o_ref[0] = _group_rms(o, bm_ref[1], HEAD_DIM, nw_ref[...]) * gate
    else:
        o_ref[0] = ob_ref[...]


def _hgrn_direction(p_hg, lb_row, nw_row, other, direction, nbl):
    b, lt, width = p_hg.shape
    nb = nbl + 1
    blk_of = lambda s: _scan_block(s, nbl, direction)
    finish = other is not None
    fmask, bmask = _scan_tables(direction, TOKEN_BLOCK)
    kern = functools.partial(_hgrn_kernel, direction=direction, finish=finish, nbl=nbl)
    in_specs = [pl.BlockSpec((1, TOKEN_BLOCK, width), lambda i, s: (i, blk_of(s), 0)),
                pl.BlockSpec((1, W_GROUP), lambda i, s: (0, 0)),
                pl.BlockSpec((1, W_GROUP), lambda i, s: (0, 0)),
                pl.BlockSpec(fmask.shape, lambda i, s: (0, 0, 0)),
                pl.BlockSpec(bmask.shape, lambda i, s: (0, 0, 0))]
    args = [p_hg, lb_row, nw_row, fmask, bmask]
    if finish:
        in_specs.append(pl.BlockSpec((1, TOKEN_BLOCK, W_GROUP), lambda i, s: (i, blk_of(s), 0)))
        args.append(other)
    return pl.pallas_call(
        kern,
        out_shape=jax.ShapeDtypeStruct((b, lt, W_GROUP), F32),
        grid=(b, nb),
        in_specs=in_specs,
        out_specs=pl.BlockSpec((1, TOKEN_BLOCK, W_GROUP), lambda i, s: (i, blk_of(s), 0)),
        scratch_shapes=[pltpu.VMEM((N_HEADS, HEAD_DIM, HEAD_DIM), F32),
                        pltpu.VMEM((TOKEN_BLOCK, W_GROUP), F32)],
        compiler_params=_cparams(2), name=f"hgrn_dir{direction}",
    )(*args)


def _da_prep_kernel(p_ref, cos_ref, sin_ref, qw_ref, kw_ref, q_ref, k_ref, v_ref):
    p = p_ref[0]
    g32 = _group_ones(W_GROUP, DA_DK)
    lane = lax.broadcasted_iota(jnp.int32, (TOKEN_BLOCK, W_GROUP), 1)
    first_half = (lane % 16) < 8
    cos, sin = cos_ref[...], sin_ref[...]

    def norm_rope(x, w):
        ms = _bdot(x * x, g32) * (1.0 / DA_DK)
        y = x * lax.rsqrt(ms + EPS) * w
        partner = jnp.where(first_half, pltpu.roll(y, W_GROUP - 8, axis=1), pltpu.roll(y, 8, axis=1))
        return y * cos + partner * sin

    q_ref[0] = (norm_rope(p[:, :W_GROUP], qw_ref[...]) * (DA_DK ** -0.5)).astype(BF16)
    k_ref[0] = norm_rope(p[:, W_GROUP:2 * W_GROUP], kw_ref[...]).astype(BF16)
    src = lax.broadcasted_iota(jnp.int32, (W_GROUP, 2 * W_GROUP), 0)
    dst = lax.broadcasted_iota(jnp.int32, (W_GROUP, 2 * W_GROUP), 1)
    spread = (dst == (src // HEAD_DIM) * 128 + src % HEAD_DIM).astype(BF16)
    wide = jnp.dot(p[:, 2 * W_GROUP:].astype(BF16), spread, preferred_element_type=F32)
    lane_w = lax.broadcasted_iota(jnp.int32, wide.shape, 1)
    v_ref[0] = jnp.where(lane_w % 128 == HEAD_DIM, 1.0, wide).astype(BF16)


def _da_prep(p_da, cos_tab, sin_tab, qw_row, kw_row):
    b, lt, width = p_da.shape
    nb = lt // TOKEN_BLOCK
    blk = lambda w=W_GROUP: pl.BlockSpec((1, TOKEN_BLOCK, w), lambda i, t: (i, t, 0))
    return pl.pallas_call(
        _da_prep_kernel,
        out_shape=[jax.ShapeDtypeStruct((b, lt, W_GROUP), BF16)] * 2
                  + [jax.ShapeDtypeStruct((b, lt, 2 * W_GROUP), BF16)],
        grid=(b, nb),
        in_specs=[pl.BlockSpec((1, TOKEN_BLOCK, width), lambda i, t: (i, t, 0)),
                  pl.BlockSpec((TOKEN_BLOCK, W_GROUP), lambda i, t: (t, 0)),
                  pl.BlockSpec((TOKEN_BLOCK, W_GROUP), lambda i, t: (t, 0)),
                  pl.BlockSpec((1, W_GROUP), lambda i, t: (0, 0)),
                  pl.BlockSpec((1, W_GROUP), lambda i, t: (0, 0))],
        out_specs=[blk(), blk(), blk(2 * W_GROUP)],
        compiler_params=_cparams(2), name="da_prep",
    )(p_da, cos_tab, sin_tab, qw_row, kw_row)


def _da_attn_kernel(q_ref, k_ref, v_ref, lam_ref, sw_ref, o_ref, *, lam_init):
    q = q_ref[0]
    k = k_ref[0]
    lp = lam_ref[...]
    lam = (jnp.exp(jnp.sum(lp[0:1] * lp[1:2], axis=1, keepdims=True))
           - jnp.exp(jnp.sum(lp[2:3] * lp[3:4], axis=1, keepdims=True)) + lam_init)
    lane = lax.broadcasted_iota(jnp.int32, q.shape, 1)
    lane128 = lax.broadcasted_iota(jnp.int32, (q.shape[0], 128), 1)
    outs = []
    for h in range(N_HEADS):
        v_h = v_ref[0, :, h * 128:(h + 1) * 128]
        maps = []
        for j in range(2):
            lo = h * HEAD_DIM + j * DA_DK
            qm = jnp.where(jnp.logical_and(lane >= lo, lane < lo + DA_DK), q, jnp.zeros_like(q))
            s = lax.dot_general(qm, k, (((1,), (1,)), ((), ())), preferred_element_type=F32).astype(BF16)
            e = jnp.exp(s - jnp.max(s, axis=1, keepdims=True))
            acc = jnp.dot(e, v_h, preferred_element_type=F32)
            maps.append(acc / acc[:, HEAD_DIM:HEAD_DIM + 1])
        outs.append(maps[0] - lam * maps[1])
    pairs = [jnp.where(lane128 < HEAD_DIM, outs[2 * i], pltpu.roll(outs[2 * i + 1], HEAD_DIM, axis=1))
             for i in range(N_HEADS // 2)]
    o = jnp.concatenate(pairs, axis=1)
    g64 = _group_ones(W_GROUP, HEAD_DIM)
    o_ref[0] = _group_rms(o, g64, HEAD_DIM, sw_ref[...]) * (1.0 - lam_init)


def _da_attention(qh, kh, vh, lam_p, sw_row, lam_init, q_blocks, q_off, k_rows, k_blk, name):
    b, lt, _ = qh.shape
    kern = functools.partial(_da_attn_kernel, lam_init=lam_init)
    return pl.pallas_call(
        kern,
        out_shape=jax.ShapeDtypeStruct((b, q_blocks * TOKEN_BLOCK, W_GROUP), F32),
        grid=(b, q_blocks),
        in_specs=[pl.BlockSpec((1, TOKEN_BLOCK, W_GROUP), lambda i, t: (i, t + q_off, 0)),
                  pl.BlockSpec((1, k_rows, W_GROUP), lambda i, t: (i, k_blk, 0)),
                  pl.BlockSpec((1, k_rows, 2 * W_GROUP), lambda i, t: (i, k_blk, 0)),
                  pl.BlockSpec((4, DA_DK), lambda i, t: (0, 0)),
                  pl.BlockSpec((1, W_GROUP), lambda i, t: (0, 0))],
        out_specs=pl.BlockSpec((1, TOKEN_BLOCK, W_GROUP), lambda i, t: (i, t, 0)),
        compiler_params=_cparams(2), name=name,
    )(qh, kh, vh, lam_p, sw_row)


def _hy_conv_kernel(p_ref, hp_ref, hn_ref, cw_ref, v_ref, x1_ref, x2_ref, xp_ref, *, nbl):
    blk = pl.program_id(1)
    xc = _short_conv(p_ref[0], hp_ref, hn_ref, cw_ref, xp_ref, blk, nbl, 3 * W_GROUP)
    v_ref[0] = xc[:, :W_GROUP]
    x1_ref[0] = xc[:, W_GROUP:2 * W_GROUP]
    x2_ref[0] = xc[:, 2 * W_GROUP:]


def _hy_short_conv(p_hy, conv_w, nbl):
    b, lt, width = p_hy.shape
    nb = nbl + 1
    blk = lambda: pl.BlockSpec((1, TOKEN_BLOCK, W_GROUP), lambda i, t: (i, t, 0))
    return pl.pallas_call(
        functools.partial(_hy_conv_kernel, nbl=nbl),
        out_shape=[jax.ShapeDtypeStruct((b, lt, W_GROUP), F32)] * 3,
        grid=(b, nb),
        in_specs=[pl.BlockSpec((1, TOKEN_BLOCK, width), lambda i, t: (i, t, 0))]
                 + _halo_specs(width, nbl, lambda s: s, lt)
                 + [pl.BlockSpec((3, width), lambda i, t: (0, 0))],
        out_specs=[blk(), blk(), blk()],
        scratch_shapes=[pltpu.VMEM((TOKEN_BLOCK + 16, width), F32)],
        compiler_params=_cparams(2), name="hy_short_conv",
    )(p_hy, p_hy, p_hy, conv_w)


def _hy_filter_kernel(z_ref, w1_ref, b1_ref, f1_ref, w2_ref, b2_ref, f2_ref, w3_ref, dec_ref, o_ref):
    z = z_ref[...]
    h = jnp.sin(f1_ref[...] * (_hp_dot(z, w1_ref[...]) + b1_ref[...]))
    h = jnp.sin(f2_ref[...] * (_hp_dot(h, w2_ref[...]) + b2_ref[...]))
    h = _hp_dot(h, w3_ref[...])
    o_ref[...] = h * jnp.exp(-z[:, 0:1] * jnp.abs(dec_ref[...]))


def _hy_filters(zfeat, w1p, b1, f1, w2, b2, f2, w3, dec_row):
    n = zfeat.shape[0]
    tn = min(n, 512)
    full = lambda a: pl.BlockSpec(a.shape, lambda i: (0,) * a.ndim)
    args = [w1p, b1, f1, w2, b2, f2, w3, dec_row]
    return pl.pallas_call(
        _hy_filter_kernel,
        out_shape=jax.ShapeDtypeStruct((n, 4 * W_GROUP), F32),
        grid=(n // tn,),
        in_specs=[pl.BlockSpec((tn, 128), lambda i: (i, 0))] + [full(a) for a in args],
        out_specs=pl.BlockSpec((tn, 4 * W_GROUP), lambda i: (i, 0)),
        compiler_params=_cparams(1), name="hy_filters",
    )(zfeat, *args)


def _fft_stage1_kernel(a_ref, u_ref, o_ref):
    o_ref[0] = _hp_dot(a_ref[...], u_ref[0])


def _fft_stage1(a_mat, u_view, row_blk, col_tile):
    nbatch, _, cols = u_view.shape
    m, kdim = a_mat.shape
    return pl.pallas_call(
        _fft_stage1_kernel,
        out_shape=jax.ShapeDtypeStruct((nbatch, m, cols), F32),
        grid=(nbatch, cols // col_tile),
        in_specs=[pl.BlockSpec((m, kdim), lambda i, j: (0, 0)),
                  pl.BlockSpec((1, kdim, col_tile), lambda i, j: (i, row_blk, j))],
        out_specs=pl.BlockSpec((1, m, col_tile), lambda i, j: (i, 0, j)),
        compiler_params=_cparams(2), name="fft_stage1",
    )(a_mat, u_view)


def _fft_mid_kernel(*refs, n2, multiply):
    if multiply:
        y_ref, g_ref, kf_ref, gi_ref, o_ref = refs
    else:
        y_ref, g_ref, o_ref = refs
    y = y_ref[0].reshape(2 * n2, W_GROUP)
    z = _bdot(g_ref[0], y)
    if multiply:
        zr, zi = z[:n2], z[n2:]
        kr, ki = kf_ref[0, 0, 0], kf_ref[0, 1, 0]
        pr = zr * kr - zi * ki
        pi = zr * ki + zi * kr
        z = _bdot(gi_ref[0], jnp.concatenate([pr, pi], axis=0))
    o_ref[0] = z.reshape(2, 1, n2, W_GROUP)


def _fft_mid(y1, g_fwd, kf, g_inv, n1, n2):
    nbatch = y1.shape[0]
    multiply = kf is not None
    in_specs = [pl.BlockSpec((1, 2, 1, n2, W_GROUP), lambda k, i: (i, 0, k, 0, 0)),
                pl.BlockSpec((1, 2 * n2, 2 * n2), lambda k, i: (k, 0, 0))]
    args = [y1, g_fwd]
    if multiply:
        in_specs += [pl.BlockSpec((1, 2, 1, n2, W_GROUP), lambda k, i: (0, 0, k, 0, 0)),
                     pl.BlockSpec((1, 2 * n2, 2 * n2), lambda k, i: (k, 0, 0))]
        args += [kf, g_inv]
    return pl.pallas_call(
        functools.partial(_fft_mid_kernel, n2=n2, multiply=multiply),
        out_shape=jax.ShapeDtypeStruct((nbatch, 2, n1, n2, W_GROUP), F32),
        grid=(n1, nbatch),
        in_specs=in_specs,
        out_specs=pl.BlockSpec((1, 2, 1, n2, W_GROUP), lambda k, i: (i, 0, k, 0, 0)),
        compiler_params=_cparams(2), name="fft_mid_mul" if multiply else "fft_mid",
    )(*args)


def _fft_stage4_kernel(a_ref, v_ref, u_ref, gate_ref, skip_ref, o_ref):
    y = _hp_dot(a_ref[...], v_ref[0])
    o_ref[0] = gate_ref[0] * (y + skip_ref[...] * u_ref[0])


def _fft_stage4(a_mat, v_view, u_view, u_blk, gate_view, gate_blk, skip_row, col_tile):
    nbatch, kdim, cols = v_view.shape
    m = a_mat.shape[0]
    return pl.pallas_call(
        _fft_stage4_kernel,
        out_shape=jax.ShapeDtypeStruct((nbatch, m, cols), F32),
        grid=(nbatch, cols // col_tile),
        in_specs=[pl.BlockSpec((m, kdim), lambda i, j: (0, 0)),
                  pl.BlockSpec((1, kdim, col_tile), lambda i, j: (i, 0, j)),
                  pl.BlockSpec((1, m, col_tile), lambda i, j: (i, u_blk, j)),
                  pl.BlockSpec((1, m, col_tile), lambda i, j: (i, gate_blk, j)),
                  pl.BlockSpec((1, col_tile), lambda i, j: (0, j))],
        out_specs=pl.BlockSpec((1, m, col_tile), lambda i, j: (i, 0, j)),
        compiler_params=_cparams(2), name="fft_stage4",
    )(a_mat, v_view, u_view, gate_view, skip_row)


def _fft_tables(n, n2):
    big = 2 * n
    n1 = big // n2
    two_pi = 2.0 * math.pi

    def cs(num, den):
        ang = (num % den).astype(F32) * (two_pi / den)
        return jnp.cos(ang), jnp.sin(ang)

    k1 = jnp.arange(n1, dtype=jnp.int32)
    c, s = cs(k1[:, None] * k1[None, :], n1)
    a1_full = jnp.concatenate([c, -s], axis=0)
    k2 = jnp.arange(n2, dtype=jnp.int32)
    num = (k2[None, :, None] * k2[None, None, :] * n1 + k1[:, None, None] * k2[None, None, :]) % big
    c, s = cs(num, big)
    g_fwd = jnp.concatenate([jnp.concatenate([c, s], axis=2), jnp.concatenate([-s, c], axis=2)], axis=1)
    num = (k2[None, :, None] * k2[None, None, :] * n1 + k1[:, None, None] * k2[None, :, None]) % big
    c, s = cs(num, big)
    g_inv = jnp.concatenate([jnp.concatenate([c, -s], axis=2), jnp.concatenate([s, c], axis=2)], axis=1)
    t1 = jnp.arange(n1 // 2, dtype=jnp.int32)
    c, s = cs(t1[:, None] * k1[None, :], n1)
    a4 = jnp.concatenate([c, -s], axis=1) * (1.0 / big)
    return n1, a1_full, g_fwd.astype(BF16), g_inv.astype(BF16), a4


def _hy_positional(n):
    pos = jnp.arange(n, dtype=F32)
    t01 = pos / max(n - 1, 1)
    bands = jnp.linspace(1e-4, HY_BANDS - 1, HY_BANDS, dtype=F32)
    ang = (2.0 * math.pi / n) * pos[:, None] * bands
    z = jnp.concatenate([t01[:, None], jnp.cos(ang), -jnp.sin(ang)], axis=-1)
    return jnp.pad(z, ((0, 0), (0, 128 - z.shape[1])))


def _hyena_segment(v_all, x1_all, x2_all, seg_start, n, n2, filt_args, skip):
    b, lt, c = v_all.shape
    n1, a1_full, g_fwd, g_inv, a4 = _fft_tables(n, n2)
    a1_half = a1_full[:, :n1 // 2]
    cols = n2 * c
    col_tile = min(cols, 8192)
    seg_blk = seg_start // n
    view = lambda a: a.reshape(b, lt // n2, cols)

    h = _hy_filters(_hy_positional(n), *filt_args)
    h = h.reshape(n, 2, 2, c)
    kern = jnp.concatenate([h[:, 0], jnp.zeros((1, 2, c), F32), jnp.flip(h[1:, 1], axis=0)], axis=0)
    kern = jnp.moveaxis(kern, 1, 0)
    ky = _fft_stage1(a1_full, kern.reshape(2, n1, cols), 0, col_tile)
    kf = _fft_mid(ky.reshape(2, 2, n1, n2, c), g_fwd, None, None, n1, n2)

    def long_conv(u_view, u_blk, gate_all, filt):
        y1 = _fft_stage1(a1_half, u_view, u_blk, col_tile)
        vmid = _fft_mid(y1.reshape(b, 2, n1, n2, c), g_fwd, kf[filt:filt + 1], g_inv, n1, n2)
        skip_row = jnp.tile(skip[filt], n2).reshape(1, cols)
        return _fft_stage4(a4, vmid.reshape(b, 2 * n1, cols), u_view, u_blk, view(gate_all), seg_blk,
                           skip_row, col_tile)

    z = long_conv(view(v_all), seg_blk, x1_all, 0)
    out = long_conv(z, 0, x2_all, 1)
    return out.reshape(b, n, c)


def _rope_tables(rows, lt):
    n_freq = DA_DK // 4
    inv = ROPE_THETA ** (-jnp.arange(n_freq, dtype=F32) / n_freq)
    r = jnp.repeat(jnp.arange(rows, dtype=F32), GRID_W)
    col = jnp.tile(jnp.arange(GRID_W, dtype=F32), rows)
    ang_r, ang_c = r[:, None] * inv, col[:, None] * inv
    cos32 = jnp.concatenate([jnp.cos(ang_r)] * 2 + [jnp.cos(ang_c)] * 2, axis=-1)
    sin32 = jnp.concatenate([-jnp.sin(ang_r), jnp.sin(ang_r), -jnp.sin(ang_c), jnp.sin(ang_c)], axis=-1)
    n = rows * GRID_W
    pad = lt - n
    cos = jnp.concatenate([jnp.tile(cos32, (1, W_GROUP // DA_DK)), jnp.ones((pad, W_GROUP), F32)], axis=0)
    sin = jnp.concatenate([jnp.tile(sin32, (1, W_GROUP // DA_DK)), jnp.zeros((pad, W_GROUP), F32)], axis=0)
    return cos, sin


def kernel(x, c, ctx, c_ctx, mod_w, mod_b, ln1_w, ln2_w, w_in, gdn_conv_w, gdn_a_log, gdn_dt_bias, gdn_norm_w, hy_conv_w, hy_w1, hy_b1, hy_f1, hy_w2, hy_b2, hy_f2, hy_w3, hy_decay, hy_bias, hg_lb_raw, hg_norm_w, da_q_norm, da_k_norm, da_lam, da_subln, w_out, mlp_w1, mlp_w2):
    batch, seq, d = x.shape
    n_ctx = ctx.shape[1]
    depth = mod_w.shape[0]
    assert n_ctx == TOKEN_BLOCK and seq % TOKEN_BLOCK == 0 and d == 4 * W_GROUP
    nbl = seq // TOKEN_BLOCK
    lt = seq + n_ctx

    xs = jnp.concatenate([x, ctx], axis=1)
    cond = jnp.concatenate([c, c_ctx[None, :], jnp.zeros((-(batch + 1) % 8, d), F32)], axis=0)
    mods = _modulation(cond, mod_w, mod_b)[:, :batch + 1].reshape(depth, batch + 1, 6, d)
    mods = jnp.pad(mods, ((0, 0), (0, 0), (0, 2), (0, 0)))

    lb_all = jnp.cumsum(jax.nn.softmax(hg_lb_raw.astype(F32), axis=0), axis=0)
    lb_all = lb_all - lb_all[0]
    cos_tab, sin_tab = _rope_tables(seq // GRID_W, lt)

    gdn_cols = 4 * W_GROUP + 4 * N_HEADS
    hy_cols, hg_cols, da_cols = 3 * W_GROUP, 5 * W_GROUP, 3 * W_GROUP
    splits = (GDN_PAD_COLS, hy_cols, hg_cols, da_cols)

    for l in range(depth):
        want_ctx = l < depth - 1
        modtab = mods[l]
        wl = w_in[l]
        w_pad = jnp.concatenate([wl[:, :gdn_cols], jnp.zeros((d, GDN_PAD_COLS - gdn_cols), F32),
                                 wl[:, gdn_cols:]], axis=1).astype(BF16)
        p_gdn, p_hy, p_hg, p_da = _adaln_proj(xs, modtab, ln1_w[l], w_pad, splits, (F32,) * 4, 0, 1, False,
                                              nbl, "in_proj")

        pad16 = lambda a: jnp.pad(a.reshape(1, 2 * N_HEADS), ((0, 0), (0, 128 - 2 * N_HEADS)))
        alog_row, dtb_row = pad16(gdn_a_log[l]), pad16(gdn_dt_bias[l])
        gnw = jnp.tile(gdn_norm_w[l], N_HEADS).reshape(1, W_GROUP)
        o_b = _gdn_direction(p_gdn, gdn_conv_w[l], alog_row, dtb_row, gnw, None, 1, nbl)
        o_gdn = _gdn_direction(p_gdn, gdn_conv_w[l], alog_row, dtb_row, gnw, o_b, 0, nbl)

        hv, hx1, hx2 = _hy_short_conv(p_hy, hy_conv_w[l], nbl)
        w1p = jnp.pad(hy_w1[l], ((0, 128 - hy_w1.shape[1]), (0, 0)))
        row = lambda a: a.reshape(1, -1)
        dec_row = jnp.tile(hy_decay[l].reshape(1, 2 * W_GROUP), (1, 2))
        filt_args = (w1p, row(hy_b1[l]), row(hy_f1[l]), hy_w2[l], row(hy_b2[l]), row(hy_f2[l]), hy_w3[l], dec_row)
        o_hy_lat = _hyena_segment(hv, hx1, hx2, 0, seq, FFT_N2_LAT, filt_args, hy_bias[l])
        if want_ctx:
            o_hy_ctx = _hyena_segment(hv, hx1, hx2, seq, n_ctx, FFT_N2_CTX, filt_args, hy_bias[l])
        else:
            o_hy_ctx = jnp.zeros((batch, n_ctx, W_GROUP), F32)
        o_hy = jnp.concatenate([o_hy_lat, o_hy_ctx], axis=1)

        lb_row = lb_all[l].reshape(1, W_GROUP)
        hnw = jnp.tile(hg_norm_w[l], N_HEADS).reshape(1, W_GROUP)
        o_hb = _hgrn_direction(p_hg, lb_row, hnw, None, 1, nbl)
        o_hg = _hgrn_direction(p_hg, lb_row, hnw, o_hb, 0, nbl)

        lam_init = 0.8 - 0.6 * math.exp(-0.3 * l)
        qw = jnp.tile(da_q_norm[l], W_GROUP // DA_DK).reshape(1, W_GROUP)
        kw = jnp.tile(da_k_norm[l], W_GROUP // DA_DK).reshape(1, W_GROUP)
        sw = jnp.tile(da_subln[l], N_HEADS).reshape(1, W_GROUP)
        qh, kh, vh = _da_prep(p_da, cos_tab, sin_tab, qw, kw)
        o_da = _da_attention(qh, kh, vh, da_lam[l], sw, lam_init, nbl, 0, lt, 0, "da_attn_lat")
        if want_ctx:
            o_da_ctx = _da_attention(qh, kh, vh, da_lam[l], sw, lam_init, 1, nbl, TOKEN_BLOCK, nbl, "da_attn_ctx")
        else:
            o_da_ctx = jnp.zeros((batch, n_ctx, W_GROUP), F32)
        o_da = jnp.concatenate([o_da, o_da_ctx], axis=1)

        n_blocks = nbl + 1 if want_ctx else nbl
        xs_mid = _proj_residual(xs, modtab, w_out[l].astype(BF16), [o_gdn, o_hy, o_hg, o_da], 2, n_blocks,
                                nbl, "out_proj")
        (hid,) = _adaln_proj(xs_mid, modtab, ln2_w[l], mlp_w1[l].astype(BF16), (mlp_w1.shape[2],), (BF16,),
                             3, 4, True, nbl, "mlp_up")
        xs = _proj_residual(xs_mid, modtab, mlp_w2[l].astype(BF16), [hid], 5, n_blocks, nbl, "mlp_down")
    return xs
```

```python
import functools
import math

import jax
import jax.numpy as jnp
from jax import lax
from jax.experimental import pallas as pl
from jax.experimental.pallas import tpu as pltpu

F32 = jnp.float32
BF16 = jnp.bfloat16

EPS = 1e-6
TOKEN_BLOCK = 256
CHUNK = 64
HEAD_DIM = 64
N_HEADS = 4
W_GROUP = 256
DA_DK = 32
GRID_W = 64
ROPE_THETA = 10000.0
HY_BANDS = 16
FFT_N2_LAT = 128
FFT_N2_CTX = 32
FFT_K1_GROUP = 8
DA_Q_TILE = 512
GDN_PAD_COLS = 1152
VMEM_LIMIT_BYTES = 56 * 1024 * 1024


def _cparams(n_axes):
    return pltpu.CompilerParams(dimension_semantics=("arbitrary",) * n_axes,
                                vmem_limit_bytes=VMEM_LIMIT_BYTES)


def _sigmoid(x):
    return 1.0 / (1.0 + jnp.exp(-x))


def _silu(x):
    return x * _sigmoid(x)


def _softplus(x):
    return jnp.maximum(x, 0.0) + jnp.log(1.0 + jnp.exp(-jnp.abs(x)))


def _log_sigmoid(x):
    return jnp.minimum(x, 0.0) - jnp.log(1.0 + jnp.exp(-jnp.abs(x)))


def _bdot(a, b):
    return jnp.dot(a.astype(BF16), b.astype(BF16), preferred_element_type=F32)


def _bdot_nt(a, b):
    return lax.dot_general(a.astype(BF16), b.astype(BF16), (((1,), (1,)), ((), ())),
                           preferred_element_type=F32)


def _split3(x):
    hi = x.astype(BF16)
    r = x - hi.astype(F32)
    mid = r.astype(BF16)
    lo = (r - mid.astype(F32)).astype(BF16)
    return hi, mid, lo


def _exact_left(m_bf16, x):
    hi, mid, lo = _split3(x)
    d = lambda p: jnp.dot(m_bf16, p, preferred_element_type=F32)
    return d(hi) + d(mid) + d(lo)


def _exact_right(x, m_bf16):
    hi, mid, lo = _split3(x)
    d = lambda p: jnp.dot(p, m_bf16, preferred_element_type=F32)
    return d(hi) + d(mid) + d(lo)


def _hp_dot(a, b):
    ah = a.astype(BF16)
    al = (a - ah.astype(F32)).astype(BF16)
    bh = b.astype(BF16)
    bl = (b - bh.astype(F32)).astype(BF16)
    d = lambda p, q: jnp.dot(p, q, preferred_element_type=F32)
    return d(ah, bh) + d(ah, bl) + d(al, bh)


def _group_ones(width, group):
    r = lax.broadcasted_iota(jnp.int32, (width, width), 0) // group
    c = lax.broadcasted_iota(jnp.int32, (width, width), 1) // group
    return (r == c).astype(BF16)


def _mod_kernel(s_ref, w_ref, b_ref, o_ref):
    s = _silu(s_ref[...])
    o_ref[0] = _bdot(s, w_ref[0]) + b_ref[0]


def _modulation(cond, mod_w, mod_b):
    depth, d, n = mod_w.shape
    rows = cond.shape[0]
    tn = 1536
    return pl.pallas_call(
        _mod_kernel,
        out_shape=jax.ShapeDtypeStruct((depth, rows, n), F32),
        grid=(depth, n // tn),
        in_specs=[pl.BlockSpec((rows, d), lambda l, j: (0, 0)),
                  pl.BlockSpec((1, d, tn), lambda l, j: (l, 0, j)),
                  pl.BlockSpec((1, 1, tn), lambda l, j: (l, 0, j))],
        out_specs=pl.BlockSpec((1, rows, tn), lambda l, j: (l, 0, j)),
        compiler_params=_cparams(2), name="modulation",
    )(cond, mod_w, mod_b.reshape(depth, 1, n))


def _adaln_proj_kernel(x_ref, mod_ref, lnw_ref, w_ref, *out_refs, shift_row, scale_row, splits, sq_relu):
    x = x_ref[0]
    ms = jnp.mean(x * x, axis=-1, keepdims=True)
    y = x * lax.rsqrt(ms + EPS) * lnw_ref[...]
    mod = mod_ref[0]
    y = y * (1.0 + mod[scale_row:scale_row + 1]) + mod[shift_row:shift_row + 1]
    h = _bdot(y, w_ref[...])
    if sq_relu:
        h = jnp.square(jnp.maximum(h, 0.0))
    off = 0
    for o_ref, n in zip(out_refs, splits):
        o_ref[0] = h[:, off:off + n].astype(o_ref.dtype)
        off += n


def _mod_index(nbl, batch):
    return lambda b, t: (jnp.where(t == nbl, batch, b), 0, 0)


def _adaln_proj(xs, modtab, ln_w, w, splits, out_dtypes, shift_row, scale_row, sq_relu, nbl, name):
    b, lt, d = xs.shape
    nb = lt // TOKEN_BLOCK
    n = w.shape[1]
    kern = functools.partial(_adaln_proj_kernel, shift_row=shift_row, scale_row=scale_row,
                             splits=splits, sq_relu=sq_relu)
    return pl.pallas_call(
        kern,
        out_shape=[jax.ShapeDtypeStruct((b, lt, s), dt) for s, dt in zip(splits, out_dtypes)],
        grid=(b, nb),
        in_specs=[pl.BlockSpec((1, TOKEN_BLOCK, d), lambda i, t: (i, t, 0)),
                  pl.BlockSpec((1, 8, d), _mod_index(nbl, b)),
                  pl.BlockSpec((1, d), lambda i, t: (0, 0)),
                  pl.BlockSpec((d, n), lambda i, t: (0, 0))],
        out_specs=[pl.BlockSpec((1, TOKEN_BLOCK, s), lambda i, t: (i, t, 0)) for s in splits],
        compiler_params=_cparams(2), name=name,
    )(xs, modtab, ln_w.reshape(1, d), w)


def _proj_residual_kernel(x_ref, mod_ref, w_ref, *rest, gate_row):
    a_refs, o_ref = rest[:-1], rest[-1]
    a = jnp.concatenate([r[0].astype(BF16) for r in a_refs], axis=1) if len(a_refs) > 1 else a_refs[0][0]
    gate = mod_ref[0][gate_row:gate_row + 1]
    o_ref[0] = x_ref[0] + gate * _bdot(a, w_ref[...])


def _proj_residual(xs, modtab, w, acts, gate_row, n_blocks, nbl, name):
    b, lt, d = xs.shape
    kern = functools.partial(_proj_residual_kernel, gate_row=gate_row)
    return pl.pallas_call(
        kern,
        out_shape=jax.ShapeDtypeStruct((b, n_blocks * TOKEN_BLOCK, d), F32),
        grid=(b, n_blocks),
        in_specs=[pl.BlockSpec((1, TOKEN_BLOCK, d), lambda i, t: (i, t, 0)),
                  pl.BlockSpec((1, 8, d), _mod_index(nbl, b)),
                  pl.BlockSpec(w.shape, lambda i, t: (0, 0))]
                 + [pl.BlockSpec((1, TOKEN_BLOCK, a.shape[2]), lambda i, t: (i, t, 0)) for a in acts],
        out_specs=pl.BlockSpec((1, TOKEN_BLOCK, d), lambda i, t: (i, t, 0)),
        compiler_params=_cparams(2), name=name,
    )(xs, modtab, w, *acts)


def _scan_block(step, nbl, direction):
    lat = step - 1 if direction == 0 else nbl - step
    return jnp.where(step == 0, nbl, lat)


def _halo_specs(width, nbl, block_of_step, lt):
    rows8 = TOKEN_BLOCK // 8

    def prev_map(b, s):
        return (b, jnp.maximum(block_of_step(s) * rows8 - 1, 0), 0)

    def next_map(b, s):
        return (b, jnp.minimum((block_of_step(s) + 1) * rows8, lt // 8 - 1), 0)

    return [pl.BlockSpec((1, 8, width), prev_map), pl.BlockSpec((1, 8, width), next_map)]


def _short_conv(cur, hp_ref, hn_ref, cw_ref, xp_ref, blk, nbl, width):
    seg_first = jnp.logical_or(blk == 0, blk == nbl)
    seg_last = jnp.logical_or(blk == nbl - 1, blk == nbl)
    xp_ref[0:8, :] = jnp.where(seg_first, 0.0, hp_ref[0][:, :width])
    xp_ref[8:8 + TOKEN_BLOCK, :] = cur
    xp_ref[8 + TOKEN_BLOCK:16 + TOKEN_BLOCK, :] = jnp.where(seg_last, 0.0, hn_ref[0][:, :width])
    cw = cw_ref[...]
    return (cw[0:1] * xp_ref[7:7 + TOKEN_BLOCK, :] + cw[1:2] * xp_ref[8:8 + TOKEN_BLOCK, :]
            + cw[2:3] * xp_ref[9:9 + TOKEN_BLOCK, :])


def _scan_tables(direction, chunk):
    i = jnp.arange(TOKEN_BLOCK, dtype=jnp.int32)[:, None]
    j = jnp.arange(TOKEN_BLOCK, dtype=jnp.int32)[None, :]
    same = (i // chunk) == (j // chunk)
    pos = (lambda t: t % chunk) if direction == 0 else (lambda t: chunk - 1 - t % chunk)
    pi, pj = pos(i), pos(j)
    eye = i == j
    fl, bl = [same & (pj <= pi), same & (pj < pi), eye], [same & (pj <= pi), (i // HEAD_DIM) == (j // HEAD_DIM)]
    m = 1
    while m < chunk:
        fl.append(same & ((pi // (2 * m)) == (pj // (2 * m))) & ((pi // m) % 2 == 1) & ((pj // m) % 2 == 0))
        boundary = (pi // (2 * m)) * (2 * m) + m - 1
        bl.append(eye.astype(F32) - (same & (pj == boundary)).astype(F32))
        m *= 2
    return jnp.stack([a.astype(F32) for a in fl]), jnp.stack([a.astype(BF16) for a in bl])


def _group_rms(o, gmat, group, weight):
    ms = jnp.dot((o * o).astype(BF16), gmat, preferred_element_type=F32) * (1.0 / group)
    return o * lax.rsqrt(ms + EPS) * weight


def _gdn_kernel(*refs, direction, finish, nbl):
    if finish:
        (p_ref, hp_ref, hn_ref, cw_ref, alog_ref, dtb_ref, nw_ref, fm_ref, bm_ref, other_ref,
         o_ref, s_ref, xp_ref, ob_ref) = refs
    else:
        p_ref, hp_ref, hn_ref, cw_ref, alog_ref, dtb_ref, nw_ref, fm_ref, bm_ref, o_ref, s_ref, xp_ref, ob_ref = refs
    step = pl.program_id(1)
    blk = _scan_block(step, nbl, direction)

    @pl.when(step == 0)
    def _():
        s_ref[...] = jnp.zeros_like(s_ref)

    p = p_ref[0]
    xc = _silu(_short_conv(p[:, :3 * W_GROUP], hp_ref, hn_ref, cw_ref, xp_ref, blk, nbl, 3 * W_GROUP))
    g64 = bm_ref[1]
    q = xc[:, :W_GROUP]
    k = xc[:, W_GROUP:2 * W_GROUP]
    v = xc[:, 2 * W_GROUP:]
    q = q * lax.rsqrt(_bdot(q * q, g64) + EPS) * (HEAD_DIM ** -0.5)
    k = k * lax.rsqrt(_bdot(k * k, g64) + EPS)
    k_t = k.T

    ab = p[:, 4 * W_GROUP:4 * W_GROUP + 128]
    lane = lax.broadcasted_iota(jnp.int32, ab.shape, 1)
    g_all = -jnp.exp(alog_ref[...]) * _softplus(ab + dtb_ref[...])
    w_all = jnp.where(lane < 2 * N_HEADS, g_all, _sigmoid(ab))
    z_col = jnp.where(lane < 2 * N_HEADS, _exact_left(bm_ref[0], w_all), w_all)
    z_row = z_col.T

    heads = range(N_HEADS)
    lanes = [slice(h * HEAD_DIM, (h + 1) * HEAD_DIM) for h in heads]
    gidx = [N_HEADS * direction + h for h in heads]
    incl = fm_ref[0] > 0.5
    gc_r = [z_row[gi:gi + 1, :] for gi in gidx]
    beta = [z_col[:, 2 * N_HEADS + gi:2 * N_HEADS + gi + 1] for gi in gidx]
    eg = [jnp.exp(z_col[:, gi:gi + 1]) for gi in gidx]
    decay = [jnp.exp(jnp.where(incl, z_col[:, gi:gi + 1] - gc_r[h], -jnp.inf)) for h, gi in enumerate(gidx)]
    kk = [_bdot_nt(k[:, ls], k[:, ls]) for ls in lanes]
    qk = [_bdot_nt(q[:, ls], k[:, ls]) for ls in lanes]
    n = [kk[h] * beta[h] * decay[h] * fm_ref[1] for h in heads]
    inv = [fm_ref[2] - n[h] * fm_ref[3] for h in heads]
    for lev in range(1, 6):
        half = fm_ref[3 + lev]
        x = [_bdot(inv[h], n[h] * half) for h in heads]
        inv = [inv[h] - _bdot(x[h], inv[h]) for h in heads]
    uw = [_bdot(inv[h], jnp.concatenate([v[:, ls] * beta[h], k[:, ls] * (beta[h] * eg[h])], axis=1))
          for h, ls in enumerate(lanes)]
    attn = [qk[h] * decay[h] for h in heads]
    qg = [q[:, ls] * eg[h] for h, ls in enumerate(lanes)]
    state = [s_ref[h] for h in heads]

    last = CHUNK - 1 if direction == 0 else 0
    n_chunks = TOKEN_BLOCK // CHUNK
    for ci in range(n_chunks):
        c = ci if direction == 0 else n_chunks - 1 - ci
        rs = slice(c * CHUNK, (c + 1) * CHUNK)
        g_last = [z_col[c * CHUNK + last:c * CHUNK + last + 1, gi:gi + 1] for gi in gidx]
        v_new = [uw[h][rs, :HEAD_DIM] - _bdot(uw[h][rs, HEAD_DIM:], state[h]) for h in heads]
        for h, ls in enumerate(lanes):
            ob_ref[rs, ls] = _bdot(qg[h][rs], state[h]) + _bdot(attn[h][rs, rs], v_new[h])
        state = [state[h] * jnp.exp(g_last[h])
                 + _bdot(k_t[ls, rs] * jnp.exp(g_last[h] - gc_r[h][:, rs]), v_new[h]) for h, ls in enumerate(lanes)]
    for h in heads:
        s_ref[h] = state[h]

    if finish:
        o = ob_ref[...] + other_ref[0]
        gate = _silu(p[:, 3 * W_GROUP:4 * W_GROUP])
        o_ref[0] = _group_rms(o, g64, HEAD_DIM, nw_ref[...]) * gate
    else:
        o_ref[0] = ob_ref[...]


def _gdn_direction(p_gdn, conv_w, alog_row, dtb_row, nw_row, other, direction, nbl):
    b, lt, width = p_gdn.shape
    nb = nbl + 1
    blk_of = lambda s: _scan_block(s, nbl, direction)
    finish = other is not None
    fmask, bmask = _scan_tables(direction, CHUNK)
    bmask = bmask[:2]
    kern = functools.partial(_gdn_kernel, direction=direction, finish=finish, nbl=nbl)
    in_specs = ([pl.BlockSpec((1, TOKEN_BLOCK, width), lambda i, s: (i, blk_of(s), 0))]
                + _halo_specs(3 * W_GROUP, nbl, blk_of, lt)
                + [pl.BlockSpec((3, 3 * W_GROUP), lambda i, s: (0, 0)),
                   pl.BlockSpec((1, 128), lambda i, s: (0, 0)),
                   pl.BlockSpec((1, 128), lambda i, s: (0, 0)),
                   pl.BlockSpec((1, W_GROUP), lambda i, s: (0, 0)),
                   pl.BlockSpec(fmask.shape, lambda i, s: (0, 0, 0)),
                   pl.BlockSpec(bmask.shape, lambda i, s: (0, 0, 0))])
    args = [p_gdn, p_gdn, p_gdn, conv_w, alog_row, dtb_row, nw_row, fmask, bmask]
    if finish:
        in_specs.append(pl.BlockSpec((1, TOKEN_BLOCK, W_GROUP), lambda i, s: (i, blk_of(s), 0)))
        args.append(other)
    return pl.pallas_call(
        kern,
        out_shape=jax.ShapeDtypeStruct((b, lt, W_GROUP), F32),
        grid=(b, nb),
        in_specs=in_specs,
        out_specs=pl.BlockSpec((1, TOKEN_BLOCK, W_GROUP), lambda i, s: (i, blk_of(s), 0)),
        scratch_shapes=[pltpu.VMEM((N_HEADS, HEAD_DIM, HEAD_DIM), F32),
                        pltpu.VMEM((TOKEN_BLOCK + 16, 3 * W_GROUP), F32),
                        pltpu.VMEM((TOKEN_BLOCK, W_GROUP), F32)],
        compiler_params=_cparams(2), name=f"gdn_dir{direction}",
    )(*args)


def _hgrn_kernel(*refs, direction, finish, nbl):
    if finish:
        p_ref, lb_ref, nw_ref, fm_ref, bm_ref, other_ref, o_ref, s_ref, ob_ref = refs
    else:
        p_ref, lb_ref, nw_ref, fm_ref, bm_ref, o_ref, s_ref, ob_ref = refs
    step = pl.program_id(1)

    @pl.when(step == 0)
    def _():
        s_ref[...] = jnp.zeros_like(s_ref)

    p = p_ref[0]
    lb = lb_ref[...]
    q = _silu(p[:, :W_GROUP])
    v = p[:, W_GROUP:2 * W_GROUP]
    fl = p[:, (2 + direction) * W_GROUP:(3 + direction) * W_GROUP]
    a = jnp.log(lb)
    bb = jnp.log(1.0 - lb) + _log_sigmoid(fl)
    logf = jnp.maximum(a, bb) + jnp.log(1.0 + jnp.exp(-jnp.abs(a - bb)))
    key = (1.0 - lb) * _sigmoid(-fl)
    gcum = _exact_left(bm_ref[0], logf)
    pieces = _split3(gcum)
    n_levels = fm_ref.shape[0] - 3
    lev_e = []
    for lev in range(n_levels):
        dm = bm_ref[2 + lev]
        d = sum(jnp.dot(dm, piece, preferred_element_type=F32) for piece in pieces)
        lev_e.append(jnp.exp(-jnp.abs(d)))
    last = TOKEN_BLOCK - 1 if direction == 0 else 0
    g_end = gcum[last:last + 1, :]
    qg = q * jnp.exp(gcum)
    kdec = key * jnp.exp(g_end - gcum)
    decay_end = jnp.exp(g_end)
    v_t = v.T

    lanes = [slice(h * HEAD_DIM, (h + 1) * HEAD_DIM) for h in range(N_HEADS)]
    amat = [_bdot_nt(q[:, ls], key[:, ls]) * fm_ref[2] for ls in lanes]
    for lev in range(n_levels):
        qe, ke = q * lev_e[lev], key * lev_e[lev]
        half = fm_ref[3 + lev]
        amat = [amat[h] + half * _bdot_nt(qe[:, ls], ke[:, ls]) for h, ls in enumerate(lanes)]
    for h, ls in enumerate(lanes):
        st = s_ref[h]
        ob_ref[:, ls] = _bdot_nt(qg[:, ls], st) + _bdot(amat[h], v[:, ls])
        s_ref[h] = st * decay_end[:, ls] + _bdot(v_t[ls, :], kdec[:, ls])

    if finish:
        o = ob_ref[...] + other_ref[0]
        gate = _silu(p[:, 4 * W_GROUP:5 * W_GROUP])
        o_ref[0] = _group_rms(o, bm_ref[1], HEAD_DIM, nw_ref[...]) * gate
    else:
        o_ref[0] = ob_ref[...]


def _hgrn_direction(p_hg, lb_row, nw_row, other, direction, nbl):
    b, lt, width = p_hg.shape
    nb = nbl + 1
    blk_of = lambda s: _scan_block(s, nbl, direction)
    finish = other is not None
    fmask, bmask = _scan_tables(direction, TOKEN_BLOCK)
    kern = functools.partial(_hgrn_kernel, direction=direction, finish=finish, nbl=nbl)
    in_specs = [pl.BlockSpec((1, TOKEN_BLOCK, width), lambda i, s: (i, blk_of(s), 0)),
                pl.BlockSpec((1, W_GROUP), lambda i, s: (0, 0)),
                pl.BlockSpec((1, W_GROUP), lambda i, s: (0, 0)),
                pl.BlockSpec(fmask.shape, lambda i, s: (0, 0, 0)),
                pl.BlockSpec(bmask.shape, lambda i, s: (0, 0, 0))]
    args = [p_hg, lb_row, nw_row, fmask, bmask]
    if finish:
        in_specs.append(pl.BlockSpec((1, TOKEN_BLOCK, W_GROUP), lambda i, s: (i, blk_of(s), 0)))
        args.append(other)
    return pl.pallas_call(
        kern,
        out_shape=jax.ShapeDtypeStruct((b, lt, W_GROUP), F32),
        grid=(b, nb),
        in_specs=in_specs,
        out_specs=pl.BlockSpec((1, TOKEN_BLOCK, W_GROUP), lambda i, s: (i, blk_of(s), 0)),
        scratch_shapes=[pltpu.VMEM((N_HEADS, HEAD_DIM, HEAD_DIM), F32),
                        pltpu.VMEM((TOKEN_BLOCK, W_GROUP), F32)],
        compiler_params=_cparams(2), name=f"hgrn_dir{direction}",
    )(*args)


def _da_prep_kernel(p_ref, cos_ref, sin_ref, qw_ref, kw_ref, q_ref, k_ref, v_ref):
    p = p_ref[0]
    g32 = _group_ones(W_GROUP, DA_DK)
    lane = lax.broadcasted_iota(jnp.int32, (TOKEN_BLOCK, W_GROUP), 1)
    first_half = (lane % 16) < 8
    cos, sin = cos_ref[...], sin_ref[...]

    def norm_rope(x, w):
        ms = _bdot(x * x, g32) * (1.0 / DA_DK)
        y = x * lax.rsqrt(ms + EPS) * w
        partner = jnp.where(first_half, pltpu.roll(y, W_GROUP - 8, axis=1), pltpu.roll(y, 8, axis=1))
        return y * cos + partner * sin

    q_ref[0] = (norm_rope(p[:, :W_GROUP], qw_ref[...]) * (DA_DK ** -0.5)).astype(BF16)
    k_ref[0] = norm_rope(p[:, W_GROUP:2 * W_GROUP], kw_ref[...]).astype(BF16)
    src = lax.broadcasted_iota(jnp.int32, (W_GROUP, 2 * W_GROUP), 0)
    dst = lax.broadcasted_iota(jnp.int32, (W_GROUP, 2 * W_GROUP), 1)
    spread = (dst == (src // HEAD_DIM) * 128 + src % HEAD_DIM).astype(BF16)
    wide = jnp.dot(p[:, 2 * W_GROUP:].astype(BF16), spread, preferred_element_type=F32)
    lane_w = lax.broadcasted_iota(jnp.int32, wide.shape, 1)
    v_ref[0] = jnp.where(lane_w % 128 == HEAD_DIM, 1.0, wide).astype(BF16)


def _da_prep(p_da, cos_tab, sin_tab, qw_row, kw_row):
    b, lt, width = p_da.shape
    nb = lt // TOKEN_BLOCK
    blk = lambda w=W_GROUP: pl.BlockSpec((1, TOKEN_BLOCK, w), lambda i, t: (i, t, 0))
    return pl.pallas_call(
        _da_prep_kernel,
        out_shape=[jax.ShapeDtypeStruct((b, lt, W_GROUP), BF16)] * 2
                  + [jax.ShapeDtypeStruct((b, lt, 2 * W_GROUP), BF16)],
        grid=(b, nb),
        in_specs=[pl.BlockSpec((1, TOKEN_BLOCK, width), lambda i, t: (i, t, 0)),
                  pl.BlockSpec((TOKEN_BLOCK, W_GROUP), lambda i, t: (t, 0)),
                  pl.BlockSpec((TOKEN_BLOCK, W_GROUP), lambda i, t: (t, 0)),
                  pl.BlockSpec((1, W_GROUP), lambda i, t: (0, 0)),
                  pl.BlockSpec((1, W_GROUP), lambda i, t: (0, 0))],
        out_specs=[blk(), blk(), blk(2 * W_GROUP)],
        compiler_params=_cparams(2), name="da_prep",
    )(p_da, cos_tab, sin_tab, qw_row, kw_row)


def _da_attn_kernel(q_ref, k_ref, v_ref, lam_ref, sw_ref, o_ref, *, lam_init):
    q = q_ref[0]
    k = k_ref[0]
    lp = lam_ref[...]
    lam = (jnp.exp(jnp.sum(lp[0:1] * lp[1:2], axis=1, keepdims=True))
           - jnp.exp(jnp.sum(lp[2:3] * lp[3:4], axis=1, keepdims=True)) + lam_init)
    lane = lax.broadcasted_iota(jnp.int32, q.shape, 1)
    lane128 = lax.broadcasted_iota(jnp.int32, (q.shape[0], 128), 1)
    outs = []
    for h in range(N_HEADS):
        v_h = v_ref[0, :, h * 128:(h + 1) * 128]
        maps = []
        for j in range(2):
            lo = h * HEAD_DIM + j * DA_DK
            qm = jnp.where(jnp.logical_and(lane >= lo, lane < lo + DA_DK), q, jnp.zeros_like(q))
            s = lax.dot_general(qm, k, (((1,), (1,)), ((), ())), preferred_element_type=F32).astype(BF16)
            e = jnp.exp(s - jnp.max(s, axis=1, keepdims=True))
            acc = jnp.dot(e, v_h, preferred_element_type=F32)
            maps.append(acc / acc[:, HEAD_DIM:HEAD_DIM + 1])
        outs.append(maps[0] - lam * maps[1])
    pairs = [jnp.where(lane128 < HEAD_DIM, outs[2 * i], pltpu.roll(outs[2 * i + 1], HEAD_DIM, axis=1))
             for i in range(N_HEADS // 2)]
    o = jnp.concatenate(pairs, axis=1)
    g64 = _group_ones(W_GROUP, HEAD_DIM)
    o_ref[0] = _group_rms(o, g64, HEAD_DIM, sw_ref[...]) * (1.0 - lam_init)


def _da_attention(qh, kh, vh, lam_p, sw_row, lam_init, q_tile, q_blocks, q_off, k_rows, k_blk, name):
    b, lt, _ = qh.shape
    kern = functools.partial(_da_attn_kernel, lam_init=lam_init)
    return pl.pallas_call(
        kern,
        out_shape=jax.ShapeDtypeStruct((b, q_blocks * q_tile, W_GROUP), F32),
        grid=(b, q_blocks),
        in_specs=[pl.BlockSpec((1, q_tile, W_GROUP), lambda i, t: (i, t + q_off, 0)),
                  pl.BlockSpec((1, k_rows, W_GROUP), lambda i, t: (i, k_blk, 0)),
                  pl.BlockSpec((1, k_rows, 2 * W_GROUP), lambda i, t: (i, k_blk, 0)),
                  pl.BlockSpec((4, DA_DK), lambda i, t: (0, 0)),
                  pl.BlockSpec((1, W_GROUP), lambda i, t: (0, 0))],
        out_specs=pl.BlockSpec((1, q_tile, W_GROUP), lambda i, t: (i, t, 0)),
        compiler_params=_cparams(2), name=name,
    )(qh, kh, vh, lam_p, sw_row)


def _hy_conv_kernel(p_ref, hp_ref, hn_ref, cw_ref, v_ref, x1_ref, x2_ref, xp_ref, *, nbl):
    blk = pl.program_id(1)
    xc = _short_conv(p_ref[0], hp_ref, hn_ref, cw_ref, xp_ref, blk, nbl, 3 * W_GROUP)
    v_ref[0] = xc[:, :W_GROUP]
    x1_ref[0] = xc[:, W_GROUP:2 * W_GROUP]
    x2_ref[0] = xc[:, 2 * W_GROUP:]


def _hy_short_conv(p_hy, conv_w, nbl):
    b, lt, width = p_hy.shape
    nb = nbl + 1
    blk = lambda: pl.BlockSpec((1, TOKEN_BLOCK, W_GROUP), lambda i, t: (i, t, 0))
    return pl.pallas_call(
        functools.partial(_hy_conv_kernel, nbl=nbl),
        out_shape=[jax.ShapeDtypeStruct((b, lt, W_GROUP), F32)] * 3,
        grid=(b, nb),
        in_specs=[pl.BlockSpec((1, TOKEN_BLOCK, width), lambda i, t: (i, t, 0))]
                 + _halo_specs(width, nbl, lambda s: s, lt)
                 + [pl.BlockSpec((3, width), lambda i, t: (0, 0))],
        out_specs=[blk(), blk(), blk()],
        scratch_shapes=[pltpu.VMEM((TOKEN_BLOCK + 16, width), F32)],
        compiler_params=_cparams(2), name="hy_short_conv",
    )(p_hy, p_hy, p_hy, conv_w)


def _hy_filter_kernel(z_ref, w1_ref, b1_ref, f1_ref, w2_ref, b2_ref, f2_ref, w3_ref, dec_ref, o_ref):
    z = z_ref[...]
    h = jnp.sin(f1_ref[...] * (_hp_dot(z, w1_ref[...]) + b1_ref[...]))
    h = jnp.sin(f2_ref[...] * (_hp_dot(h, w2_ref[...]) + b2_ref[...]))
    h = _hp_dot(h, w3_ref[...])
    o_ref[...] = h * jnp.exp(-z[:, 0:1] * jnp.abs(dec_ref[...]))


def _hy_filters(zfeat, w1p, b1, f1, w2, b2, f2, w3, dec_row):
    n = zfeat.shape[0]
    tn = min(n, 512)
    full = lambda a: pl.BlockSpec(a.shape, lambda i: (0,) * a.ndim)
    args = [w1p, b1, f1, w2, b2, f2, w3, dec_row]
    return pl.pallas_call(
        _hy_filter_kernel,
        out_shape=jax.ShapeDtypeStruct((n, 4 * W_GROUP), F32),
        grid=(n // tn,),
        in_specs=[pl.BlockSpec((tn, 128), lambda i: (i, 0))] + [full(a) for a in args],
        out_specs=pl.BlockSpec((tn, 4 * W_GROUP), lambda i: (i, 0)),
        compiler_params=_cparams(1), name="hy_filters",
    )(zfeat, *args)


def _fft_stage1_kernel(a_ref, u_ref, o_ref):
    o_ref[0] = _hp_dot(a_ref[...], u_ref[0]).astype(o_ref.dtype)


def _fft_stage1(a_mat, u_view, row_blk, col_tile):
    nbatch, _, cols = u_view.shape
    m, kdim = a_mat.shape
    return pl.pallas_call(
        _fft_stage1_kernel,
        out_shape=jax.ShapeDtypeStruct((nbatch, m, cols), BF16),
        grid=(nbatch, cols // col_tile),
        in_specs=[pl.BlockSpec((m, kdim), lambda i, j: (0, 0)),
                  pl.BlockSpec((1, kdim, col_tile), lambda i, j: (i, row_blk, j))],
        out_specs=pl.BlockSpec((1, m, col_tile), lambda i, j: (i, 0, j)),
        compiler_params=_cparams(2), name="fft_stage1",
    )(a_mat, u_view)


def _fft_mid_kernel(*refs, n2, multiply):
    if multiply:
        y_ref, g_ref, kf_ref, gi_ref, o_ref = refs
    else:
        y_ref, g_ref, o_ref = refs
    group = range(FFT_K1_GROUP)
    z = [_bdot(g_ref[j], y_ref[0, :, j].reshape(2 * n2, W_GROUP)) for j in group]
    if multiply:
        prod = []
        for j in group:
            zr, zi = z[j][:n2], z[j][n2:]
            kr, ki = kf_ref[0, 0, j].astype(F32), kf_ref[0, 1, j].astype(F32)
            prod.append(jnp.concatenate([zr * kr - zi * ki, zr * ki + zi * kr], axis=0))
        z = [_bdot(gi_ref[j], prod[j]) for j in group]
    for j in group:
        o_ref[0, :, j] = z[j].reshape(2, n2, W_GROUP).astype(o_ref.dtype)


def _fft_mid(y1, g_fwd, kf, g_inv, k1p, n2):
    nbatch = y1.shape[0]
    multiply = kf is not None
    data = lambda fixed: pl.BlockSpec((1, 2, FFT_K1_GROUP, n2, W_GROUP),
                                      (lambda k, i: (0, 0, k, 0, 0)) if fixed else (lambda k, i: (i, 0, k, 0, 0)))
    mat = lambda: pl.BlockSpec((FFT_K1_GROUP, 2 * n2, 2 * n2), lambda k, i: (k, 0, 0))
    in_specs, args = [data(False), mat()], [y1, g_fwd]
    if multiply:
        in_specs += [data(True), mat()]
        args += [kf, g_inv]
    return pl.pallas_call(
        functools.partial(_fft_mid_kernel, n2=n2, multiply=multiply),
        out_shape=jax.ShapeDtypeStruct((nbatch, 2, k1p, n2, W_GROUP), BF16),
        grid=(k1p // FFT_K1_GROUP, nbatch),
        in_specs=in_specs,
        out_specs=data(False),
        compiler_params=_cparams(2), name="fft_mid_mul" if multiply else "fft_mid",
    )(*args)


def _fft_stage4_kernel(a_ref, v_ref, u_ref, gate_ref, skip_ref, o_ref):
    a = a_ref[...]
    a_hi = a.astype(BF16)
    a_lo = (a - a_hi.astype(F32)).astype(BF16)
    v = v_ref[0]
    y = jnp.dot(a_hi, v, preferred_element_type=F32) + jnp.dot(a_lo, v, preferred_element_type=F32)
    o_ref[0] = gate_ref[0] * (y + skip_ref[...] * u_ref[0])


def _fft_stage4(a_mat, v_view, u_view, u_blk, gate_view, gate_blk, skip_row, col_tile):
    nbatch, kdim, cols = v_view.shape
    m = a_mat.shape[0]
    return pl.pallas_call(
        _fft_stage4_kernel,
        out_shape=jax.ShapeDtypeStruct((nbatch, m, cols), F32),
        grid=(nbatch, cols // col_tile),
        in_specs=[pl.BlockSpec((m, kdim), lambda i, j: (0, 0)),
                  pl.BlockSpec((1, kdim, col_tile), lambda i, j: (i, 0, j)),
                  pl.BlockSpec((1, m, col_tile), lambda i, j: (i, u_blk, j)),
                  pl.BlockSpec((1, m, col_tile), lambda i, j: (i, gate_blk, j)),
                  pl.BlockSpec((1, col_tile), lambda i, j: (0, j))],
        out_specs=pl.BlockSpec((1, m, col_tile), lambda i, j: (i, 0, j)),
        compiler_params=_cparams(2), name="fft_stage4",
    )(a_mat, v_view, u_view, gate_view, skip_row)


def _fft_tables(n, n2):
    big = 2 * n
    n1 = big // n2
    two_pi = 2.0 * math.pi

    def cs(num, den):
        ang = (num % den).astype(F32) * (two_pi / den)
        return jnp.cos(ang), jnp.sin(ang)

    k1p = -(-(n1 // 2 + 1) // FFT_K1_GROUP) * FFT_K1_GROUP
    k1 = jnp.arange(k1p, dtype=jnp.int32)
    live = (k1 <= n1 // 2).astype(F32)
    weight = jnp.where((k1 == 0) | (k1 == n1 // 2), 1.0, 2.0) * live
    m1 = jnp.arange(n1, dtype=jnp.int32)
    c, s = cs(k1[:, None] * m1[None, :], n1)
    a1_full = jnp.concatenate([c * live[:, None], -s * live[:, None]], axis=0)
    k2 = jnp.arange(n2, dtype=jnp.int32)
    num = (k2[None, :, None] * k2[None, None, :] * n1 + k1[:, None, None] * k2[None, None, :]) % big
    c, s = cs(num, big)
    g_fwd = jnp.concatenate([jnp.concatenate([c, s], axis=2), jnp.concatenate([-s, c], axis=2)], axis=1)
    num = (k2[None, :, None] * k2[None, None, :] * n1 + k1[:, None, None] * k2[None, :, None]) % big
    c, s = cs(num, big)
    g_inv = jnp.concatenate([jnp.concatenate([c, -s], axis=2), jnp.concatenate([s, c], axis=2)], axis=1)
    t1 = jnp.arange(n1 // 2, dtype=jnp.int32)
    c, s = cs(t1[:, None] * k1[None, :], n1)
    a4 = jnp.concatenate([c * weight[None, :], -s * weight[None, :]], axis=1) * (1.0 / big)
    return n1, k1p, a1_full, g_fwd.astype(BF16), g_inv.astype(BF16), a4


def _hy_positional(n):
    pos = jnp.arange(n, dtype=F32)
    t01 = pos / max(n - 1, 1)
    bands = jnp.linspace(1e-4, HY_BANDS - 1, HY_BANDS, dtype=F32)
    ang = (2.0 * math.pi / n) * pos[:, None] * bands
    z = jnp.concatenate([t01[:, None], jnp.cos(ang), -jnp.sin(ang)], axis=-1)
    return jnp.pad(z, ((0, 0), (0, 128 - z.shape[1])))


def _hyena_segment(v_all, x1_all, x2_all, seg_start, n, n2, filt_args, skip):
    b, lt, c = v_all.shape
    n1, k1p, a1_full, g_fwd, g_inv, a4 = _fft_tables(n, n2)
    a1_half = a1_full[:, :n1 // 2]
    cols = n2 * c
    col_tile = min(cols, 8192)
    seg_blk = seg_start // n
    view = lambda a: a.reshape(b, lt // n2, cols)

    h = _hy_filters(_hy_positional(n), *filt_args)
    h = h.reshape(n, 2, 2, c)
    kern = jnp.concatenate([h[:, 0], jnp.zeros((1, 2, c), F32), jnp.flip(h[1:, 1], axis=0)], axis=0)
    kern = jnp.moveaxis(kern, 1, 0)
    ky = _fft_stage1(a1_full, kern.reshape(2, n1, cols), 0, col_tile)
    kf = _fft_mid(ky.reshape(2, 2, k1p, n2, c), g_fwd, None, None, k1p, n2)

    def long_conv(u_view, u_blk, gate_all, filt):
        y1 = _fft_stage1(a1_half, u_view, u_blk, col_tile)
        vmid = _fft_mid(y1.reshape(b, 2, k1p, n2, c), g_fwd, kf[filt:filt + 1], g_inv, k1p, n2)
        skip_row = jnp.tile(skip[filt], n2).reshape(1, cols)
        return _fft_stage4(a4, vmid.reshape(b, 2 * k1p, cols), u_view, u_blk, view(gate_all), seg_blk,
                           skip_row, col_tile)

    z = long_conv(view(v_all), seg_blk, x1_all, 0)
    out = long_conv(z, 0, x2_all, 1)
    return out.reshape(b, n, c)


def _rope_tables(rows, lt):
    n_freq = DA_DK // 4
    inv = ROPE_THETA ** (-jnp.arange(n_freq, dtype=F32) / n_freq)
    r = jnp.repeat(jnp.arange(rows, dtype=F32), GRID_W)
    col = jnp.tile(jnp.arange(GRID_W, dtype=F32), rows)
    ang_r, ang_c = r[:, None] * inv, col[:, None] * inv
    cos32 = jnp.concatenate([jnp.cos(ang_r)] * 2 + [jnp.cos(ang_c)] * 2, axis=-1)
    sin32 = jnp.concatenate([-jnp.sin(ang_r), jnp.sin(ang_r), -jnp.sin(ang_c), jnp.sin(ang_c)], axis=-1)
    n = rows * GRID_W
    pad = lt - n
    cos = jnp.concatenate([jnp.tile(cos32, (1, W_GROUP // DA_DK)), jnp.ones((pad, W_GROUP), F32)], axis=0)
    sin = jnp.concatenate([jnp.tile(sin32, (1, W_GROUP // DA_DK)), jnp.zeros((pad, W_GROUP), F32)], axis=0)
    return cos, sin


def kernel(x, c, ctx, c_ctx, mod_w, mod_b, ln1_w, ln2_w, w_in, gdn_conv_w, gdn_a_log, gdn_dt_bias, gdn_norm_w, hy_conv_w, hy_w1, hy_b1, hy_f1, hy_w2, hy_b2, hy_f2, hy_w3, hy_decay, hy_bias, hg_lb_raw, hg_norm_w, da_q_norm, da_k_norm, da_lam, da_subln, w_out, mlp_w1, mlp_w2):
    batch, seq, d = x.shape
    n_ctx = ctx.shape[1]
    depth = mod_w.shape[0]
    assert n_ctx == TOKEN_BLOCK and seq % TOKEN_BLOCK == 0 and d == 4 * W_GROUP
    nbl = seq // TOKEN_BLOCK
    lt = seq + n_ctx

    xs = jnp.concatenate([x, ctx], axis=1)
    cond = jnp.concatenate([c, c_ctx[None, :], jnp.zeros((-(batch + 1) % 8, d), F32)], axis=0)
    mods = _modulation(cond, mod_w, mod_b)[:, :batch + 1].reshape(depth, batch + 1, 6, d)
    mods = jnp.pad(mods, ((0, 0), (0, 0), (0, 2), (0, 0)))

    lb_all = jnp.cumsum(jax.nn.softmax(hg_lb_raw.astype(F32), axis=0), axis=0)
    lb_all = lb_all - lb_all[0]
    cos_tab, sin_tab = _rope_tables(seq // GRID_W, lt)

    gdn_cols = 4 * W_GROUP + 4 * N_HEADS
    hy_cols, hg_cols, da_cols = 3 * W_GROUP, 5 * W_GROUP, 3 * W_GROUP
    splits = (GDN_PAD_COLS, hy_cols, hg_cols, da_cols)

    for l in range(depth):
        want_ctx = l < depth - 1
        modtab = mods[l]
        wl = w_in[l]
        w_pad = jnp.concatenate([wl[:, :gdn_cols], jnp.zeros((d, GDN_PAD_COLS - gdn_cols), F32),
                                 wl[:, gdn_cols:]], axis=1).astype(BF16)
        p_gdn, p_hy, p_hg, p_da = _adaln_proj(xs, modtab, ln1_w[l], w_pad, splits, (F32,) * 4, 0, 1, False,
                                              nbl, "in_proj")

        pad16 = lambda a: jnp.pad(a.reshape(1, 2 * N_HEADS), ((0, 0), (0, 128 - 2 * N_HEADS)))
        alog_row, dtb_row = pad16(gdn_a_log[l]), pad16(gdn_dt_bias[l])
        gnw = jnp.tile(gdn_norm_w[l], N_HEADS).reshape(1, W_GROUP)
        o_b = _gdn_direction(p_gdn, gdn_conv_w[l], alog_row, dtb_row, gnw, None, 1, nbl)
        o_gdn = _gdn_direction(p_gdn, gdn_conv_w[l], alog_row, dtb_row, gnw, o_b, 0, nbl)

        hv, hx1, hx2 = _hy_short_conv(p_hy, hy_conv_w[l], nbl)
        w1p = jnp.pad(hy_w1[l], ((0, 128 - hy_w1.shape[1]), (0, 0)))
        row = lambda a: a.reshape(1, -1)
        dec_row = jnp.tile(hy_decay[l].reshape(1, 2 * W_GROUP), (1, 2))
        filt_args = (w1p, row(hy_b1[l]), row(hy_f1[l]), hy_w2[l], row(hy_b2[l]), row(hy_f2[l]), hy_w3[l], dec_row)
        o_hy_lat = _hyena_segment(hv, hx1, hx2, 0, seq, FFT_N2_LAT, filt_args, hy_bias[l])
        if want_ctx:
            o_hy_ctx = _hyena_segment(hv, hx1, hx2, seq, n_ctx, FFT_N2_CTX, filt_args, hy_bias[l])
        else:
            o_hy_ctx = jnp.zeros((batch, n_ctx, W_GROUP), F32)
        o_hy = jnp.concatenate([o_hy_lat, o_hy_ctx], axis=1)

        lb_row = lb_all[l].reshape(1, W_GROUP)
        hnw = jnp.tile(hg_norm_w[l], N_HEADS).reshape(1, W_GROUP)
        o_hb = _hgrn_direction(p_hg, lb_row, hnw, None, 1, nbl)
        o_hg = _hgrn_direction(p_hg, lb_row, hnw, o_hb, 0, nbl)

        lam_init = 0.8 - 0.6 * math.exp(-0.3 * l)
        qw = jnp.tile(da_q_norm[l], W_GROUP // DA_DK).reshape(1, W_GROUP)
        kw = jnp.tile(da_k_norm[l], W_GROUP // DA_DK).reshape(1, W_GROUP)
        sw = jnp.tile(da_subln[l], N_HEADS).reshape(1, W_GROUP)
        qh, kh, vh = _da_prep(p_da, cos_tab, sin_tab, qw, kw)
        o_da = _da_attention(qh, kh, vh, da_lam[l], sw, lam_init, DA_Q_TILE, seq // DA_Q_TILE, 0, lt, 0,
                             "da_attn_lat")
        if want_ctx:
            o_da_ctx = _da_attention(qh, kh, vh, da_lam[l], sw, lam_init, TOKEN_BLOCK, 1, nbl, TOKEN_BLOCK, nbl,
                                     "da_attn_ctx")
        else:
            o_da_ctx = jnp.zeros((batch, n_ctx, W_GROUP), F32)
        o_da = jnp.concatenate([o_da, o_da_ctx], axis=1)

        n_blocks = nbl + 1 if want_ctx else nbl
        xs_mid = _proj_residual(xs, modtab, w_out[l].astype(BF16), [o_gdn, o_hy, o_hg, o_da], 2, n_blocks,
                                nbl, "out_proj")
        (hid,) = _adaln_proj(xs_mid, modtab, ln2_w[l], mlp_w1[l].astype(BF16), (mlp_w1.shape[2],), (BF16,),
                             3, 4, True, nbl, "mlp_up")
        xs = _proj_residual(xs_mid, modtab, mlp_w2[l].astype(BF16), [hid], 5, n_blocks, nbl, "mlp_down")
    return xs
```

```python
import functools
import math

import jax
import jax.numpy as jnp
from jax import lax
from jax.experimental import pallas as pl
from jax.experimental.pallas import tpu as pltpu

F32 = jnp.float32
BF16 = jnp.bfloat16

EPS = 1e-6
TOKEN_BLOCK = 256
CHUNK = 64
HEAD_DIM = 64
N_HEADS = 4
W_GROUP = 256
DA_DK = 32
GRID_W = 64
ROPE_THETA = 10000.0
HY_BANDS = 16
FFT_N2_LAT = 128
FFT_N2_CTX = 16
FFT_K1_GROUP = 8
DA_Q_TILE = 512
GDN_PAD_COLS = 1152
VMEM_LIMIT_BYTES = 56 * 1024 * 1024


def _cparams(n_axes):
    return pltpu.CompilerParams(dimension_semantics=("arbitrary",) * n_axes,
                                vmem_limit_bytes=VMEM_LIMIT_BYTES)


def _sigmoid(x):
    return 1.0 / (1.0 + jnp.exp(-x))


def _silu(x):
    return x * _sigmoid(x)


def _softplus(x):
    return jnp.maximum(x, 0.0) + jnp.log(1.0 + jnp.exp(-jnp.abs(x)))


def _log_sigmoid(x):
    return jnp.minimum(x, 0.0) - jnp.log(1.0 + jnp.exp(-jnp.abs(x)))


def _bdot(a, b):
    return jnp.dot(a.astype(BF16), b.astype(BF16), preferred_element_type=F32)


def _bdot_nt(a, b):
    return lax.dot_general(a.astype(BF16), b.astype(BF16), (((1,), (1,)), ((), ())),
                           preferred_element_type=F32)


def _split3(x):
    hi = x.astype(BF16)
    r = x - hi.astype(F32)
    mid = r.astype(BF16)
    lo = (r - mid.astype(F32)).astype(BF16)
    return hi, mid, lo


def _exact_left(m_bf16, x):
    hi, mid, lo = _split3(x)
    d = lambda p: jnp.dot(m_bf16, p, preferred_element_type=F32)
    return d(hi) + d(mid) + d(lo)


def _exact_right(x, m_bf16):
    hi, mid, lo = _split3(x)
    d = lambda p: jnp.dot(p, m_bf16, preferred_element_type=F32)
    return d(hi) + d(mid) + d(lo)


def _hp_dot(a, b):
    ah = a.astype(BF16)
    al = (a - ah.astype(F32)).astype(BF16)
    bh = b.astype(BF16)
    bl = (b - bh.astype(F32)).astype(BF16)
    d = lambda p, q: jnp.dot(p, q, preferred_element_type=F32)
    return d(ah, bh) + d(ah, bl) + d(al, bh)


def _group_ones(width, group):
    r = lax.broadcasted_iota(jnp.int32, (width, width), 0) // group
    c = lax.broadcasted_iota(jnp.int32, (width, width), 1) // group
    return (r == c).astype(BF16)


def _mod_kernel(s_ref, w_ref, b_ref, o_ref):
    s = _silu(s_ref[...])
    o_ref[0] = _bdot(s, w_ref[0]) + b_ref[0]


def _modulation(cond, mod_w, mod_b):
    depth, d, n = mod_w.shape
    rows = cond.shape[0]
    tn = 1536
    return pl.pallas_call(
        _mod_kernel,
        out_shape=jax.ShapeDtypeStruct((depth, rows, n), F32),
        grid=(depth, n // tn),
        in_specs=[pl.BlockSpec((rows, d), lambda l, j: (0, 0)),
                  pl.BlockSpec((1, d, tn), lambda l, j: (l, 0, j)),
                  pl.BlockSpec((1, 1, tn), lambda l, j: (l, 0, j))],
        out_specs=pl.BlockSpec((1, rows, tn), lambda l, j: (l, 0, j)),
        compiler_params=_cparams(2), name="modulation",
    )(cond, mod_w, mod_b.reshape(depth, 1, n))


def _adaln_proj_kernel(x_ref, mod_ref, lnw_ref, w_ref, *out_refs, shift_row, scale_row, splits, sq_relu):
    x = x_ref[0]
    ms = jnp.mean(x * x, axis=-1, keepdims=True)
    y = x * lax.rsqrt(ms + EPS) * lnw_ref[...]
    mod = mod_ref[0]
    y = y * (1.0 + mod[scale_row:scale_row + 1]) + mod[shift_row:shift_row + 1]
    h = _bdot(y, w_ref[...])
    if sq_relu:
        h = jnp.square(jnp.maximum(h, 0.0))
    off = 0
    for o_ref, n in zip(out_refs, splits):
        o_ref[0] = h[:, off:off + n].astype(o_ref.dtype)
        off += n


def _mod_index(nbl, batch):
    return lambda b, t: (jnp.where(t == nbl, batch, b), 0, 0)


def _adaln_proj(xs, modtab, ln_w, w, splits, out_dtypes, shift_row, scale_row, sq_relu, nbl, name):
    b, lt, d = xs.shape
    nb = lt // TOKEN_BLOCK
    n = w.shape[1]
    kern = functools.partial(_adaln_proj_kernel, shift_row=shift_row, scale_row=scale_row,
                             splits=splits, sq_relu=sq_relu)
    return pl.pallas_call(
        kern,
        out_shape=[jax.ShapeDtypeStruct((b, lt, s), dt) for s, dt in zip(splits, out_dtypes)],
        grid=(b, nb),
        in_specs=[pl.BlockSpec((1, TOKEN_BLOCK, d), lambda i, t: (i, t, 0)),
                  pl.BlockSpec((1, 8, d), _mod_index(nbl, b)),
                  pl.BlockSpec((1, d), lambda i, t: (0, 0)),
                  pl.BlockSpec((d, n), lambda i, t: (0, 0))],
        out_specs=[pl.BlockSpec((1, TOKEN_BLOCK, s), lambda i, t: (i, t, 0)) for s in splits],
        compiler_params=_cparams(2), name=name,
    )(xs, modtab, ln_w.reshape(1, d), w)


def _proj_residual_kernel(x_ref, mod_ref, w_ref, *rest, gate_row, split, nbl):
    a_refs, o_ref = rest[:-1], rest[-1]
    is_ctx = pl.program_id(1) == nbl
    pieces, pos = [], 0
    for two in split:
        lat = a_refs[pos][0].astype(BF16)
        pieces.append(jnp.where(is_ctx, a_refs[pos + 1][0].astype(BF16), lat) if two else lat)
        pos += 2 if two else 1
    a = jnp.concatenate(pieces, axis=1) if len(pieces) > 1 else pieces[0]
    gate = mod_ref[0][gate_row:gate_row + 1]
    o_ref[0] = x_ref[0] + gate * _bdot(a, w_ref[...])


def _proj_residual(xs, modtab, w, acts, gate_row, n_blocks, nbl, name):
    b, lt, d = xs.shape
    split = tuple(isinstance(a, tuple) and a[1] is not None for a in acts)
    specs, args = [], []
    for a, two in zip(acts, split):
        lat = a[0] if isinstance(a, tuple) else a
        width = lat.shape[2]
        if isinstance(a, tuple):
            specs.append(pl.BlockSpec((1, TOKEN_BLOCK, width), lambda i, t: (i, jnp.minimum(t, nbl - 1), 0)))
        else:
            specs.append(pl.BlockSpec((1, TOKEN_BLOCK, width), lambda i, t: (i, t, 0)))
        args.append(lat)
        if two:
            specs.append(pl.BlockSpec((1, TOKEN_BLOCK, width), lambda i, t: (i, 0, 0)))
            args.append(a[1])
    kern = functools.partial(_proj_residual_kernel, gate_row=gate_row, split=split, nbl=nbl)
    return pl.pallas_call(
        kern,
        out_shape=jax.ShapeDtypeStruct((b, n_blocks * TOKEN_BLOCK, d), F32),
        grid=(b, n_blocks),
        in_specs=[pl.BlockSpec((1, TOKEN_BLOCK, d), lambda i, t: (i, t, 0)),
                  pl.BlockSpec((1, 8, d), _mod_index(nbl, b)),
                  pl.BlockSpec(w.shape, lambda i, t: (0, 0))] + specs,
        out_specs=pl.BlockSpec((1, TOKEN_BLOCK, d), lambda i, t: (i, t, 0)),
        compiler_params=_cparams(2), name=name,
    )(xs, modtab, w, *args)


def _scan_block(step, nbl, direction):
    lat = step - 1 if direction == 0 else nbl - step
    return jnp.where(step == 0, nbl, lat)


def _halo_specs(width, nbl, block_of_step, lt):
    rows8 = TOKEN_BLOCK // 8

    def prev_map(b, s):
        return (b, jnp.maximum(block_of_step(s) * rows8 - 1, 0), 0)

    def next_map(b, s):
        return (b, jnp.minimum((block_of_step(s) + 1) * rows8, lt // 8 - 1), 0)

    return [pl.BlockSpec((1, 8, width), prev_map), pl.BlockSpec((1, 8, width), next_map)]


def _short_conv(cur, hp_ref, hn_ref, cw_ref, xp_ref, blk, nbl, width):
    seg_first = jnp.logical_or(blk == 0, blk == nbl)
    seg_last = jnp.logical_or(blk == nbl - 1, blk == nbl)
    xp_ref[0:8, :] = jnp.where(seg_first, 0.0, hp_ref[0][:, :width])
    xp_ref[8:8 + TOKEN_BLOCK, :] = cur
    xp_ref[8 + TOKEN_BLOCK:16 + TOKEN_BLOCK, :] = jnp.where(seg_last, 0.0, hn_ref[0][:, :width])
    cw = cw_ref[...]
    return (cw[0:1] * xp_ref[7:7 + TOKEN_BLOCK, :] + cw[1:2] * xp_ref[8:8 + TOKEN_BLOCK, :]
            + cw[2:3] * xp_ref[9:9 + TOKEN_BLOCK, :])


def _scan_tables(direction, chunk):
    i = jnp.arange(TOKEN_BLOCK, dtype=jnp.int32)[:, None]
    j = jnp.arange(TOKEN_BLOCK, dtype=jnp.int32)[None, :]
    same = (i // chunk) == (j // chunk)
    pos = (lambda t: t % chunk) if direction == 0 else (lambda t: chunk - 1 - t % chunk)
    pi, pj = pos(i), pos(j)
    eye = i == j
    fl = [same & (pj <= pi), same & (pj < pi), eye]
    bl = [same & (pj <= pi), (i // HEAD_DIM) == (j // HEAD_DIM), eye]
    m = 1
    while m < chunk:
        fl.append(same & ((pi // (2 * m)) == (pj // (2 * m))) & ((pi // m) % 2 == 1) & ((pj // m) % 2 == 0))
        boundary = (pi // (2 * m)) * (2 * m) + m - 1
        bl.append(eye.astype(F32) - (same & (pj == boundary)).astype(F32))
        m *= 2
    return jnp.stack([a.astype(F32) for a in fl]), jnp.stack([a.astype(BF16) for a in bl])


def _group_rms(o, gmat, group, weight):
    ms = jnp.dot((o * o).astype(BF16), gmat, preferred_element_type=F32) * (1.0 / group)
    return o * lax.rsqrt(ms + EPS) * weight


def _gdn_kernel(*refs, direction, finish, nbl):
    if finish:
        (p_ref, hp_ref, hn_ref, cw_ref, alog_ref, dtb_ref, nw_ref, fm_ref, bm_ref, other_ref,
         o_ref, s_ref, xp_ref, ob_ref) = refs
    else:
        p_ref, hp_ref, hn_ref, cw_ref, alog_ref, dtb_ref, nw_ref, fm_ref, bm_ref, o_ref, s_ref, xp_ref, ob_ref = refs
    step = pl.program_id(1)
    blk = _scan_block(step, nbl, direction)

    @pl.when(step == 0)
    def _():
        s_ref[...] = jnp.zeros_like(s_ref)

    p = p_ref[0]
    xc = _silu(_short_conv(p[:, :3 * W_GROUP], hp_ref, hn_ref, cw_ref, xp_ref, blk, nbl, 3 * W_GROUP))
    g64 = bm_ref[1]
    q = xc[:, :W_GROUP]
    k = xc[:, W_GROUP:2 * W_GROUP]
    v = xc[:, 2 * W_GROUP:]
    q = q * lax.rsqrt(_bdot(q * q, g64) + EPS) * (HEAD_DIM ** -0.5)
    k = k * lax.rsqrt(_bdot(k * k, g64) + EPS)
    k_t = _bdot_nt(bm_ref[2], k)

    ab = p[:, 4 * W_GROUP:4 * W_GROUP + 128]
    lane = lax.broadcasted_iota(jnp.int32, ab.shape, 1)
    g_all = -jnp.exp(alog_ref[...]) * _softplus(ab + dtb_ref[...])
    w_all = jnp.where(lane < 2 * N_HEADS, g_all, _sigmoid(ab))
    z_col = jnp.where(lane < 2 * N_HEADS, _exact_left(bm_ref[0], w_all), w_all)
    z_row = z_col.T

    heads = range(N_HEADS)
    lanes = [slice(h * HEAD_DIM, (h + 1) * HEAD_DIM) for h in heads]
    gidx = [N_HEADS * direction + h for h in heads]
    incl = fm_ref[0] > 0.5
    gc_r = [z_row[gi:gi + 1, :] for gi in gidx]
    beta = [z_col[:, 2 * N_HEADS + gi:2 * N_HEADS + gi + 1] for gi in gidx]
    eg = [jnp.exp(z_col[:, gi:gi + 1]) for gi in gidx]
    decay = [jnp.exp(jnp.where(incl, z_col[:, gi:gi + 1] - gc_r[h], -jnp.inf)) for h, gi in enumerate(gidx)]
    kk = [_bdot_nt(k[:, ls], k[:, ls]) for ls in lanes]
    qk = [_bdot_nt(q[:, ls], k[:, ls]) for ls in lanes]
    n = [kk[h] * beta[h] * decay[h] * fm_ref[1] for h in heads]
    inv = [fm_ref[2] - n[h] * fm_ref[3] for h in heads]
    for lev in range(1, 6):
        half = fm_ref[3 + lev]
        x = [_bdot(inv[h], n[h] * half) for h in heads]
        inv = [inv[h] - _bdot(x[h], inv[h]) for h in heads]
    uw = [_bdot(inv[h], jnp.concatenate([v[:, ls] * beta[h], k[:, ls] * (beta[h] * eg[h])], axis=1))
          for h, ls in enumerate(lanes)]
    attn = [qk[h] * decay[h] for h in heads]
    qg = [q[:, ls] * eg[h] for h, ls in enumerate(lanes)]
    state = [s_ref[h] for h in heads]

    last = CHUNK - 1 if direction == 0 else 0
    n_chunks = TOKEN_BLOCK // CHUNK
    for ci in range(n_chunks):
        c = ci if direction == 0 else n_chunks - 1 - ci
        rs = slice(c * CHUNK, (c + 1) * CHUNK)
        g_last = [z_col[c * CHUNK + last:c * CHUNK + last + 1, gi:gi + 1] for gi in gidx]
        v_new = [uw[h][rs, :HEAD_DIM] - _bdot(uw[h][rs, HEAD_DIM:], state[h]) for h in heads]
        for h, ls in enumerate(lanes):
            ob_ref[rs, ls] = _bdot(qg[h][rs], state[h]) + _bdot(attn[h][rs, rs], v_new[h])
        state = [state[h] * jnp.exp(g_last[h])
                 + _bdot(k_t[ls, rs] * jnp.exp(g_last[h] - gc_r[h][:, rs]), v_new[h]) for h, ls in enumerate(lanes)]
    for h in heads:
        s_ref[h] = state[h]

    if finish:
        o = ob_ref[...] + other_ref[0]
        gate = _silu(p[:, 3 * W_GROUP:4 * W_GROUP])
        o_ref[0] = _group_rms(o, g64, HEAD_DIM, nw_ref[...]) * gate
    else:
        o_ref[0] = ob_ref[...]


def _gdn_direction(p_gdn, conv_w, alog_row, dtb_row, nw_row, other, direction, nbl):
    b, lt, width = p_gdn.shape
    nb = nbl + 1
    blk_of = lambda s: _scan_block(s, nbl, direction)
    finish = other is not None
    fmask, bmask = _scan_tables(direction, CHUNK)
    bmask = bmask[:3]
    kern = functools.partial(_gdn_kernel, direction=direction, finish=finish, nbl=nbl)
    in_specs = ([pl.BlockSpec((1, TOKEN_BLOCK, width), lambda i, s: (i, blk_of(s), 0))]
                + _halo_specs(3 * W_GROUP, nbl, blk_of, lt)
                + [pl.BlockSpec((3, 3 * W_GROUP), lambda i, s: (0, 0)),
                   pl.BlockSpec((1, 128), lambda i, s: (0, 0)),
                   pl.BlockSpec((1, 128), lambda i, s: (0, 0)),
                   pl.BlockSpec((1, W_GROUP), lambda i, s: (0, 0)),
                   pl.BlockSpec(fmask.shape, lambda i, s: (0, 0, 0)),
                   pl.BlockSpec(bmask.shape, lambda i, s: (0, 0, 0))])
    args = [p_gdn, p_gdn, p_gdn, conv_w, alog_row, dtb_row, nw_row, fmask, bmask]
    if finish:
        in_specs.append(pl.BlockSpec((1, TOKEN_BLOCK, W_GROUP), lambda i, s: (i, blk_of(s), 0)))
        args.append(other)
    return pl.pallas_call(
        kern,
        out_shape=jax.ShapeDtypeStruct((b, lt, W_GROUP), F32),
        grid=(b, nb),
        in_specs=in_specs,
        out_specs=pl.BlockSpec((1, TOKEN_BLOCK, W_GROUP), lambda i, s: (i, blk_of(s), 0)),
        scratch_shapes=[pltpu.VMEM((N_HEADS, HEAD_DIM, HEAD_DIM), F32),
                        pltpu.VMEM((TOKEN_BLOCK + 16, 3 * W_GROUP), F32),
                        pltpu.VMEM((TOKEN_BLOCK, W_GROUP), F32)],
        compiler_params=_cparams(2), name=f"gdn_dir{direction}",
    )(*args)


def _hgrn_kernel(*refs, direction, finish, nbl):
    if finish:
        p_ref, lb_ref, nw_ref, fm_ref, bm_ref, other_ref, o_ref, s_ref, ob_ref = refs
    else:
        p_ref, lb_ref, nw_ref, fm_ref, bm_ref, o_ref, s_ref, ob_ref = refs
    step = pl.program_id(1)

    @pl.when(step == 0)
    def _():
        s_ref[...] = jnp.zeros_like(s_ref)

    p = p_ref[0]
    lb = lb_ref[...]
    q = _silu(p[:, :W_GROUP])
    v = p[:, W_GROUP:2 * W_GROUP]
    fl = p[:, (2 + direction) * W_GROUP:(3 + direction) * W_GROUP]
    a = jnp.log(lb)
    bb = jnp.log(1.0 - lb) + _log_sigmoid(fl)
    logf = jnp.maximum(a, bb) + jnp.log(1.0 + jnp.exp(-jnp.abs(a - bb)))
    key = (1.0 - lb) * _sigmoid(-fl)
    gcum = _exact_left(bm_ref[0], logf)
    pieces = _split3(gcum)
    n_levels = fm_ref.shape[0] - 3
    lev_e = []
    for lev in range(n_levels):
        dm = bm_ref[3 + lev]
        d = sum(jnp.dot(dm, piece, preferred_element_type=F32) for piece in pieces)
        lev_e.append(jnp.exp(-jnp.abs(d)))
    last = TOKEN_BLOCK - 1 if direction == 0 else 0
    g_end = gcum[last:last + 1, :]
    qg = q * jnp.exp(gcum)
    kdec = key * jnp.exp(g_end - gcum)
    decay_end = jnp.exp(g_end)
    v_t = _bdot_nt(bm_ref[2], v)

    lanes = [slice(h * HEAD_DIM, (h + 1) * HEAD_DIM) for h in range(N_HEADS)]
    amat = [_bdot_nt(q[:, ls], key[:, ls]) * fm_ref[2] for ls in lanes]
    for lev in range(n_levels):
        qe, ke = q * lev_e[lev], key * lev_e[lev]
        half = fm_ref[3 + lev]
        amat = [amat[h] + half * _bdot_nt(qe[:, ls], ke[:, ls]) for h, ls in enumerate(lanes)]
    for h, ls in enumerate(lanes):
        st = s_ref[h]
        ob_ref[:, ls] = _bdot_nt(qg[:, ls], st) + _bdot(amat[h], v[:, ls])
        s_ref[h] = st * decay_end[:, ls] + _bdot(v_t[ls, :], kdec[:, ls])

    if finish:
        o = ob_ref[...] + other_ref[0]
        gate = _silu(p[:, 4 * W_GROUP:5 * W_GROUP])
        o_ref[0] = _group_rms(o, bm_ref[1], HEAD_DIM, nw_ref[...]) * gate
    else:
        o_ref[0] = ob_ref[...]


def _hgrn_direction(p_hg, lb_row, nw_row, other, direction, nbl):
    b, lt, width = p_hg.shape
    nb = nbl + 1
    blk_of = lambda s: _scan_block(s, nbl, direction)
    finish = other is not None
    fmask, bmask = _scan_tables(direction, TOKEN_BLOCK)
    kern = functools.partial(_hgrn_kernel, direction=direction, finish=finish, nbl=nbl)
    in_specs = [pl.BlockSpec((1, TOKEN_BLOCK, width), lambda i, s: (i, blk_of(s), 0)),
                pl.BlockSpec((1, W_GROUP), lambda i, s: (0, 0)),
                pl.BlockSpec((1, W_GROUP), lambda i, s: (0, 0)),
                pl.BlockSpec(fmask.shape, lambda i, s: (0, 0, 0)),
                pl.BlockSpec(bmask.shape, lambda i, s: (0, 0, 0))]
    args = [p_hg, lb_row, nw_row, fmask, bmask]
    if finish:
        in_specs.append(pl.BlockSpec((1, TOKEN_BLOCK, W_GROUP), lambda i, s: (i, blk_of(s), 0)))
        args.append(other)
    return pl.pallas_call(
        kern,
        out_shape=jax.ShapeDtypeStruct((b, lt, W_GROUP), F32),
        grid=(b, nb),
        in_specs=in_specs,
        out_specs=pl.BlockSpec((1, TOKEN_BLOCK, W_GROUP), lambda i, s: (i, blk_of(s), 0)),
        scratch_shapes=[pltpu.VMEM((N_HEADS, HEAD_DIM, HEAD_DIM), F32),
                        pltpu.VMEM((TOKEN_BLOCK, W_GROUP), F32)],
        compiler_params=_cparams(2), name=f"hgrn_dir{direction}",
    )(*args)


def _da_prep_kernel(p_ref, cos_ref, sin_ref, qw_ref, kw_ref, q_ref, k_ref, v_ref):
    p = p_ref[0]
    g32 = _group_ones(W_GROUP, DA_DK)
    lane = lax.broadcasted_iota(jnp.int32, (TOKEN_BLOCK, W_GROUP), 1)
    first_half = (lane % 16) < 8
    cos, sin = cos_ref[...], sin_ref[...]

    def norm_rope(x, w):
        ms = _bdot(x * x, g32) * (1.0 / DA_DK)
        y = x * lax.rsqrt(ms + EPS) * w
        partner = jnp.where(first_half, pltpu.roll(y, W_GROUP - 8, axis=1), pltpu.roll(y, 8, axis=1))
        return y * cos + partner * sin

    q_ref[0] = (norm_rope(p[:, :W_GROUP], qw_ref[...]) * (DA_DK ** -0.5)).astype(BF16)
    k_ref[0] = norm_rope(p[:, W_GROUP:2 * W_GROUP], kw_ref[...]).astype(BF16)
    dst = lax.broadcasted_iota(jnp.int32, (2 * W_GROUP, W_GROUP), 0)
    src = lax.broadcasted_iota(jnp.int32, (2 * W_GROUP, W_GROUP), 1)
    spread = (dst == (src // HEAD_DIM) * 128 + src % HEAD_DIM).astype(BF16)
    wide_t = _bdot_nt(spread, p[:, 2 * W_GROUP:])
    row_w = lax.broadcasted_iota(jnp.int32, wide_t.shape, 0)
    v_ref[0] = jnp.where(row_w % 128 == HEAD_DIM, 1.0, wide_t).astype(BF16)


def _da_prep(p_da, cos_tab, sin_tab, qw_row, kw_row):
    b, lt, width = p_da.shape
    nb = lt // TOKEN_BLOCK
    blk = lambda w=W_GROUP: pl.BlockSpec((1, TOKEN_BLOCK, w), lambda i, t: (i, t, 0))
    return pl.pallas_call(
        _da_prep_kernel,
        out_shape=[jax.ShapeDtypeStruct((b, lt, W_GROUP), BF16)] * 2
                  + [jax.ShapeDtypeStruct((b, 2 * W_GROUP, lt), BF16)],
        grid=(b, nb),
        in_specs=[pl.BlockSpec((1, TOKEN_BLOCK, width), lambda i, t: (i, t, 0)),
                  pl.BlockSpec((TOKEN_BLOCK, W_GROUP), lambda i, t: (t, 0)),
                  pl.BlockSpec((TOKEN_BLOCK, W_GROUP), lambda i, t: (t, 0)),
                  pl.BlockSpec((1, W_GROUP), lambda i, t: (0, 0)),
                  pl.BlockSpec((1, W_GROUP), lambda i, t: (0, 0))],
        out_specs=[blk(), blk(), pl.BlockSpec((1, 2 * W_GROUP, TOKEN_BLOCK), lambda i, t: (i, 0, t))],
        compiler_params=_cparams(2), name="da_prep",
    )(p_da, cos_tab, sin_tab, qw_row, kw_row)


def _da_attn_kernel(q_ref, k_ref, v_ref, lam_ref, sw_ref, o_ref, *, lam_init):
    q = q_ref[0]
    k = k_ref[0]
    lp = lam_ref[...]
    lam = (jnp.exp(jnp.sum(lp[0:1] * lp[1:2], axis=1, keepdims=True))
           - jnp.exp(jnp.sum(lp[2:3] * lp[3:4], axis=1, keepdims=True)) + lam_init)
    lane = lax.broadcasted_iota(jnp.int32, q.shape, 1)

    def scores(m):
        lo = (m // 2) * HEAD_DIM + (m % 2) * DA_DK
        qm = jnp.where(jnp.logical_and(lane >= lo, lane < lo + DA_DK), q, jnp.zeros_like(q))
        return lax.dot_general(k, qm, (((1,), (1,)), ((), ())), preferred_element_type=F32).astype(BF16)

    n_maps = 2 * N_HEADS
    maps = []
    s_next = scores(0)
    for m in range(n_maps):
        s = s_next
        if m + 1 < n_maps:
            s_next = scores(m + 1)
        e = jnp.exp(s - jnp.max(s, axis=0, keepdims=True))
        vt_h = v_ref[0, (m // 2) * 128:(m // 2 + 1) * 128, :]
        acc = jnp.dot(vt_h, e, preferred_element_type=F32)
        maps.append(acc / acc[HEAD_DIM:HEAD_DIM + 1, :])
    outs = [(maps[2 * h] - lam * maps[2 * h + 1])[:HEAD_DIM] for h in range(N_HEADS)]
    o = jnp.concatenate(outs, axis=0).T
    g64 = _group_ones(W_GROUP, HEAD_DIM)
    o_ref[0] = _group_rms(o, g64, HEAD_DIM, sw_ref[...]) * (1.0 - lam_init)


def _da_attention(qh, kh, vh, lam_p, sw_row, lam_init, q_tile, q_blocks, q_off, k_rows, k_blk, name):
    b, lt, _ = qh.shape
    kern = functools.partial(_da_attn_kernel, lam_init=lam_init)
    return pl.pallas_call(
        kern,
        out_shape=jax.ShapeDtypeStruct((b, q_blocks * q_tile, W_GROUP), F32),
        grid=(b, q_blocks),
        in_specs=[pl.BlockSpec((1, q_tile, W_GROUP), lambda i, t: (i, t + q_off, 0)),
                  pl.BlockSpec((1, k_rows, W_GROUP), lambda i, t: (i, k_blk, 0)),
                  pl.BlockSpec((1, 2 * W_GROUP, k_rows), lambda i, t: (i, 0, k_blk)),
                  pl.BlockSpec((4, DA_DK), lambda i, t: (0, 0)),
                  pl.BlockSpec((1, W_GROUP), lambda i, t: (0, 0))],
        out_specs=pl.BlockSpec((1, q_tile, W_GROUP), lambda i, t: (i, t, 0)),
        compiler_params=_cparams(2), name=name,
    )(qh, kh, vh, lam_p, sw_row)


def _hy_conv_kernel(p_ref, hp_ref, hn_ref, cw_ref, v_ref, x1_ref, x2_ref, xp_ref, *, nbl):
    blk = pl.program_id(1)
    xc = _short_conv(p_ref[0], hp_ref, hn_ref, cw_ref, xp_ref, blk, nbl, 3 * W_GROUP)
    v_ref[0] = xc[:, :W_GROUP].astype(v_ref.dtype)
    x1_ref[0] = xc[:, W_GROUP:2 * W_GROUP].astype(x1_ref.dtype)
    x2_ref[0] = xc[:, 2 * W_GROUP:].astype(x2_ref.dtype)


def _hy_short_conv(p_hy, conv_w, nbl):
    b, lt, width = p_hy.shape
    nb = nbl + 1
    blk = lambda: pl.BlockSpec((1, TOKEN_BLOCK, W_GROUP), lambda i, t: (i, t, 0))
    return pl.pallas_call(
        functools.partial(_hy_conv_kernel, nbl=nbl),
        out_shape=[jax.ShapeDtypeStruct((b, lt, W_GROUP), BF16)] * 3,
        grid=(b, nb),
        in_specs=[pl.BlockSpec((1, TOKEN_BLOCK, width), lambda i, t: (i, t, 0))]
                 + _halo_specs(width, nbl, lambda s: s, lt)
                 + [pl.BlockSpec((3, width), lambda i, t: (0, 0))],
        out_specs=[blk(), blk(), blk()],
        scratch_shapes=[pltpu.VMEM((TOKEN_BLOCK + 16, width), F32)],
        compiler_params=_cparams(2), name="hy_short_conv",
    )(p_hy, p_hy, p_hy, conv_w)


def _hy_filter_kernel(z_ref, w1_ref, b1_ref, f1_ref, w2_ref, b2_ref, f2_ref, w3_ref, dec_ref, o_ref):
    z = z_ref[...]
    h = jnp.sin(f1_ref[...] * (_hp_dot(z, w1_ref[...]) + b1_ref[...]))
    h = jnp.sin(f2_ref[...] * (_hp_dot(h, w2_ref[...]) + b2_ref[...]))
    h = _hp_dot(h, w3_ref[...])
    h = h * jnp.exp(-z[:, 0:1] * jnp.abs(dec_ref[...]))
    first = (pl.program_id(0) * h.shape[0] + lax.broadcasted_iota(jnp.int32, (h.shape[0], 1), 0)) == 0
    for plane in range(4):
        piece = h[:, plane * W_GROUP:(plane + 1) * W_GROUP]
        o_ref[plane] = jnp.where(first, 0.0, piece) if plane >= 2 else piece


def _hy_filters(zfeat, w1p, b1, f1, w2, b2, f2, w3, dec_row):
    n = zfeat.shape[0]
    tn = min(n, 512)
    full = lambda a: pl.BlockSpec(a.shape, lambda i: (0,) * a.ndim)
    args = [w1p, b1, f1, w2, b2, f2, w3, dec_row]
    return pl.pallas_call(
        _hy_filter_kernel,
        out_shape=jax.ShapeDtypeStruct((4, n, W_GROUP), F32),
        grid=(n // tn,),
        in_specs=[pl.BlockSpec((tn, 128), lambda i: (i, 0))] + [full(a) for a in args],
        out_specs=pl.BlockSpec((4, tn, W_GROUP), lambda i: (0, i, 0)),
        compiler_params=_cparams(1), name="hy_filters",
    )(zfeat, *args)


def _left_split_dot(a, v):
    a_hi = a.astype(BF16)
    a_lo = (a - a_hi.astype(F32)).astype(BF16)
    return jnp.dot(a_hi, v, preferred_element_type=F32) + jnp.dot(a_lo, v, preferred_element_type=F32)


def _fft_stage1_kernel(a_ref, u_ref, o_ref):
    u = u_ref[0]
    y = _left_split_dot(a_ref[...], u) if u.dtype == BF16 else _hp_dot(a_ref[...], u)
    o_ref[0] = y.astype(o_ref.dtype)


def _fft_stage1(a_mat, u_view, row_blk, col_tile):
    nbatch, _, cols = u_view.shape
    m, kdim = a_mat.shape
    return pl.pallas_call(
        _fft_stage1_kernel,
        out_shape=jax.ShapeDtypeStruct((nbatch, m, cols), BF16),
        grid=(nbatch, cols // col_tile),
        in_specs=[pl.BlockSpec((m, kdim), lambda i, j: (0, 0)),
                  pl.BlockSpec((1, kdim, col_tile), lambda i, j: (i, row_blk, j))],
        out_specs=pl.BlockSpec((1, m, col_tile), lambda i, j: (i, 0, j)),
        compiler_params=_cparams(2), name="fft_stage1",
    )(a_mat, u_view)


def _fft_mid_kernel(*refs, n2, multiply):
    if multiply:
        y_ref, g_ref, kf_ref, kb_ref, gi_ref, o_ref = refs
    else:
        y_ref, g_ref, o_ref = refs
    group = range(FFT_K1_GROUP)
    z = [_bdot(g_ref[j], y_ref[0, :, j].reshape(2 * n2, W_GROUP)) for j in group]
    if multiply:
        prod = []
        for j in group:
            zr, zi = z[j][:n2], z[j][n2:]
            kr = kf_ref[0, 0, j].astype(F32) + kb_ref[0, 0, j].astype(F32)
            ki = kf_ref[0, 1, j].astype(F32) - kb_ref[0, 1, j].astype(F32)
            prod.append(jnp.concatenate([zr * kr - zi * ki, zr * ki + zi * kr], axis=0))
        z = [_bdot(gi_ref[j], prod[j]) for j in group]
    for j in group:
        o_ref[0, :, j] = z[j].reshape(2, n2, W_GROUP).astype(o_ref.dtype)


def _fft_mid(y1, g_fwd, kf, filt, g_inv, k1p, n2):
    nbatch = y1.shape[0]
    multiply = kf is not None
    data = lambda plane: pl.BlockSpec((1, 2, FFT_K1_GROUP, n2, W_GROUP),
                                      (lambda k, i: (i, 0, k, 0, 0)) if plane is None
                                      else (lambda k, i: (plane, 0, k, 0, 0)))
    mat = lambda: pl.BlockSpec((FFT_K1_GROUP, 2 * n2, 2 * n2), lambda k, i: (k, 0, 0))
    in_specs, args = [data(None), mat()], [y1, g_fwd]
    if multiply:
        in_specs += [data(filt), data(2 + filt), mat()]
        args += [kf, kf, g_inv]
    return pl.pallas_call(
        functools.partial(_fft_mid_kernel, n2=n2, multiply=multiply),
        out_shape=jax.ShapeDtypeStruct((nbatch, 2, k1p, n2, W_GROUP), BF16),
        grid=(k1p // FFT_K1_GROUP, nbatch),
        in_specs=in_specs,
        out_specs=data(None),
        compiler_params=_cparams(2), name="fft_mid_mul" if multiply else "fft_mid",
    )(*args)


def _fft_stage4_kernel(a_ref, v_ref, u_ref, gate_ref, skip_ref, o_ref):
    y = _left_split_dot(a_ref[...], v_ref[0])
    o_ref[0] = (gate_ref[0].astype(F32) * (y + skip_ref[...] * u_ref[0].astype(F32))).astype(o_ref.dtype)


def _fft_stage4(a_mat, v_view, u_view, u_blk, gate_view, gate_blk, skip_row, col_tile):
    nbatch, kdim, cols = v_view.shape
    m = a_mat.shape[0]
    return pl.pallas_call(
        _fft_stage4_kernel,
        out_shape=jax.ShapeDtypeStruct((nbatch, m, cols), BF16),
        grid=(nbatch, cols // col_tile),
        in_specs=[pl.BlockSpec((m, kdim), lambda i, j: (0, 0)),
                  pl.BlockSpec((1, kdim, col_tile), lambda i, j: (i, 0, j)),
                  pl.BlockSpec((1, m, col_tile), lambda i, j: (i, u_blk, j)),
                  pl.BlockSpec((1, m, col_tile), lambda i, j: (i, gate_blk, j)),
                  pl.BlockSpec((1, col_tile), lambda i, j: (0, j))],
        out_specs=pl.BlockSpec((1, m, col_tile), lambda i, j: (i, 0, j)),
        compiler_params=_cparams(2), name="fft_stage4",
    )(a_mat, v_view, u_view, gate_view, skip_row)


def _fft_tables(n, n2):
    big = 2 * n
    n1 = big // n2
    two_pi = 2.0 * math.pi

    def cs(num, den):
        ang = (num % den).astype(F32) * (two_pi / den)
        return jnp.cos(ang), jnp.sin(ang)

    k1p = -(-(n1 // 2 + 1) // FFT_K1_GROUP) * FFT_K1_GROUP
    k1 = jnp.arange(k1p, dtype=jnp.int32)
    live = (k1 <= n1 // 2).astype(F32)
    weight = jnp.where((k1 == 0) | (k1 == n1 // 2), 1.0, 2.0) * live
    m1 = jnp.arange(n1, dtype=jnp.int32)
    c, s = cs(k1[:, None] * m1[None, :], n1)
    a1_full = jnp.concatenate([c * live[:, None], -s * live[:, None]], axis=0)
    k2 = jnp.arange(n2, dtype=jnp.int32)
    num = (k2[None, :, None] * k2[None, None, :] * n1 + k1[:, None, None] * k2[None, None, :]) % big
    c, s = cs(num, big)
    g_fwd = jnp.concatenate([jnp.concatenate([c, s], axis=2), jnp.concatenate([-s, c], axis=2)], axis=1)
    num = (k2[None, :, None] * k2[None, None, :] * n1 + k1[:, None, None] * k2[None, :, None]) % big
    c, s = cs(num, big)
    g_inv = jnp.concatenate([jnp.concatenate([c, -s], axis=2), jnp.concatenate([s, c], axis=2)], axis=1)
    t1 = jnp.arange(n1 // 2, dtype=jnp.int32)
    c, s = cs(t1[:, None] * k1[None, :], n1)
    a4 = jnp.concatenate([c * weight[None, :], -s * weight[None, :]], axis=1) * (1.0 / big)
    return n1, k1p, a1_full, g_fwd.astype(BF16), g_inv.astype(BF16), a4


def _hy_positional(n):
    pos = jnp.arange(n, dtype=F32)
    t01 = pos / max(n - 1, 1)
    bands = jnp.linspace(1e-4, HY_BANDS - 1, HY_BANDS, dtype=F32)
    ang = (2.0 * math.pi / n) * pos[:, None] * bands
    z = jnp.concatenate([t01[:, None], jnp.cos(ang), -jnp.sin(ang)], axis=-1)
    return jnp.pad(z, ((0, 0), (0, 128 - z.shape[1])))


def _hyena_segment(v_all, x1_all, x2_all, seg_start, n, n2, filt_args, skip):
    b, lt, c = v_all.shape
    n1, k1p, a1_full, g_fwd, g_inv, a4 = _fft_tables(n, n2)
    a1_half = a1_full[:, :n1 // 2]
    cols = n2 * c
    col_tile = min(cols, 8192)
    seg_blk = seg_start // n
    view = lambda a: a.reshape(b, lt // n2, cols)

    h = _hy_filters(_hy_positional(n), *filt_args)
    ky = _fft_stage1(a1_half, h.reshape(4, n1 // 2, cols), 0, col_tile)
    kf = _fft_mid(ky.reshape(4, 2, k1p, n2, c), g_fwd, None, 0, None, k1p, n2)

    def long_conv(u_view, u_blk, gate_all, filt):
        y1 = _fft_stage1(a1_half, u_view, u_blk, col_tile)
        vmid = _fft_mid(y1.reshape(b, 2, k1p, n2, c), g_fwd, kf, filt, g_inv, k1p, n2)
        skip_row = jnp.tile(skip[filt], n2).reshape(1, cols)
        return _fft_stage4(a4, vmid.reshape(b, 2 * k1p, cols), u_view, u_blk, view(gate_all), seg_blk,
                           skip_row, col_tile)

    z = long_conv(view(v_all), seg_blk, x1_all, 0)
    out = long_conv(z, 0, x2_all, 1)
    return out.reshape(b, n, c)


def _rope_tables(rows, lt):
    n_freq = DA_DK // 4
    inv = ROPE_THETA ** (-jnp.arange(n_freq, dtype=F32) / n_freq)
    r = jnp.repeat(jnp.arange(rows, dtype=F32), GRID_W)
    col = jnp.tile(jnp.arange(GRID_W, dtype=F32), rows)
    ang_r, ang_c = r[:, None] * inv, col[:, None] * inv
    cos32 = jnp.concatenate([jnp.cos(ang_r)] * 2 + [jnp.cos(ang_c)] * 2, axis=-1)
    sin32 = jnp.concatenate([-jnp.sin(ang_r), jnp.sin(ang_r), -jnp.sin(ang_c), jnp.sin(ang_c)], axis=-1)
    n = rows * GRID_W
    pad = lt - n
    cos = jnp.concatenate([jnp.tile(cos32, (1, W_GROUP // DA_DK)), jnp.ones((pad, W_GROUP), F32)], axis=0)
    sin = jnp.concatenate([jnp.tile(sin32, (1, W_GROUP // DA_DK)), jnp.zeros((pad, W_GROUP), F32)], axis=0)
    return cos, sin


def kernel(x, c, ctx, c_ctx, mod_w, mod_b, ln1_w, ln2_w, w_in, gdn_conv_w, gdn_a_log, gdn_dt_bias, gdn_norm_w, hy_conv_w, hy_w1, hy_b1, hy_f1, hy_w2, hy_b2, hy_f2, hy_w3, hy_decay, hy_bias, hg_lb_raw, hg_norm_w, da_q_norm, da_k_norm, da_lam, da_subln, w_out, mlp_w1, mlp_w2):
    batch, seq, d = x.shape
    n_ctx = ctx.shape[1]
    depth = mod_w.shape[0]
    assert n_ctx == TOKEN_BLOCK and seq % TOKEN_BLOCK == 0 and d == 4 * W_GROUP
    nbl = seq // TOKEN_BLOCK
    lt = seq + n_ctx

    xs = jnp.concatenate([x, ctx], axis=1)
    cond = jnp.concatenate([c, c_ctx[None, :], jnp.zeros((-(batch + 1) % 8, d), F32)], axis=0)
    mods = _modulation(cond, mod_w, mod_b)[:, :batch + 1].reshape(depth, batch + 1, 6, d)
    mods = jnp.pad(mods, ((0, 0), (0, 0), (0, 2), (0, 0)))

    lb_all = jnp.cumsum(jax.nn.softmax(hg_lb_raw.astype(F32), axis=0), axis=0)
    lb_all = lb_all - lb_all[0]
    cos_tab, sin_tab = _rope_tables(seq // GRID_W, lt)

    gdn_cols = 4 * W_GROUP + 4 * N_HEADS
    hy_cols, hg_cols, da_cols = 3 * W_GROUP, 5 * W_GROUP, 3 * W_GROUP
    splits = (GDN_PAD_COLS, hy_cols, hg_cols, da_cols)

    for l in range(depth):
        want_ctx = l < depth - 1
        modtab = mods[l]
        wl = w_in[l]
        w_pad = jnp.concatenate([wl[:, :gdn_cols], jnp.zeros((d, GDN_PAD_COLS - gdn_cols), F32),
                                 wl[:, gdn_cols:]], axis=1).astype(BF16)
        p_gdn, p_hy, p_hg, p_da = _adaln_proj(xs, modtab, ln1_w[l], w_pad, splits, (F32,) * 4, 0, 1, False,
                                              nbl, "in_proj")

        pad16 = lambda a: jnp.pad(a.reshape(1, 2 * N_HEADS), ((0, 0), (0, 128 - 2 * N_HEADS)))
        alog_row, dtb_row = pad16(gdn_a_log[l]), pad16(gdn_dt_bias[l])
        gnw = jnp.tile(gdn_norm_w[l], N_HEADS).reshape(1, W_GROUP)
        o_b = _gdn_direction(p_gdn, gdn_conv_w[l], alog_row, dtb_row, gnw, None, 1, nbl)
        o_gdn = _gdn_direction(p_gdn, gdn_conv_w[l], alog_row, dtb_row, gnw, o_b, 0, nbl)

        hv, hx1, hx2 = _hy_short_conv(p_hy, hy_conv_w[l], nbl)
        w1p = jnp.pad(hy_w1[l], ((0, 128 - hy_w1.shape[1]), (0, 0)))
        row = lambda a: a.reshape(1, -1)
        dec_row = jnp.tile(hy_decay[l].reshape(1, 2 * W_GROUP), (1, 2))
        filt_args = (w1p, row(hy_b1[l]), row(hy_f1[l]), hy_w2[l], row(hy_b2[l]), row(hy_f2[l]), hy_w3[l], dec_row)
        o_hy_lat = _hyena_segment(hv, hx1, hx2, 0, seq, FFT_N2_LAT, filt_args, hy_bias[l])
        o_hy_ctx = (_hyena_segment(hv, hx1, hx2, seq, n_ctx, FFT_N2_CTX, filt_args, hy_bias[l])
                    if want_ctx else None)

        lb_row = lb_all[l].reshape(1, W_GROUP)
        hnw = jnp.tile(hg_norm_w[l], N_HEADS).reshape(1, W_GROUP)
        o_hb = _hgrn_direction(p_hg, lb_row, hnw, None, 1, nbl)
        o_hg = _hgrn_direction(p_hg, lb_row, hnw, o_hb, 0, nbl)

        lam_init = 0.8 - 0.6 * math.exp(-0.3 * l)
        qw = jnp.tile(da_q_norm[l], W_GROUP // DA_DK).reshape(1, W_GROUP)
        kw = jnp.tile(da_k_norm[l], W_GROUP // DA_DK).reshape(1, W_GROUP)
        sw = jnp.tile(da_subln[l], N_HEADS).reshape(1, W_GROUP)
        qh, kh, vh = _da_prep(p_da, cos_tab, sin_tab, qw, kw)
        o_da = _da_attention(qh, kh, vh, da_lam[l], sw, lam_init, DA_Q_TILE, seq // DA_Q_TILE, 0, lt, 0,
                             "da_attn_lat")
        o_da_ctx = (_da_attention(qh, kh, vh, da_lam[l], sw, lam_init, TOKEN_BLOCK, 1, nbl, TOKEN_BLOCK, nbl,
                                  "da_attn_ctx") if want_ctx else None)

        n_blocks = nbl + 1 if want_ctx else nbl
        xs_mid = _proj_residual(xs, modtab, w_out[l].astype(BF16),
                                [o_gdn, (o_hy_lat, o_hy_ctx), o_hg, (o_da, o_da_ctx)], 2, n_blocks, nbl, "out_proj")
        (hid,) = _adaln_proj(xs_mid, modtab, ln2_w[l], mlp_w1[l].astype(BF16), (mlp_w1.shape[2],), (BF16,),
                             3, 4, True, nbl, "mlp_up")
        xs = _proj_residual(xs_mid, modtab, mlp_w2[l].astype(BF16), [hid], 5, n_blocks, nbl, "mlp_down")
    return xs
```

```python
import functools
import math

import jax
import jax.numpy as jnp
from jax import lax
from jax.experimental import pallas as pl
from jax.experimental.pallas import tpu as pltpu

F32 = jnp.float32
BF16 = jnp.bfloat16

EPS = 1e-6
TOKEN_BLOCK = 256
CHUNK = 64
HEAD_DIM = 64
N_HEADS = 4
W_GROUP = 256
DA_DK = 32
GRID_W = 64
ROPE_THETA = 10000.0
HY_BANDS = 16
FFT_N2_LAT = 128
FFT_N2_CTX = 16
FFT_K1_GROUP = 8
DA_Q_TILE = 512
GDN_PAD_COLS = 1152
VMEM_LIMIT_BYTES = 56 * 1024 * 1024


def _cparams(n_axes):
    return pltpu.CompilerParams(dimension_semantics=("arbitrary",) * n_axes,
                                vmem_limit_bytes=VMEM_LIMIT_BYTES)


def _sigmoid(x):
    return 1.0 / (1.0 + jnp.exp(-x))


def _silu(x):
    return x * _sigmoid(x)


def _softplus(x):
    return jnp.maximum(x, 0.0) + jnp.log(1.0 + jnp.exp(-jnp.abs(x)))


def _log_sigmoid(x):
    return jnp.minimum(x, 0.0) - jnp.log(1.0 + jnp.exp(-jnp.abs(x)))


def _bdot(a, b):
    return jnp.dot(a.astype(BF16), b.astype(BF16), preferred_element_type=F32)


def _bdot_nt(a, b):
    return lax.dot_general(a.astype(BF16), b.astype(BF16), (((1,), (1,)), ((), ())),
                           preferred_element_type=F32)


def _split3(x):
    hi = x.astype(BF16)
    r = x - hi.astype(F32)
    mid = r.astype(BF16)
    lo = (r - mid.astype(F32)).astype(BF16)
    return hi, mid, lo


def _exact_left(m_bf16, x):
    hi, mid, lo = _split3(x)
    d = lambda p: jnp.dot(m_bf16, p, preferred_element_type=F32)
    return d(hi) + d(mid) + d(lo)


def _exact_right(x, m_bf16):
    hi, mid, lo = _split3(x)
    d = lambda p: jnp.dot(p, m_bf16, preferred_element_type=F32)
    return d(hi) + d(mid) + d(lo)


def _hp_dot(a, b):
    ah = a.astype(BF16)
    al = (a - ah.astype(F32)).astype(BF16)
    bh = b.astype(BF16)
    bl = (b - bh.astype(F32)).astype(BF16)
    d = lambda p, q: jnp.dot(p, q, preferred_element_type=F32)
    return d(ah, bh) + d(ah, bl) + d(al, bh)


def _group_ones(width, group):
    r = lax.broadcasted_iota(jnp.int32, (width, width), 0) // group
    c = lax.broadcasted_iota(jnp.int32, (width, width), 1) // group
    return (r == c).astype(BF16)


def _mod_kernel(s_ref, w_ref, b_ref, o_ref):
    s = _silu(s_ref[...])
    o_ref[0] = _bdot(s, w_ref[0]) + b_ref[0]


def _modulation(cond, mod_w, mod_b):
    depth, d, n = mod_w.shape
    rows = cond.shape[0]
    tn = 1536
    return pl.pallas_call(
        _mod_kernel,
        out_shape=jax.ShapeDtypeStruct((depth, rows, n), F32),
        grid=(depth, n // tn),
        in_specs=[pl.BlockSpec((rows, d), lambda l, j: (0, 0)),
                  pl.BlockSpec((1, d, tn), lambda l, j: (l, 0, j)),
                  pl.BlockSpec((1, 1, tn), lambda l, j: (l, 0, j))],
        out_specs=pl.BlockSpec((1, rows, tn), lambda l, j: (l, 0, j)),
        compiler_params=_cparams(2), name="modulation",
    )(cond, mod_w, mod_b.reshape(depth, 1, n))


def _adaln_proj_kernel(x_ref, mod_ref, lnw_ref, w_ref, *out_refs, shift_row, scale_row, splits, sq_relu):
    x = x_ref[0]
    ms = jnp.mean(x * x, axis=-1, keepdims=True)
    y = x * lax.rsqrt(ms + EPS) * lnw_ref[...]
    mod = mod_ref[0]
    y = y * (1.0 + mod[scale_row:scale_row + 1]) + mod[shift_row:shift_row + 1]
    h = _bdot(y, w_ref[...])
    if sq_relu:
        h = jnp.square(jnp.maximum(h, 0.0))
    off = 0
    for o_ref, n in zip(out_refs, splits):
        o_ref[0] = h[:, off:off + n].astype(o_ref.dtype)
        off += n


def _mod_index(nbl, batch):
    return lambda b, t: (jnp.where(t == nbl, batch, b), 0, 0)


def _adaln_proj(xs, modtab, ln_w, w, splits, out_dtypes, shift_row, scale_row, sq_relu, nbl, name):
    b, lt, d = xs.shape
    nb = lt // TOKEN_BLOCK
    n = w.shape[1]
    kern = functools.partial(_adaln_proj_kernel, shift_row=shift_row, scale_row=scale_row,
                             splits=splits, sq_relu=sq_relu)
    return pl.pallas_call(
        kern,
        out_shape=[jax.ShapeDtypeStruct((b, lt, s), dt) for s, dt in zip(splits, out_dtypes)],
        grid=(b, nb),
        in_specs=[pl.BlockSpec((1, TOKEN_BLOCK, d), lambda i, t: (i, t, 0)),
                  pl.BlockSpec((1, 8, d), _mod_index(nbl, b)),
                  pl.BlockSpec((1, d), lambda i, t: (0, 0)),
                  pl.BlockSpec((d, n), lambda i, t: (0, 0))],
        out_specs=[pl.BlockSpec((1, TOKEN_BLOCK, s), lambda i, t: (i, t, 0)) for s in splits],
        compiler_params=_cparams(2), name=name,
    )(xs, modtab, ln_w.reshape(1, d), w)


def _proj_residual_kernel(x_ref, mod_ref, w_ref, *rest, gate_row, split, nbl):
    a_refs, o_ref = rest[:-1], rest[-1]
    is_ctx = pl.program_id(1) == nbl
    pieces, pos = [], 0
    for two in split:
        lat = a_refs[pos][0].astype(BF16)
        pieces.append(jnp.where(is_ctx, a_refs[pos + 1][0].astype(BF16), lat) if two else lat)
        pos += 2 if two else 1
    a = jnp.concatenate(pieces, axis=1) if len(pieces) > 1 else pieces[0]
    gate = mod_ref[0][gate_row:gate_row + 1]
    o_ref[0] = x_ref[0] + gate * _bdot(a, w_ref[...])


def _proj_residual(xs, modtab, w, acts, gate_row, n_blocks, nbl, name):
    b, lt, d = xs.shape
    split = tuple(isinstance(a, tuple) and a[1] is not None for a in acts)
    specs, args = [], []
    for a, two in zip(acts, split):
        lat = a[0] if isinstance(a, tuple) else a
        width = lat.shape[2]
        if isinstance(a, tuple):
            specs.append(pl.BlockSpec((1, TOKEN_BLOCK, width), lambda i, t: (i, jnp.minimum(t, nbl - 1), 0)))
        else:
            specs.append(pl.BlockSpec((1, TOKEN_BLOCK, width), lambda i, t: (i, t, 0)))
        args.append(lat)
        if two:
            specs.append(pl.BlockSpec((1, TOKEN_BLOCK, width), lambda i, t: (i, 0, 0)))
            args.append(a[1])
    kern = functools.partial(_proj_residual_kernel, gate_row=gate_row, split=split, nbl=nbl)
    return pl.pallas_call(
        kern,
        out_shape=jax.ShapeDtypeStruct((b, n_blocks * TOKEN_BLOCK, d), F32),
        grid=(b, n_blocks),
        in_specs=[pl.BlockSpec((1, TOKEN_BLOCK, d), lambda i, t: (i, t, 0)),
                  pl.BlockSpec((1, 8, d), _mod_index(nbl, b)),
                  pl.BlockSpec(w.shape, lambda i, t: (0, 0))] + specs,
        out_specs=pl.BlockSpec((1, TOKEN_BLOCK, d), lambda i, t: (i, t, 0)),
        compiler_params=_cparams(2), name=name,
    )(xs, modtab, w, *args)


def _scan_block(step, nbl, direction):
    lat = step - 1 if direction == 0 else nbl - step
    return jnp.where(step == 0, nbl, lat)


def _halo_specs(width, nbl, block_of_step, lt):
    rows8 = TOKEN_BLOCK // 8

    def prev_map(b, s):
        return (b, jnp.maximum(block_of_step(s) * rows8 - 1, 0), 0)

    def next_map(b, s):
        return (b, jnp.minimum((block_of_step(s) + 1) * rows8, lt // 8 - 1), 0)

    return [pl.BlockSpec((1, 8, width), prev_map), pl.BlockSpec((1, 8, width), next_map)]


def _short_conv(cur, hp_ref, hn_ref, cw_ref, xp_ref, blk, nbl, width):
    seg_first = jnp.logical_or(blk == 0, blk == nbl)
    seg_last = jnp.logical_or(blk == nbl - 1, blk == nbl)
    xp_ref[0:8, :] = jnp.where(seg_first, 0.0, hp_ref[0][:, :width])
    xp_ref[8:8 + TOKEN_BLOCK, :] = cur
    xp_ref[8 + TOKEN_BLOCK:16 + TOKEN_BLOCK, :] = jnp.where(seg_last, 0.0, hn_ref[0][:, :width])
    cw = cw_ref[...]
    return (cw[0:1] * xp_ref[7:7 + TOKEN_BLOCK, :] + cw[1:2] * xp_ref[8:8 + TOKEN_BLOCK, :]
            + cw[2:3] * xp_ref[9:9 + TOKEN_BLOCK, :])


def _scan_tables(direction, chunk):
    i = jnp.arange(TOKEN_BLOCK, dtype=jnp.int32)[:, None]
    j = jnp.arange(TOKEN_BLOCK, dtype=jnp.int32)[None, :]
    same = (i // chunk) == (j // chunk)
    pos = (lambda t: t % chunk) if direction == 0 else (lambda t: chunk - 1 - t % chunk)
    pi, pj = pos(i), pos(j)
    eye = i == j
    fl = [same & (pj <= pi), same & (pj < pi), eye]
    bl = [same & (pj <= pi), (i // HEAD_DIM) == (j // HEAD_DIM), eye]
    m = 1
    while m < chunk:
        fl.append(same & ((pi // (2 * m)) == (pj // (2 * m))) & ((pi // m) % 2 == 1) & ((pj // m) % 2 == 0))
        boundary = (pi // (2 * m)) * (2 * m) + m - 1
        bl.append((same & (pj <= pi)).astype(F32) - (same & (pj <= boundary)).astype(F32))
        m *= 2
    return jnp.stack([a.astype(F32) for a in fl]), jnp.stack([a.astype(BF16) for a in bl])


def _group_rms(o, gmat, group, weight):
    ms = jnp.dot((o * o).astype(BF16), gmat, preferred_element_type=F32) * (1.0 / group)
    return o * lax.rsqrt(ms + EPS) * weight


def _gdn_kernel(*refs, direction, finish, nbl):
    if finish:
        (p_ref, hp_ref, hn_ref, cw_ref, alog_ref, dtb_ref, nw_ref, bm_ref, other_ref,
         o_ref, s_ref, xp_ref, ob_ref) = refs
    else:
        p_ref, hp_ref, hn_ref, cw_ref, alog_ref, dtb_ref, nw_ref, bm_ref, o_ref, s_ref, xp_ref, ob_ref = refs
    step = pl.program_id(1)
    blk = _scan_block(step, nbl, direction)

    @pl.when(step == 0)
    def _():
        s_ref[...] = jnp.zeros_like(s_ref)

    p = p_ref[0]
    xc = _silu(_short_conv(p[:, :3 * W_GROUP], hp_ref, hn_ref, cw_ref, xp_ref, blk, nbl, 3 * W_GROUP))
    g64 = bm_ref[1]
    q = xc[:, :W_GROUP]
    k = xc[:, W_GROUP:2 * W_GROUP]
    v = xc[:, 2 * W_GROUP:]
    q = q * lax.rsqrt(_bdot(q * q, g64) + EPS) * (HEAD_DIM ** -0.5)
    k = k * lax.rsqrt(_bdot(k * k, g64) + EPS)
    k_t = _bdot_nt(bm_ref[2], k)

    ab = p[:, 4 * W_GROUP:4 * W_GROUP + 128]
    lane = lax.broadcasted_iota(jnp.int32, ab.shape, 1)
    g_all = -jnp.exp(alog_ref[...]) * _softplus(ab + dtb_ref[...])
    w_all = jnp.where(lane < 2 * N_HEADS, g_all, _sigmoid(ab))
    z_col = jnp.where(lane < 2 * N_HEADS, _exact_left(bm_ref[0], w_all), w_all)
    z_row = z_col.T

    heads = range(N_HEADS)
    n_chunks = TOKEN_BLOCK // CHUNK
    lanes = [slice(h * HEAD_DIM, (h + 1) * HEAD_DIM) for h in heads]
    gidx = [N_HEADS * direction + h for h in heads]
    ci_ = lax.broadcasted_iota(jnp.int32, (CHUNK, CHUNK), 0)
    cj_ = lax.broadcasted_iota(jnp.int32, (CHUNK, CHUNK), 1)
    pi_, pj_ = (ci_, cj_) if direction == 0 else (CHUNK - 1 - ci_, CHUNK - 1 - cj_)
    incl = pj_ <= pi_
    strict_f = (pj_ < pi_).astype(F32)
    eye_f = (ci_ == cj_).astype(F32)

    def half_f(m):
        return (((pi_ // (2 * m)) == (pj_ // (2 * m))) & ((pi_ // m) % 2 == 1) & ((pj_ // m) % 2 == 0)).astype(F32)

    chunked = lambda a: a.reshape(n_chunks, CHUNK, a.shape[1])
    bmm = lambda a, b: jnp.einsum("cij,cjk->cik", a.astype(BF16), b.astype(BF16), preferred_element_type=F32)
    bmm_nt = lambda a, b: jnp.einsum("cid,cjd->cij", a.astype(BF16), b.astype(BF16), preferred_element_type=F32)
    gc_c = [chunked(z_col[:, gi:gi + 1]) for gi in gidx]
    gc_r = [jnp.stack([z_row[gi:gi + 1, c * CHUNK:(c + 1) * CHUNK] for c in range(n_chunks)]) for gi in gidx]
    beta = [chunked(z_col[:, 2 * N_HEADS + gi:2 * N_HEADS + gi + 1]) for gi in gidx]
    qc = [chunked(q[:, ls]) for ls in lanes]
    kc = [chunked(k[:, ls]) for ls in lanes]
    vc = [chunked(v[:, ls]) for ls in lanes]
    eg = [jnp.exp(g) for g in gc_c]
    decay = [jnp.exp(jnp.where(incl, gc_c[h] - gc_r[h], -jnp.inf)) for h in heads]
    kk = [bmm_nt(kc[h], kc[h]) for h in heads]
    qk = [bmm_nt(qc[h], kc[h]) for h in heads]
    n = [kk[h] * beta[h] * decay[h] * strict_f for h in heads]
    inv = [eye_f - n[h] * half_f(1) for h in heads]
    m = 2
    while m < CHUNK:
        half = half_f(m)
        x = [bmm(inv[h], n[h] * half) for h in heads]
        inv = [inv[h] - bmm(x[h], inv[h]) for h in heads]
        m *= 2
    uw = [bmm(inv[h], jnp.concatenate([vc[h] * beta[h], kc[h] * (beta[h] * eg[h])], axis=2)) for h in heads]
    attn = [qk[h] * decay[h] for h in heads]
    qg = [qc[h] * eg[h] for h in heads]
    state = [s_ref[h] for h in heads]

    last = CHUNK - 1 if direction == 0 else 0
    for ci in range(n_chunks):
        c = ci if direction == 0 else n_chunks - 1 - ci
        rs = slice(c * CHUNK, (c + 1) * CHUNK)
        g_last = [gc_c[h][c, last:last + 1, :] for h in heads]
        v_new = [uw[h][c, :, :HEAD_DIM] - _bdot(uw[h][c, :, HEAD_DIM:], state[h]) for h in heads]
        for h, ls in enumerate(lanes):
            ob_ref[rs, ls] = _bdot(qg[h][c], state[h]) + _bdot(attn[h][c], v_new[h])
        state = [state[h] * jnp.exp(g_last[h])
                 + _bdot(k_t[ls, rs] * jnp.exp(g_last[h] - gc_r[h][c]), v_new[h]) for h, ls in enumerate(lanes)]
    for h in heads:
        s_ref[h] = state[h]

    if finish:
        o = ob_ref[...] + other_ref[0]
        gate = _silu(p[:, 3 * W_GROUP:4 * W_GROUP])
        o_ref[0] = _group_rms(o, g64, HEAD_DIM, nw_ref[...]) * gate
    else:
        o_ref[0] = ob_ref[...]


def _gdn_direction(p_gdn, conv_w, alog_row, dtb_row, nw_row, other, direction, nbl):
    b, lt, width = p_gdn.shape
    nb = nbl + 1
    blk_of = lambda s: _scan_block(s, nbl, direction)
    finish = other is not None
    bmask = _scan_tables(direction, CHUNK)[1][:3]
    kern = functools.partial(_gdn_kernel, direction=direction, finish=finish, nbl=nbl)
    in_specs = ([pl.BlockSpec((1, TOKEN_BLOCK, width), lambda i, s: (i, blk_of(s), 0))]
                + _halo_specs(3 * W_GROUP, nbl, blk_of, lt)
                + [pl.BlockSpec((3, 3 * W_GROUP), lambda i, s: (0, 0)),
                   pl.BlockSpec((1, 128), lambda i, s: (0, 0)),
                   pl.BlockSpec((1, 128), lambda i, s: (0, 0)),
                   pl.BlockSpec((1, W_GROUP), lambda i, s: (0, 0)),
                   pl.BlockSpec(bmask.shape, lambda i, s: (0, 0, 0))])
    args = [p_gdn, p_gdn, p_gdn, conv_w, alog_row, dtb_row, nw_row, bmask]
    if finish:
        in_specs.append(pl.BlockSpec((1, TOKEN_BLOCK, W_GROUP), lambda i, s: (i, blk_of(s), 0)))
        args.append(other)
    return pl.pallas_call(
        kern,
        out_shape=jax.ShapeDtypeStruct((b, lt, W_GROUP), F32),
        grid=(b, nb),
        in_specs=in_specs,
        out_specs=pl.BlockSpec((1, TOKEN_BLOCK, W_GROUP), lambda i, s: (i, blk_of(s), 0)),
        scratch_shapes=[pltpu.VMEM((N_HEADS, HEAD_DIM, HEAD_DIM), F32),
                        pltpu.VMEM((TOKEN_BLOCK + 16, 3 * W_GROUP), F32),
                        pltpu.VMEM((TOKEN_BLOCK, W_GROUP), F32)],
        compiler_params=_cparams(2), name=f"gdn_dir{direction}",
    )(*args)


def _hgrn_kernel(*refs, direction, finish, nbl):
    if finish:
        p_ref, lb_ref, nw_ref, bm_ref, cum_ref, other_ref, o_ref, s_ref, ob_ref = refs
    else:
        p_ref, lb_ref, nw_ref, bm_ref, cum_ref, o_ref, s_ref, ob_ref = refs
    step = pl.program_id(1)

    @pl.when(step == 0)
    def _():
        s_ref[...] = jnp.zeros_like(s_ref)

    p = p_ref[0]
    lb = lb_ref[...]
    q = _silu(p[:, :W_GROUP])
    v = p[:, W_GROUP:2 * W_GROUP]
    fl = p[:, (2 + direction) * W_GROUP:(3 + direction) * W_GROUP]
    a = jnp.log(lb)
    bb = jnp.log(1.0 - lb) + _log_sigmoid(fl)
    logf = jnp.maximum(a, bb) + jnp.log(1.0 + jnp.exp(-jnp.abs(a - bb)))
    key = (1.0 - lb) * _sigmoid(-fl)
    n_levels = cum_ref.shape[0] // TOKEN_BLOCK - 1
    logf_hi = logf.astype(BF16)
    logf_lo = (logf - logf_hi.astype(F32)).astype(BF16)
    cums = (jnp.dot(cum_ref[...], logf_hi, preferred_element_type=F32)
            + jnp.dot(cum_ref[...], logf_lo, preferred_element_type=F32))
    gcum = cums[:TOKEN_BLOCK]
    lev_e = [jnp.exp(-jnp.abs(cums[(lev + 1) * TOKEN_BLOCK:(lev + 2) * TOKEN_BLOCK])) for lev in range(n_levels)]
    qg = q * jnp.exp(gcum)
    v_t = _bdot_nt(bm_ref[2], v)

    heads = range(N_HEADS)
    n_chunks = TOKEN_BLOCK // CHUNK
    lanes = [slice(h * HEAD_DIM, (h + 1) * HEAD_DIM) for h in heads]
    ci_ = lax.broadcasted_iota(jnp.int32, (CHUNK, CHUNK), 0)
    cj_ = lax.broadcasted_iota(jnp.int32, (CHUNK, CHUNK), 1)
    pi_, pj_ = (ci_, cj_) if direction == 0 else (CHUNK - 1 - ci_, CHUNK - 1 - cj_)
    chunked = lambda a: a.reshape(n_chunks, CHUNK, a.shape[1])
    bmm = lambda a, b: jnp.einsum("cij,cjk->cik", a.astype(BF16), b.astype(BF16), preferred_element_type=F32)
    bmm_nt = lambda a, b: jnp.einsum("cid,cjd->cij", a.astype(BF16), b.astype(BF16), preferred_element_type=F32)
    qc = [chunked(q[:, ls]) for ls in lanes]
    kc = [chunked(key[:, ls]) for ls in lanes]
    amat = [bmm_nt(qc[h], kc[h]) * (ci_ == cj_).astype(F32) for h in heads]
    for lev in range(n_levels):
        m = 1 << lev
        half = (((pi_ // (2 * m)) == (pj_ // (2 * m))) & ((pi_ // m) % 2 == 1) & ((pj_ // m) % 2 == 0)).astype(F32)
        e = [chunked(lev_e[lev][:, ls]) for ls in lanes]
        amat = [amat[h] + half * bmm_nt(qc[h] * e[h], kc[h] * e[h]) for h in heads]
    intra = [bmm(amat[h], chunked(v[:, ls])) for h, ls in enumerate(lanes)]
    state = [s_ref[h] for h in heads]

    last = CHUNK - 1 if direction == 0 else 0
    for ci in range(n_chunks):
        c = ci if direction == 0 else n_chunks - 1 - ci
        rs = slice(c * CHUNK, (c + 1) * CHUNK)
        g_end = gcum[c * CHUNK + last:c * CHUNK + last + 1, :]
        kdec = key[rs, :] * jnp.exp(g_end - gcum[rs, :])
        decay_end = jnp.exp(g_end)
        for h, ls in enumerate(lanes):
            ob_ref[rs, ls] = _bdot_nt(qg[rs, ls], state[h]) + intra[h][c]
        state = [state[h] * decay_end[:, ls] + _bdot(v_t[ls, rs], kdec[:, ls]) for h, ls in enumerate(lanes)]
    for h in heads:
        s_ref[h] = state[h]

    if finish:
        o = ob_ref[...] + other_ref[0]
        gate = _silu(p[:, 4 * W_GROUP:5 * W_GROUP])
        o_ref[0] = _group_rms(o, bm_ref[1], HEAD_DIM, nw_ref[...]) * gate
    else:
        o_ref[0] = ob_ref[...]


def _hgrn_direction(p_hg, lb_row, nw_row, other, direction, nbl):
    b, lt, width = p_hg.shape
    nb = nbl + 1
    blk_of = lambda s: _scan_block(s, nbl, direction)
    finish = other is not None
    tables = _scan_tables(direction, CHUNK)[1]
    bmask = tables[:3]
    cum_tab = jnp.concatenate([tables[0:1], tables[3:]], axis=0).reshape(-1, TOKEN_BLOCK)
    kern = functools.partial(_hgrn_kernel, direction=direction, finish=finish, nbl=nbl)
    in_specs = [pl.BlockSpec((1, TOKEN_BLOCK, width), lambda i, s: (i, blk_of(s), 0)),
                pl.BlockSpec((1, W_GROUP), lambda i, s: (0, 0)),
                pl.BlockSpec((1, W_GROUP), lambda i, s: (0, 0)),
                pl.BlockSpec(bmask.shape, lambda i, s: (0, 0, 0)),
                pl.BlockSpec(cum_tab.shape, lambda i, s: (0, 0))]
    args = [p_hg, lb_row, nw_row, bmask, cum_tab]
    if finish:
        in_specs.append(pl.BlockSpec((1, TOKEN_BLOCK, W_GROUP), lambda i, s: (i, blk_of(s), 0)))
        args.append(other)
    return pl.pallas_call(
        kern,
        out_shape=jax.ShapeDtypeStruct((b, lt, W_GROUP), F32),
        grid=(b, nb),
        in_specs=in_specs,
        out_specs=pl.BlockSpec((1, TOKEN_BLOCK, W_GROUP), lambda i, s: (i, blk_of(s), 0)),
        scratch_shapes=[pltpu.VMEM((N_HEADS, HEAD_DIM, HEAD_DIM), F32),
                        pltpu.VMEM((TOKEN_BLOCK, W_GROUP), F32)],
        compiler_params=_cparams(2), name=f"hgrn_dir{direction}",
    )(*args)


def _da_prep_kernel(p_ref, cos_ref, sin_ref, qw_ref, kw_ref, q_ref, k_ref, v_ref):
    p = p_ref[0]
    g32 = _group_ones(W_GROUP, DA_DK)
    lane = lax.broadcasted_iota(jnp.int32, (TOKEN_BLOCK, W_GROUP), 1)
    first_half = (lane % 16) < 8
    cos, sin = cos_ref[...], sin_ref[...]

    def norm_rope(x, w):
        ms = _bdot(x * x, g32) * (1.0 / DA_DK)
        y = x * lax.rsqrt(ms + EPS) * w
        partner = jnp.where(first_half, pltpu.roll(y, W_GROUP - 8, axis=1), pltpu.roll(y, 8, axis=1))
        return y * cos + partner * sin

    q_ref[0] = (norm_rope(p[:, :W_GROUP], qw_ref[...]) * (DA_DK ** -0.5)).astype(BF16)
    k_ref[0] = norm_rope(p[:, W_GROUP:2 * W_GROUP], kw_ref[...]).astype(BF16)
    dst = lax.broadcasted_iota(jnp.int32, (2 * W_GROUP, W_GROUP), 0)
    src = lax.broadcasted_iota(jnp.int32, (2 * W_GROUP, W_GROUP), 1)
    spread = (dst == (src // HEAD_DIM) * 128 + src % HEAD_DIM).astype(BF16)
    wide_t = _bdot_nt(spread, p[:, 2 * W_GROUP:])
    row_w = lax.broadcasted_iota(jnp.int32, wide_t.shape, 0)
    v_ref[0] = jnp.where(row_w % 128 == HEAD_DIM, 1.0, wide_t).astype(BF16)


def _da_prep(p_da, cos_tab, sin_tab, qw_row, kw_row):
    b, lt, width = p_da.shape
    nb = lt // TOKEN_BLOCK
    blk = lambda w=W_GROUP: pl.BlockSpec((1, TOKEN_BLOCK, w), lambda i, t: (i, t, 0))
    return pl.pallas_call(
        _da_prep_kernel,
        out_shape=[jax.ShapeDtypeStruct((b, lt, W_GROUP), BF16)] * 2
                  + [jax.ShapeDtypeStruct((b, 2 * W_GROUP, lt), BF16)],
        grid=(b, nb),
        in_specs=[pl.BlockSpec((1, TOKEN_BLOCK, width), lambda i, t: (i, t, 0)),
                  pl.BlockSpec((TOKEN_BLOCK, W_GROUP), lambda i, t: (t, 0)),
                  pl.BlockSpec((TOKEN_BLOCK, W_GROUP), lambda i, t: (t, 0)),
                  pl.BlockSpec((1, W_GROUP), lambda i, t: (0, 0)),
                  pl.BlockSpec((1, W_GROUP), lambda i, t: (0, 0))],
        out_specs=[blk(), blk(), pl.BlockSpec((1, 2 * W_GROUP, TOKEN_BLOCK), lambda i, t: (i, 0, t))],
        compiler_params=_cparams(2), name="da_prep",
    )(p_da, cos_tab, sin_tab, qw_row, kw_row)


def _da_attn_kernel(q_ref, k_ref, v_ref, lam_ref, sw_ref, o_ref, *, lam_init):
    q = q_ref[0]
    k = k_ref[0]
    lp = lam_ref[...]
    lam = (jnp.exp(jnp.sum(lp[0:1] * lp[1:2], axis=1, keepdims=True))
           - jnp.exp(jnp.sum(lp[2:3] * lp[3:4], axis=1, keepdims=True)) + lam_init)
    lane = lax.broadcasted_iota(jnp.int32, q.shape, 1)

    def scores(m):
        lo = (m // 2) * HEAD_DIM + (m % 2) * DA_DK
        qm = jnp.where(jnp.logical_and(lane >= lo, lane < lo + DA_DK), q, jnp.zeros_like(q))
        return lax.dot_general(k, qm, (((1,), (1,)), ((), ())), preferred_element_type=F32).astype(BF16)

    n_maps = 2 * N_HEADS
    maps = []
    s_next = scores(0)
    for m in range(n_maps):
        s = s_next
        if m + 1 < n_maps:
            s_next = scores(m + 1)
        e = jnp.exp(s - jnp.max(s, axis=0, keepdims=True))
        vt_h = v_ref[0, (m // 2) * 128:(m // 2 + 1) * 128, :]
        acc = jnp.dot(vt_h, e, preferred_element_type=F32)
        maps.append(acc / acc[HEAD_DIM:HEAD_DIM + 1, :])
    outs = [(maps[2 * h] - lam * maps[2 * h + 1])[:HEAD_DIM] for h in range(N_HEADS)]
    o = jnp.concatenate(outs, axis=0).T
    g64 = _group_ones(W_GROUP, HEAD_DIM)
    o_ref[0] = _group_rms(o, g64, HEAD_DIM, sw_ref[...]) * (1.0 - lam_init)


def _da_attention(qh, kh, vh, lam_p, sw_row, lam_init, q_tile, q_blocks, q_off, k_rows, k_blk, name):
    b, lt, _ = qh.shape
    kern = functools.partial(_da_attn_kernel, lam_init=lam_init)
    return pl.pallas_call(
        kern,
        out_shape=jax.ShapeDtypeStruct((b, q_blocks * q_tile, W_GROUP), F32),
        grid=(b, q_blocks),
        in_specs=[pl.BlockSpec((1, q_tile, W_GROUP), lambda i, t: (i, t + q_off, 0)),
                  pl.BlockSpec((1, k_rows, W_GROUP), lambda i, t: (i, k_blk, 0)),
                  pl.BlockSpec((1, 2 * W_GROUP, k_rows), lambda i, t: (i, 0, k_blk)),
                  pl.BlockSpec((4, DA_DK), lambda i, t: (0, 0)),
                  pl.BlockSpec((1, W_GROUP), lambda i, t: (0, 0))],
        out_specs=pl.BlockSpec((1, q_tile, W_GROUP), lambda i, t: (i, t, 0)),
        compiler_params=_cparams(2), name=name,
    )(qh, kh, vh, lam_p, sw_row)


def _hy_conv_kernel(p_ref, hp_ref, hn_ref, cw_ref, v_ref, x1_ref, x2_ref, xp_ref, *, nbl):
    blk = pl.program_id(1)
    xc = _short_conv(p_ref[0], hp_ref, hn_ref, cw_ref, xp_ref, blk, nbl, 3 * W_GROUP)
    v_ref[0] = xc[:, :W_GROUP].astype(v_ref.dtype)
    x1_ref[0] = xc[:, W_GROUP:2 * W_GROUP].astype(x1_ref.dtype)
    x2_ref[0] = xc[:, 2 * W_GROUP:].astype(x2_ref.dtype)


def _hy_short_conv(p_hy, conv_w, nbl):
    b, lt, width = p_hy.shape
    nb = nbl + 1
    blk = lambda: pl.BlockSpec((1, TOKEN_BLOCK, W_GROUP), lambda i, t: (i, t, 0))
    return pl.pallas_call(
        functools.partial(_hy_conv_kernel, nbl=nbl),
        out_shape=[jax.ShapeDtypeStruct((b, lt, W_GROUP), BF16)] * 3,
        grid=(b, nb),
        in_specs=[pl.BlockSpec((1, TOKEN_BLOCK, width), lambda i, t: (i, t, 0))]
                 + _halo_specs(width, nbl, lambda s: s, lt)
                 + [pl.BlockSpec((3, width), lambda i, t: (0, 0))],
        out_specs=[blk(), blk(), blk()],
        scratch_shapes=[pltpu.VMEM((TOKEN_BLOCK + 16, width), F32)],
        compiler_params=_cparams(2), name="hy_short_conv",
    )(p_hy, p_hy, p_hy, conv_w)


def _hy_filter_kernel(z_ref, w1_ref, b1_ref, f1_ref, w2_ref, b2_ref, f2_ref, w3_ref, dec_ref, o_ref):
    z = z_ref[...]
    h = jnp.sin(f1_ref[...] * (_hp_dot(z, w1_ref[...]) + b1_ref[...]))
    h = jnp.sin(f2_ref[...] * (_hp_dot(h, w2_ref[...]) + b2_ref[...]))
    h = _hp_dot(h, w3_ref[...])
    h = h * jnp.exp(-z[:, 0:1] * jnp.abs(dec_ref[...]))
    first = (pl.program_id(0) * h.shape[0] + lax.broadcasted_iota(jnp.int32, (h.shape[0], 1), 0)) == 0
    for plane in range(4):
        piece = h[:, plane * W_GROUP:(plane + 1) * W_GROUP]
        o_ref[plane] = jnp.where(first, 0.0, piece) if plane >= 2 else piece


def _hy_filters(zfeat, w1p, b1, f1, w2, b2, f2, w3, dec_row):
    n = zfeat.shape[0]
    tn = min(n, 512)
    full = lambda a: pl.BlockSpec(a.shape, lambda i: (0,) * a.ndim)
    args = [w1p, b1, f1, w2, b2, f2, w3, dec_row]
    return pl.pallas_call(
        _hy_filter_kernel,
        out_shape=jax.ShapeDtypeStruct((4, n, W_GROUP), F32),
        grid=(n // tn,),
        in_specs=[pl.BlockSpec((tn, 128), lambda i: (i, 0))] + [full(a) for a in args],
        out_specs=pl.BlockSpec((4, tn, W_GROUP), lambda i: (0, i, 0)),
        compiler_params=_cparams(1), name="hy_filters",
    )(zfeat, *args)


def _left_split_dot(a, v):
    a_hi = a.astype(BF16)
    a_lo = (a - a_hi.astype(F32)).astype(BF16)
    return jnp.dot(a_hi, v, preferred_element_type=F32) + jnp.dot(a_lo, v, preferred_element_type=F32)


def _fft_stage1_kernel(a_ref, u_ref, o_ref):
    u = u_ref[0]
    y = _left_split_dot(a_ref[...], u) if u.dtype == BF16 else _hp_dot(a_ref[...], u)
    o_ref[0] = y.astype(o_ref.dtype)


def _fft_stage1(a_mat, u_view, row_blk, col_tile):
    nbatch, _, cols = u_view.shape
    m, kdim = a_mat.shape
    return pl.pallas_call(
        _fft_stage1_kernel,
        out_shape=jax.ShapeDtypeStruct((nbatch, m, cols), BF16),
        grid=(nbatch, cols // col_tile),
        in_specs=[pl.BlockSpec((m, kdim), lambda i, j: (0, 0)),
                  pl.BlockSpec((1, kdim, col_tile), lambda i, j: (i, row_blk, j))],
        out_specs=pl.BlockSpec((1, m, col_tile), lambda i, j: (i, 0, j)),
        compiler_params=_cparams(2), name="fft_stage1",
    )(a_mat, u_view)


def _fft_mid_kernel(*refs, n2, multiply):
    if multiply:
        y_ref, g_ref, kf_ref, kb_ref, gi_ref, o_ref = refs
    else:
        y_ref, g_ref, o_ref = refs
    group = range(FFT_K1_GROUP)
    z = [_bdot(g_ref[j], y_ref[0, :, j].reshape(2 * n2, W_GROUP)) for j in group]
    if multiply:
        prod = []
        for j in group:
            zr, zi = z[j][:n2], z[j][n2:]
            kr = kf_ref[0, 0, j].astype(F32) + kb_ref[0, 0, j].astype(F32)
            ki = kf_ref[0, 1, j].astype(F32) - kb_ref[0, 1, j].astype(F32)
            prod.append(jnp.concatenate([zr * kr - zi * ki, zr * ki + zi * kr], axis=0))
        z = [_bdot(gi_ref[j], prod[j]) for j in group]
    for j in group:
        o_ref[0, :, j] = z[j].reshape(2, n2, W_GROUP).astype(o_ref.dtype)


def _fft_mid(y1, g_fwd, kf, filt, g_inv, k1p, n2):
    nbatch = y1.shape[0]
    multiply = kf is not None
    data = lambda plane: pl.BlockSpec((1, 2, FFT_K1_GROUP, n2, W_GROUP),
                                      (lambda k, i: (i, 0, k, 0, 0)) if plane is None
                                      else (lambda k, i: (plane, 0, k, 0, 0)))
    mat = lambda: pl.BlockSpec((FFT_K1_GROUP, 2 * n2, 2 * n2), lambda k, i: (k, 0, 0))
    in_specs, args = [data(None), mat()], [y1, g_fwd]
    if multiply:
        in_specs += [data(filt), data(2 + filt), mat()]
        args += [kf, kf, g_inv]
    return pl.pallas_call(
        functools.partial(_fft_mid_kernel, n2=n2, multiply=multiply),
        out_shape=jax.ShapeDtypeStruct((nbatch, 2, k1p, n2, W_GROUP), BF16),
        grid=(k1p // FFT_K1_GROUP, nbatch),
        in_specs=in_specs,
        out_specs=data(None),
        compiler_params=_cparams(2), name="fft_mid_mul" if multiply else "fft_mid",
    )(*args)


def _fft_stage4_kernel(a_ref, v_ref, u_ref, gate_ref, skip_ref, o_ref):
    y = _left_split_dot(a_ref[...], v_ref[0])
    o_ref[0] = (gate_ref[0].astype(F32) * (y + skip_ref[...] * u_ref[0].astype(F32))).astype(o_ref.dtype)


def _fft_stage4(a_mat, v_view, u_view, u_blk, gate_view, gate_blk, skip_row, col_tile):
    nbatch, kdim, cols = v_view.shape
    m = a_mat.shape[0]
    return pl.pallas_call(
        _fft_stage4_kernel,
        out_shape=jax.ShapeDtypeStruct((nbatch, m, cols), BF16),
        grid=(nbatch, cols // col_tile),
        in_specs=[pl.BlockSpec((m, kdim), lambda i, j: (0, 0)),
                  pl.BlockSpec((1, kdim, col_tile), lambda i, j: (i, 0, j)),
                  pl.BlockSpec((1, m, col_tile), lambda i, j: (i, u_blk, j)),
                  pl.BlockSpec((1, m, col_tile), lambda i, j: (i, gate_blk, j)),
                  pl.BlockSpec((1, col_tile), lambda i, j: (0, j))],
        out_specs=pl.BlockSpec((1, m, col_tile), lambda i, j: (i, 0, j)),
        compiler_params=_cparams(2), name="fft_stage4",
    )(a_mat, v_view, u_view, gate_view, skip_row)


def _fft_tables(n, n2):
    big = 2 * n
    n1 = big // n2
    two_pi = 2.0 * math.pi

    def cs(num, den):
        ang = (num % den).astype(F32) * (two_pi / den)
        return jnp.cos(ang), jnp.sin(ang)

    k1p = -(-(n1 // 2 + 1) // FFT_K1_GROUP) * FFT_K1_GROUP
    k1 = jnp.arange(k1p, dtype=jnp.int32)
    live = (k1 <= n1 // 2).astype(F32)
    weight = jnp.where((k1 == 0) | (k1 == n1 // 2), 1.0, 2.0) * live
    m1 = jnp.arange(n1, dtype=jnp.int32)
    c, s = cs(k1[:, None] * m1[None, :], n1)
    a1_full = jnp.concatenate([c * live[:, None], -s * live[:, None]], axis=0)
    k2 = jnp.arange(n2, dtype=jnp.int32)
    num = (k2[None, :, None] * k2[None, None, :] * n1 + k1[:, None, None] * k2[None, None, :]) % big
    c, s = cs(num, big)
    g_fwd = jnp.concatenate([jnp.concatenate([c, s], axis=2), jnp.concatenate([-s, c], axis=2)], axis=1)
    num = (k2[None, :, None] * k2[None, None, :] * n1 + k1[:, None, None] * k2[None, :, None]) % big
    c, s = cs(num, big)
    g_inv = jnp.concatenate([jnp.concatenate([c, -s], axis=2), jnp.concatenate([s, c], axis=2)], axis=1)
    t1 = jnp.arange(n1 // 2, dtype=jnp.int32)
    c, s = cs(t1[:, None] * k1[None, :], n1)
    a4 = jnp.concatenate([c * weight[None, :], -s * weight[None, :]], axis=1) * (1.0 / big)
    return n1, k1p, a1_full, g_fwd.astype(BF16), g_inv.astype(BF16), a4


def _hy_positional(n):
    pos = jnp.arange(n, dtype=F32)
    t01 = pos / max(n - 1, 1)
    bands = jnp.linspace(1e-4, HY_BANDS - 1, HY_BANDS, dtype=F32)
    ang = (2.0 * math.pi / n) * pos[:, None] * bands
    z = jnp.concatenate([t01[:, None], jnp.cos(ang), -jnp.sin(ang)], axis=-1)
    return jnp.pad(z, ((0, 0), (0, 128 - z.shape[1])))


def _hyena_segment(v_all, x1_all, x2_all, seg_start, n, n2, filt_args, skip):
    b, lt, c = v_all.shape
    n1, k1p, a1_full, g_fwd, g_inv, a4 = _fft_tables(n, n2)
    a1_half = a1_full[:, :n1 // 2]
    cols = n2 * c
    col_tile = min(cols, 8192)
    seg_blk = seg_start // n
    view = lambda a: a.reshape(b, lt // n2, cols)

    h = _hy_filters(_hy_positional(n), *filt_args)
    ky = _fft_stage1(a1_half, h.reshape(4, n1 // 2, cols), 0, col_tile)
    kf = _fft_mid(ky.reshape(4, 2, k1p, n2, c), g_fwd, None, 0, None, k1p, n2)

    def long_conv(u_view, u_blk, gate_all, filt):
        y1 = _fft_stage1(a1_half, u_view, u_blk, col_tile)
        vmid = _fft_mid(y1.reshape(b, 2, k1p, n2, c), g_fwd, kf, filt, g_inv, k1p, n2)
        skip_row = jnp.tile(skip[filt], n2).reshape(1, cols)
        return _fft_stage4(a4, vmid.reshape(b, 2 * k1p, cols), u_view, u_blk, view(gate_all), seg_blk,
                           skip_row, col_tile)

    z = long_conv(view(v_all), seg_blk, x1_all, 0)
    out = long_conv(z, 0, x2_all, 1)
    return out.reshape(b, n, c)


def _rope_tables(rows, lt):
    n_freq = DA_DK // 4
    inv = ROPE_THETA ** (-jnp.arange(n_freq, dtype=F32) / n_freq)
    r = jnp.repeat(jnp.arange(rows, dtype=F32), GRID_W)
    col = jnp.tile(jnp.arange(GRID_W, dtype=F32), rows)
    ang_r, ang_c = r[:, None] * inv, col[:, None] * inv
    cos32 = jnp.concatenate([jnp.cos(ang_r)] * 2 + [jnp.cos(ang_c)] * 2, axis=-1)
    sin32 = jnp.concatenate([-jnp.sin(ang_r), jnp.sin(ang_r), -jnp.sin(ang_c), jnp.sin(ang_c)], axis=-1)
    n = rows * GRID_W
    pad = lt - n
    cos = jnp.concatenate([jnp.tile(cos32, (1, W_GROUP // DA_DK)), jnp.ones((pad, W_GROUP), F32)], axis=0)
    sin = jnp.concatenate([jnp.tile(sin32, (1, W_GROUP // DA_DK)), jnp.zeros((pad, W_GROUP), F32)], axis=0)
    return cos, sin


def kernel(x, c, ctx, c_ctx, mod_w, mod_b, ln1_w, ln2_w, w_in, gdn_conv_w, gdn_a_log, gdn_dt_bias, gdn_norm_w, hy_conv_w, hy_w1, hy_b1, hy_f1, hy_w2, hy_b2, hy_f2, hy_w3, hy_decay, hy_bias, hg_lb_raw, hg_norm_w, da_q_norm, da_k_norm, da_lam, da_subln, w_out, mlp_w1, mlp_w2):
    batch, seq, d = x.shape
    n_ctx = ctx.shape[1]
    depth = mod_w.shape[0]
    assert n_ctx == TOKEN_BLOCK and seq % TOKEN_BLOCK == 0 and d == 4 * W_GROUP
    nbl = seq // TOKEN_BLOCK
    lt = seq + n_ctx

    xs = jnp.concatenate([x, ctx], axis=1)
    cond = jnp.concatenate([c, c_ctx[None, :], jnp.zeros((-(batch + 1) % 8, d), F32)], axis=0)
    mods = _modulation(cond, mod_w, mod_b)[:, :batch + 1].reshape(depth, batch + 1, 6, d)
    mods = jnp.pad(mods, ((0, 0), (0, 0), (0, 2), (0, 0)))

    lb_all = jnp.cumsum(jax.nn.softmax(hg_lb_raw.astype(F32), axis=0), axis=0)
    lb_all = lb_all - lb_all[0]
    cos_tab, sin_tab = _rope_tables(seq // GRID_W, lt)

    gdn_cols = 4 * W_GROUP + 4 * N_HEADS
    hy_cols, hg_cols, da_cols = 3 * W_GROUP, 5 * W_GROUP, 3 * W_GROUP
    splits = (GDN_PAD_COLS, hy_cols, hg_cols, da_cols)

    for l in range(depth):
        want_ctx = l < depth - 1
        modtab = mods[l]
        wl = w_in[l]
        w_pad = jnp.concatenate([wl[:, :gdn_cols], jnp.zeros((d, GDN_PAD_COLS - gdn_cols), F32),
                                 wl[:, gdn_cols:]], axis=1).astype(BF16)
        p_gdn, p_hy, p_hg, p_da = _adaln_proj(xs, modtab, ln1_w[l], w_pad, splits, (F32,) * 4, 0, 1, False,
                                              nbl, "in_proj")

        pad16 = lambda a: jnp.pad(a.reshape(1, 2 * N_HEADS), ((0, 0), (0, 128 - 2 * N_HEADS)))
        alog_row, dtb_row = pad16(gdn_a_log[l]), pad16(gdn_dt_bias[l])
        gnw = jnp.tile(gdn_norm_w[l], N_HEADS).reshape(1, W_GROUP)
        o_b = _gdn_direction(p_gdn, gdn_conv_w[l], alog_row, dtb_row, gnw, None, 1, nbl)
        o_gdn = _gdn_direction(p_gdn, gdn_conv_w[l], alog_row, dtb_row, gnw, o_b, 0, nbl)

        hv, hx1, hx2 = _hy_short_conv(p_hy, hy_conv_w[l], nbl)
        w1p = jnp.pad(hy_w1[l], ((0, 128 - hy_w1.shape[1]), (0, 0)))
        row = lambda a: a.reshape(1, -1)
        dec_row = jnp.tile(hy_decay[l].reshape(1, 2 * W_GROUP), (1, 2))
        filt_args = (w1p, row(hy_b1[l]), row(hy_f1[l]), hy_w2[l], row(hy_b2[l]), row(hy_f2[l]), hy_w3[l], dec_row)
        o_hy_lat = _hyena_segment(hv, hx1, hx2, 0, seq, FFT_N2_LAT, filt_args, hy_bias[l])
        o_hy_ctx = (_hyena_segment(hv, hx1, hx2, seq, n_ctx, FFT_N2_CTX, filt_args, hy_bias[l])
                    if want_ctx else None)

        lb_row = lb_all[l].reshape(1, W_GROUP)
        hnw = jnp.tile(hg_norm_w[l], N_HEADS).reshape(1, W_GROUP)
        o_hb = _hgrn_direction(p_hg, lb_row, hnw, None, 1, nbl)
        o_hg = _hgrn_direction(p_hg, lb_row, hnw, o_hb, 0, nbl)

        lam_init = 0.8 - 0.6 * math.exp(-0.3 * l)
        qw = jnp.tile(da_q_norm[l], W_GROUP // DA_DK).reshape(1, W_GROUP)
        kw = jnp.tile(da_k_norm[l], W_GROUP // DA_DK).reshape(1, W_GROUP)
        sw = jnp.tile(da_subln[l], N_HEADS).reshape(1, W_GROUP)
        qh, kh, vh = _da_prep(p_da, cos_tab, sin_tab, qw, kw)
        o_da = _da_attention(qh, kh, vh, da_lam[l], sw, lam_init, DA_Q_TILE, seq // DA_Q_TILE, 0, lt, 0,
                             "da_attn_lat")
        o_da_ctx = (_da_attention(qh, kh, vh, da_lam[l], sw, lam_init, TOKEN_BLOCK, 1, nbl, TOKEN_BLOCK, nbl,
                                  "da_attn_ctx") if want_ctx else None)

        n_blocks = nbl + 1 if want_ctx else nbl
        xs_mid = _proj_residual(xs, modtab, w_out[l].astype(BF16),
                                [o_gdn, (o_hy_lat, o_hy_ctx), o_hg, (o_da, o_da_ctx)], 2, n_blocks, nbl, "out_proj")
        (hid,) = _adaln_proj(xs_mid, modtab, ln2_w[l], mlp_w1[l].astype(BF16), (mlp_w1.shape[2],), (BF16,),
                             3, 4, True, nbl, "mlp_up")
        xs = _proj_residual(xs_mid, modtab, mlp_w2[l].astype(BF16), [hid], 5, n_blocks, nbl, "mlp_down")
    return xs
```

```python
import functools
import math

import jax
import jax.numpy as jnp
from jax import lax
from jax.experimental import pallas as pl
from jax.experimental.pallas import tpu as pltpu

F32 = jnp.float32
BF16 = jnp.bfloat16

EPS = 1e-6
TOKEN_BLOCK = 256
CHUNK = 64
HEAD_DIM = 64
N_HEADS = 4
W_GROUP = 256
DA_DK = 32
GRID_W = 64
ROPE_THETA = 10000.0
HY_BANDS = 16
FFT_N2_LAT = 128
FFT_N2_CTX = 16
FFT_K1_GROUP = 8
DA_Q_TILE = 512
SCAN_ROWS = 2
GDN_PAD_COLS = 1152
VMEM_LIMIT_BYTES = 56 * 1024 * 1024


def _cparams(n_axes):
    return pltpu.CompilerParams(dimension_semantics=("arbitrary",) * n_axes,
                                vmem_limit_bytes=VMEM_LIMIT_BYTES)


def _sigmoid(x):
    return 1.0 / (1.0 + jnp.exp(-x))


def _silu(x):
    return x * _sigmoid(x)


def _softplus(x):
    return jnp.maximum(x, 0.0) + jnp.log(1.0 + jnp.exp(-jnp.abs(x)))


def _log_sigmoid(x):
    return jnp.minimum(x, 0.0) - jnp.log(1.0 + jnp.exp(-jnp.abs(x)))


def _bdot(a, b):
    return jnp.dot(a.astype(BF16), b.astype(BF16), preferred_element_type=F32)


def _bdot_nt(a, b):
    return lax.dot_general(a.astype(BF16), b.astype(BF16), (((1,), (1,)), ((), ())),
                           preferred_element_type=F32)


def _split3(x):
    hi = x.astype(BF16)
    r = x - hi.astype(F32)
    mid = r.astype(BF16)
    lo = (r - mid.astype(F32)).astype(BF16)
    return hi, mid, lo


def _exact_left(m_bf16, x):
    hi, mid, lo = _split3(x)
    d = lambda p: jnp.dot(m_bf16, p, preferred_element_type=F32)
    return d(hi) + d(mid) + d(lo)


def _exact_right(x, m_bf16):
    hi, mid, lo = _split3(x)
    d = lambda p: jnp.dot(p, m_bf16, preferred_element_type=F32)
    return d(hi) + d(mid) + d(lo)


def _hp_dot(a, b):
    ah = a.astype(BF16)
    al = (a - ah.astype(F32)).astype(BF16)
    bh = b.astype(BF16)
    bl = (b - bh.astype(F32)).astype(BF16)
    d = lambda p, q: jnp.dot(p, q, preferred_element_type=F32)
    return d(ah, bh) + d(ah, bl) + d(al, bh)


def _group_ones(width, group):
    r = lax.broadcasted_iota(jnp.int32, (width, width), 0) // group
    c = lax.broadcasted_iota(jnp.int32, (width, width), 1) // group
    return (r == c).astype(BF16)


def _mod_kernel(s_ref, w_ref, b_ref, o_ref):
    s = _silu(s_ref[...])
    o_ref[0] = _bdot(s, w_ref[0]) + b_ref[0]


def _modulation(cond, mod_w, mod_b):
    depth, d, n = mod_w.shape
    rows = cond.shape[0]
    tn = 1536
    return pl.pallas_call(
        _mod_kernel,
        out_shape=jax.ShapeDtypeStruct((depth, rows, n), F32),
        grid=(depth, n // tn),
        in_specs=[pl.BlockSpec((rows, d), lambda l, j: (0, 0)),
                  pl.BlockSpec((1, d, tn), lambda l, j: (l, 0, j)),
                  pl.BlockSpec((1, 1, tn), lambda l, j: (l, 0, j))],
        out_specs=pl.BlockSpec((1, rows, tn), lambda l, j: (l, 0, j)),
        compiler_params=_cparams(2), name="modulation",
    )(cond, mod_w, mod_b.reshape(depth, 1, n))


def _adaln_proj_kernel(x_ref, mod_ref, lnw_ref, w_ref, *out_refs, shift_row, scale_row, splits, sq_relu):
    x = x_ref[0]
    ms = jnp.mean(x * x, axis=-1, keepdims=True)
    y = x * lax.rsqrt(ms + EPS) * lnw_ref[...]
    mod = mod_ref[0]
    y = y * (1.0 + mod[scale_row:scale_row + 1]) + mod[shift_row:shift_row + 1]
    h = _bdot(y, w_ref[...])
    if sq_relu:
        h = jnp.square(jnp.maximum(h, 0.0))
    off = 0
    for o_ref, n in zip(out_refs, splits):
        o_ref[0] = h[:, off:off + n].astype(o_ref.dtype)
        off += n


def _mod_index(nbl, batch):
    return lambda b, t: (jnp.where(t == nbl, batch, b), 0, 0)


def _adaln_proj(xs, modtab, ln_w, w, splits, out_dtypes, shift_row, scale_row, sq_relu, nbl, name):
    b, lt, d = xs.shape
    nb = lt // TOKEN_BLOCK
    n = w.shape[1]
    kern = functools.partial(_adaln_proj_kernel, shift_row=shift_row, scale_row=scale_row,
                             splits=splits, sq_relu=sq_relu)
    return pl.pallas_call(
        kern,
        out_shape=[jax.ShapeDtypeStruct((b, lt, s), dt) for s, dt in zip(splits, out_dtypes)],
        grid=(b, nb),
        in_specs=[pl.BlockSpec((1, TOKEN_BLOCK, d), lambda i, t: (i, t, 0)),
                  pl.BlockSpec((1, 8, d), _mod_index(nbl, b)),
                  pl.BlockSpec((1, d), lambda i, t: (0, 0)),
                  pl.BlockSpec((d, n), lambda i, t: (0, 0))],
        out_specs=[pl.BlockSpec((1, TOKEN_BLOCK, s), lambda i, t: (i, t, 0)) for s in splits],
        compiler_params=_cparams(2), name=name,
    )(xs, modtab, ln_w.reshape(1, d), w)


def _proj_residual_kernel(x_ref, mod_ref, w_ref, *rest, gate_row, split, nbl):
    a_refs, o_ref = rest[:-1], rest[-1]
    is_ctx = pl.program_id(1) == nbl
    pieces, pos = [], 0
    for two in split:
        lat = a_refs[pos][0].astype(BF16)
        pieces.append(jnp.where(is_ctx, a_refs[pos + 1][0].astype(BF16), lat) if two else lat)
        pos += 2 if two else 1
    a = jnp.concatenate(pieces, axis=1) if len(pieces) > 1 else pieces[0]
    gate = mod_ref[0][gate_row:gate_row + 1]
    o_ref[0] = x_ref[0] + gate * _bdot(a, w_ref[...])


def _proj_residual(xs, modtab, w, acts, gate_row, n_blocks, nbl, name):
    b, lt, d = xs.shape
    split = tuple(isinstance(a, tuple) and a[1] is not None for a in acts)
    specs, args = [], []
    for a, two in zip(acts, split):
        lat = a[0] if isinstance(a, tuple) else a
        width = lat.shape[2]
        if isinstance(a, tuple):
            specs.append(pl.BlockSpec((1, TOKEN_BLOCK, width), lambda i, t: (i, jnp.minimum(t, nbl - 1), 0)))
        else:
            specs.append(pl.BlockSpec((1, TOKEN_BLOCK, width), lambda i, t: (i, t, 0)))
        args.append(lat)
        if two:
            specs.append(pl.BlockSpec((1, TOKEN_BLOCK, width), lambda i, t: (i, 0, 0)))
            args.append(a[1])
    kern = functools.partial(_proj_residual_kernel, gate_row=gate_row, split=split, nbl=nbl)
    return pl.pallas_call(
        kern,
        out_shape=jax.ShapeDtypeStruct((b, n_blocks * TOKEN_BLOCK, d), F32),
        grid=(b, n_blocks),
        in_specs=[pl.BlockSpec((1, TOKEN_BLOCK, d), lambda i, t: (i, t, 0)),
                  pl.BlockSpec((1, 8, d), _mod_index(nbl, b)),
                  pl.BlockSpec(w.shape, lambda i, t: (0, 0))] + specs,
        out_specs=pl.BlockSpec((1, TOKEN_BLOCK, d), lambda i, t: (i, t, 0)),
        compiler_params=_cparams(2), name=name,
    )(xs, modtab, w, *args)


def _scan_block(step, nbl, direction):
    lat = step - 1 if direction == 0 else nbl - step
    return jnp.where(step == 0, nbl, lat)


def _halo_specs(width, nbl, block_of_step, lt, batch_rows=1):
    rows8 = TOKEN_BLOCK // 8

    def prev_map(b, s):
        return (b, jnp.maximum(block_of_step(s) * rows8 - 1, 0), 0)

    def next_map(b, s):
        return (b, jnp.minimum((block_of_step(s) + 1) * rows8, lt // 8 - 1), 0)

    return [pl.BlockSpec((batch_rows, 8, width), prev_map), pl.BlockSpec((batch_rows, 8, width), next_map)]


def _short_conv(cur, hp_ref, hn_ref, cw_ref, xp_ref, blk, nbl, width):
    seg_first = jnp.logical_or(blk == 0, blk == nbl)
    seg_last = jnp.logical_or(blk == nbl - 1, blk == nbl)
    xp_ref[0:8, :] = jnp.where(seg_first, 0.0, hp_ref[:, :width])
    xp_ref[8:8 + TOKEN_BLOCK, :] = cur
    xp_ref[8 + TOKEN_BLOCK:16 + TOKEN_BLOCK, :] = jnp.where(seg_last, 0.0, hn_ref[:, :width])
    cw = cw_ref[...]
    return (cw[0:1] * xp_ref[7:7 + TOKEN_BLOCK, :] + cw[1:2] * xp_ref[8:8 + TOKEN_BLOCK, :]
            + cw[2:3] * xp_ref[9:9 + TOKEN_BLOCK, :])


def _scan_tables(direction, chunk):
    i = jnp.arange(TOKEN_BLOCK, dtype=jnp.int32)[:, None]
    j = jnp.arange(TOKEN_BLOCK, dtype=jnp.int32)[None, :]
    same = (i // chunk) == (j // chunk)
    pos = (lambda t: t % chunk) if direction == 0 else (lambda t: chunk - 1 - t % chunk)
    pi, pj = pos(i), pos(j)
    eye = i == j
    fl = [same & (pj <= pi), same & (pj < pi), eye]
    bl = [same & (pj <= pi), (i // HEAD_DIM) == (j // HEAD_DIM), eye]
    m = 1
    while m < chunk:
        fl.append(same & ((pi // (2 * m)) == (pj // (2 * m))) & ((pi // m) % 2 == 1) & ((pj // m) % 2 == 0))
        boundary = (pi // (2 * m)) * (2 * m) + m - 1
        bl.append((same & (pj <= pi)).astype(F32) - (same & (pj <= boundary)).astype(F32))
        m *= 2
    return jnp.stack([a.astype(F32) for a in fl]), jnp.stack([a.astype(BF16) for a in bl])


def _group_rms(o, gmat, group, weight):
    ms = jnp.dot((o * o).astype(BF16), gmat, preferred_element_type=F32) * (1.0 / group)
    return o * lax.rsqrt(ms + EPS) * weight


def _gdn_kernel(*refs, direction, finish, nbl):
    if finish:
        (p_ref, hp_ref, hn_ref, cw_ref, alog_ref, dtb_ref, nw_ref, bm_ref, other_ref,
         o_ref, s_ref, xp_ref, ob_ref) = refs
    else:
        p_ref, hp_ref, hn_ref, cw_ref, alog_ref, dtb_ref, nw_ref, bm_ref, o_ref, s_ref, xp_ref, ob_ref = refs
    step = pl.program_id(1)
    blk = _scan_block(step, nbl, direction)

    @pl.when(step == 0)
    def _():
        s_ref[...] = jnp.zeros_like(s_ref)

    g64 = bm_ref[1]
    rows = range(SCAN_ROWS)
    p_rows, q_rows, k_rows, v_rows, kt_rows, zc_rows, zr_rows = [], [], [], [], [], [], []
    for r in rows:
        p = p_ref[r]
        xc = _silu(_short_conv(p[:, :3 * W_GROUP], hp_ref.at[r], hn_ref.at[r], cw_ref, xp_ref.at[r], blk, nbl,
                               3 * W_GROUP))
        q = xc[:, :W_GROUP]
        k = xc[:, W_GROUP:2 * W_GROUP]
        q = q * lax.rsqrt(_bdot(q * q, g64) + EPS) * (HEAD_DIM ** -0.5)
        k = k * lax.rsqrt(_bdot(k * k, g64) + EPS)
        ab = p[:, 4 * W_GROUP:4 * W_GROUP + 128]
        lane = lax.broadcasted_iota(jnp.int32, ab.shape, 1)
        g_all = -jnp.exp(alog_ref[...]) * _softplus(ab + dtb_ref[...])
        w_all = jnp.where(lane < 2 * N_HEADS, g_all, _sigmoid(ab))
        z_col = jnp.where(lane < 2 * N_HEADS, _exact_left(bm_ref[0], w_all), w_all)
        p_rows.append(p)
        q_rows.append(q)
        k_rows.append(k)
        v_rows.append(xc[:, 2 * W_GROUP:])
        kt_rows.append(_bdot_nt(bm_ref[2], k))
        zc_rows.append(z_col)
        zr_rows.append(z_col.T)

    n_chunks = TOKEN_BLOCK // CHUNK
    problems = [(r, h) for r in rows for h in range(N_HEADS)]
    heads = range(len(problems))
    lanes = [slice(h * HEAD_DIM, (h + 1) * HEAD_DIM) for _, h in problems]
    gidx = [N_HEADS * direction + h for _, h in problems]
    q = [q_rows[r] for r, _ in problems]
    k = [k_rows[r] for r, _ in problems]
    v = [v_rows[r] for r, _ in problems]
    k_t = [kt_rows[r] for r, _ in problems]
    z_col = [zc_rows[r] for r, _ in problems]
    z_row = [zr_rows[r] for r, _ in problems]
    ci_ = lax.broadcasted_iota(jnp.int32, (CHUNK, CHUNK), 0)
    cj_ = lax.broadcasted_iota(jnp.int32, (CHUNK, CHUNK), 1)
    pi_, pj_ = (ci_, cj_) if direction == 0 else (CHUNK - 1 - ci_, CHUNK - 1 - cj_)
    incl = pj_ <= pi_
    strict_f = (pj_ < pi_).astype(F32)
    eye_f = (ci_ == cj_).astype(F32)

    def half_f(m):
        return (((pi_ // (2 * m)) == (pj_ // (2 * m))) & ((pi_ // m) % 2 == 1) & ((pj_ // m) % 2 == 0)).astype(F32)

    chunked = lambda a: a.reshape(n_chunks, CHUNK, a.shape[1])
    bmm = lambda a, b: jnp.einsum("cij,cjk->cik", a.astype(BF16), b.astype(BF16), preferred_element_type=F32)
    bmm_nt = lambda a, b: jnp.einsum("cid,cjd->cij", a.astype(BF16), b.astype(BF16), preferred_element_type=F32)
    gc_c = [chunked(z_col[i][:, gi:gi + 1]) for i, gi in enumerate(gidx)]
    gc_r = [jnp.stack([z_row[i][gi:gi + 1, c * CHUNK:(c + 1) * CHUNK] for c in range(n_chunks)])
            for i, gi in enumerate(gidx)]
    beta = [chunked(z_col[i][:, 2 * N_HEADS + gi:2 * N_HEADS + gi + 1]) for i, gi in enumerate(gidx)]
    qc = [chunked(q[i][:, ls]) for i, ls in enumerate(lanes)]
    kc = [chunked(k[i][:, ls]) for i, ls in enumerate(lanes)]
    vc = [chunked(v[i][:, ls]) for i, ls in enumerate(lanes)]
    eg = [jnp.exp(g) for g in gc_c]
    decay = [jnp.exp(jnp.where(incl, gc_c[h] - gc_r[h], -jnp.inf)) for h in heads]
    kk = [bmm_nt(kc[h], kc[h]) for h in heads]
    qk = [bmm_nt(qc[h], kc[h]) for h in heads]
    n = [kk[h] * beta[h] * decay[h] * strict_f for h in heads]
    inv = [eye_f - n[h] * half_f(1) for h in heads]
    m = 2
    while m < CHUNK:
        half = half_f(m)
        x = [bmm(inv[h], n[h] * half) for h in heads]
        inv = [inv[h] - bmm(x[h], inv[h]) for h in heads]
        m *= 2
    uw = [bmm(inv[h], jnp.concatenate([vc[h] * beta[h], kc[h] * (beta[h] * eg[h])], axis=2)) for h in heads]
    attn = [qk[h] * decay[h] for h in heads]
    qg = [qc[h] * eg[h] for h in heads]
    state = [s_ref[h] for h in heads]

    last = CHUNK - 1 if direction == 0 else 0
    for ci in range(n_chunks):
        c = ci if direction == 0 else n_chunks - 1 - ci
        rs = slice(c * CHUNK, (c + 1) * CHUNK)
        g_last = [gc_c[h][c, last:last + 1, :] for h in heads]
        v_new = [uw[h][c, :, :HEAD_DIM] - _bdot(uw[h][c, :, HEAD_DIM:], state[h]) for h in heads]
        for h, ls in enumerate(lanes):
            ob_ref[problems[h][0], rs, ls] = _bdot(qg[h][c], state[h]) + _bdot(attn[h][c], v_new[h])
        state = [state[h] * jnp.exp(g_last[h])
                 + _bdot(k_t[h][ls, rs] * jnp.exp(g_last[h] - gc_r[h][c]), v_new[h]) for h, ls in enumerate(lanes)]
    for h in heads:
        s_ref[h] = state[h]

    for r in rows:
        if finish:
            o = ob_ref[r] + other_ref[r]
            gate = _silu(p_rows[r][:, 3 * W_GROUP:4 * W_GROUP])
            o_ref[r] = _group_rms(o, g64, HEAD_DIM, nw_ref[...]) * gate
        else:
            o_ref[r] = ob_ref[r]


def _gdn_direction(p_gdn, conv_w, alog_row, dtb_row, nw_row, other, direction, nbl):
    b, lt, width = p_gdn.shape
    nb = nbl + 1
    blk_of = lambda s: _scan_block(s, nbl, direction)
    finish = other is not None
    bmask = _scan_tables(direction, CHUNK)[1][:3]
    assert b % SCAN_ROWS == 0
    kern = functools.partial(_gdn_kernel, direction=direction, finish=finish, nbl=nbl)
    in_specs = ([pl.BlockSpec((SCAN_ROWS, TOKEN_BLOCK, width), lambda i, s: (i, blk_of(s), 0))]
                + _halo_specs(3 * W_GROUP, nbl, blk_of, lt, SCAN_ROWS)
                + [pl.BlockSpec((3, 3 * W_GROUP), lambda i, s: (0, 0)),
                   pl.BlockSpec((1, 128), lambda i, s: (0, 0)),
                   pl.BlockSpec((1, 128), lambda i, s: (0, 0)),
                   pl.BlockSpec((1, W_GROUP), lambda i, s: (0, 0)),
                   pl.BlockSpec(bmask.shape, lambda i, s: (0, 0, 0))])
    args = [p_gdn, p_gdn, p_gdn, conv_w, alog_row, dtb_row, nw_row, bmask]
    if finish:
        in_specs.append(pl.BlockSpec((SCAN_ROWS, TOKEN_BLOCK, W_GROUP), lambda i, s: (i, blk_of(s), 0)))
        args.append(other)
    return pl.pallas_call(
        kern,
        out_shape=jax.ShapeDtypeStruct((b, lt, W_GROUP), F32),
        grid=(b // SCAN_ROWS, nb),
        in_specs=in_specs,
        out_specs=pl.BlockSpec((SCAN_ROWS, TOKEN_BLOCK, W_GROUP), lambda i, s: (i, blk_of(s), 0)),
        scratch_shapes=[pltpu.VMEM((SCAN_ROWS * N_HEADS, HEAD_DIM, HEAD_DIM), F32),
                        pltpu.VMEM((SCAN_ROWS, TOKEN_BLOCK + 16, 3 * W_GROUP), F32),
                        pltpu.VMEM((SCAN_ROWS, TOKEN_BLOCK, W_GROUP), F32)],
        compiler_params=_cparams(2), name=f"gdn_dir{direction}",
    )(*args)


def _hgrn_kernel(*refs, direction, finish, nbl):
    if finish:
        p_ref, lb_ref, nw_ref, bm_ref, cum_ref, other_ref, o_ref, s_ref, ob_ref = refs
    else:
        p_ref, lb_ref, nw_ref, bm_ref, cum_ref, o_ref, s_ref, ob_ref = refs
    step = pl.program_id(1)

    @pl.when(step == 0)
    def _():
        s_ref[...] = jnp.zeros_like(s_ref)

    lb = lb_ref[...]
    n_levels = cum_ref.shape[0] // TOKEN_BLOCK - 1
    rows = range(SCAN_ROWS)
    p_rows, q_rows, key_rows, v_rows, vt_rows, gcum_rows, qg_rows, lev_rows = [], [], [], [], [], [], [], []
    for r in rows:
        p = p_ref[r]
        q = _silu(p[:, :W_GROUP])
        v = p[:, W_GROUP:2 * W_GROUP]
        fl = p[:, (2 + direction) * W_GROUP:(3 + direction) * W_GROUP]
        a = jnp.log(lb)
        bb = jnp.log(1.0 - lb) + _log_sigmoid(fl)
        logf = jnp.maximum(a, bb) + jnp.log(1.0 + jnp.exp(-jnp.abs(a - bb)))
        logf_hi = logf.astype(BF16)
        logf_lo = (logf - logf_hi.astype(F32)).astype(BF16)
        cums = (jnp.dot(cum_ref[...], logf_hi, preferred_element_type=F32)
                + jnp.dot(cum_ref[...], logf_lo, preferred_element_type=F32))
        gcum = cums[:TOKEN_BLOCK]
        p_rows.append(p)
        q_rows.append(q)
        key_rows.append((1.0 - lb) * _sigmoid(-fl))
        v_rows.append(v)
        vt_rows.append(_bdot_nt(bm_ref[2], v))
        gcum_rows.append(gcum)
        qg_rows.append(q * jnp.exp(gcum))
        lev_rows.append([jnp.exp(-jnp.abs(cums[(lev + 1) * TOKEN_BLOCK:(lev + 2) * TOKEN_BLOCK]))
                         for lev in range(n_levels)])

    n_chunks = TOKEN_BLOCK // CHUNK
    problems = [(r, h) for r in rows for h in range(N_HEADS)]
    heads = range(len(problems))
    lanes = [slice(h * HEAD_DIM, (h + 1) * HEAD_DIM) for _, h in problems]
    ci_ = lax.broadcasted_iota(jnp.int32, (CHUNK, CHUNK), 0)
    cj_ = lax.broadcasted_iota(jnp.int32, (CHUNK, CHUNK), 1)
    pi_, pj_ = (ci_, cj_) if direction == 0 else (CHUNK - 1 - ci_, CHUNK - 1 - cj_)
    chunked = lambda a: a.reshape(n_chunks, CHUNK, a.shape[1])
    bmm = lambda a, b: jnp.einsum("cij,cjk->cik", a.astype(BF16), b.astype(BF16), preferred_element_type=F32)
    bmm_nt = lambda a, b: jnp.einsum("cid,cjd->cij", a.astype(BF16), b.astype(BF16), preferred_element_type=F32)
    qc = [chunked(q_rows[r][:, ls]) for (r, _), ls in zip(problems, lanes)]
    kc = [chunked(key_rows[r][:, ls]) for (r, _), ls in zip(problems, lanes)]
    amat = [bmm_nt(qc[h], kc[h]) * (ci_ == cj_).astype(F32) for h in heads]
    for lev in range(n_levels):
        m = 1 << lev
        half = (((pi_ // (2 * m)) == (pj_ // (2 * m))) & ((pi_ // m) % 2 == 1) & ((pj_ // m) % 2 == 0)).astype(F32)
        e = [chunked(lev_rows[r][lev][:, ls]) for (r, _), ls in zip(problems, lanes)]
        amat = [amat[h] + half * bmm_nt(qc[h] * e[h], kc[h] * e[h]) for h in heads]
    intra = [bmm(amat[h], chunked(v_rows[r][:, ls])) for h, ((r, _), ls) in enumerate(zip(problems, lanes))]
    state = [s_ref[h] for h in heads]

    last = CHUNK - 1 if direction == 0 else 0
    for ci in range(n_chunks):
        c = ci if direction == 0 else n_chunks - 1 - ci
        rs = slice(c * CHUNK, (c + 1) * CHUNK)
        g_end = [gcum_rows[r][c * CHUNK + last:c * CHUNK + last + 1, :] for r in rows]
        kdec = [key_rows[r][rs, :] * jnp.exp(g_end[r] - gcum_rows[r][rs, :]) for r in rows]
        decay_end = [jnp.exp(g_end[r]) for r in rows]
        for h, ((r, _), ls) in enumerate(zip(problems, lanes)):
            ob_ref[r, rs, ls] = _bdot_nt(qg_rows[r][rs, ls], state[h]) + intra[h][c]
        state = [state[h] * decay_end[r][:, ls] + _bdot(vt_rows[r][ls, rs], kdec[r][:, ls])
                 for h, ((r, _), ls) in enumerate(zip(problems, lanes))]
    for h in heads:
        s_ref[h] = state[h]

    for r in rows:
        if finish:
            o = ob_ref[r] + other_ref[r]
            gate = _silu(p_rows[r][:, 4 * W_GROUP:5 * W_GROUP])
            o_ref[r] = _group_rms(o, bm_ref[1], HEAD_DIM, nw_ref[...]) * gate
        else:
            o_ref[r] = ob_ref[r]


def _hgrn_direction(p_hg, lb_row, nw_row, other, direction, nbl):
    b, lt, width = p_hg.shape
    nb = nbl + 1
    blk_of = lambda s: _scan_block(s, nbl, direction)
    finish = other is not None
    tables = _scan_tables(direction, CHUNK)[1]
    bmask = tables[:3]
    cum_tab = jnp.concatenate([tables[0:1], tables[3:]], axis=0).reshape(-1, TOKEN_BLOCK)
    assert b % SCAN_ROWS == 0
    kern = functools.partial(_hgrn_kernel, direction=direction, finish=finish, nbl=nbl)
    in_specs = [pl.BlockSpec((SCAN_ROWS, TOKEN_BLOCK, width), lambda i, s: (i, blk_of(s), 0)),
                pl.BlockSpec((1, W_GROUP), lambda i, s: (0, 0)),
                pl.BlockSpec((1, W_GROUP), lambda i, s: (0, 0)),
                pl.BlockSpec(bmask.shape, lambda i, s: (0, 0, 0)),
                pl.BlockSpec(cum_tab.shape, lambda i, s: (0, 0))]
    args = [p_hg, lb_row, nw_row, bmask, cum_tab]
    if finish:
        in_specs.append(pl.BlockSpec((SCAN_ROWS, TOKEN_BLOCK, W_GROUP), lambda i, s: (i, blk_of(s), 0)))
        args.append(other)
    return pl.pallas_call(
        kern,
        out_shape=jax.ShapeDtypeStruct((b, lt, W_GROUP), F32),
        grid=(b // SCAN_ROWS, nb),
        in_specs=in_specs,
        out_specs=pl.BlockSpec((SCAN_ROWS, TOKEN_BLOCK, W_GROUP), lambda i, s: (i, blk_of(s), 0)),
        scratch_shapes=[pltpu.VMEM((SCAN_ROWS * N_HEADS, HEAD_DIM, HEAD_DIM), F32),
                        pltpu.VMEM((SCAN_ROWS, TOKEN_BLOCK, W_GROUP), F32)],
        compiler_params=_cparams(2), name=f"hgrn_dir{direction}",
    )(*args)


def _da_prep_kernel(p_ref, cos_ref, sin_ref, qw_ref, kw_ref, q_ref, k_ref, v_ref):
    p = p_ref[0]
    g32 = _group_ones(W_GROUP, DA_DK)
    lane = lax.broadcasted_iota(jnp.int32, (TOKEN_BLOCK, W_GROUP), 1)
    first_half = (lane % 16) < 8
    cos, sin = cos_ref[...], sin_ref[...]

    def norm_rope(x, w):
        ms = _bdot(x * x, g32) * (1.0 / DA_DK)
        y = x * lax.rsqrt(ms + EPS) * w
        partner = jnp.where(first_half, pltpu.roll(y, W_GROUP - 8, axis=1), pltpu.roll(y, 8, axis=1))
        return y * cos + partner * sin

    q_ref[0] = (norm_rope(p[:, :W_GROUP], qw_ref[...]) * (DA_DK ** -0.5)).astype(BF16)
    k_ref[0] = norm_rope(p[:, W_GROUP:2 * W_GROUP], kw_ref[...]).astype(BF16)
    dst = lax.broadcasted_iota(jnp.int32, (2 * W_GROUP, W_GROUP), 0)
    src = lax.broadcasted_iota(jnp.int32, (2 * W_GROUP, W_GROUP), 1)
    spread = (dst == (src // HEAD_DIM) * 128 + src % HEAD_DIM).astype(BF16)
    wide_t = _bdot_nt(spread, p[:, 2 * W_GROUP:])
    row_w = lax.broadcasted_iota(jnp.int32, wide_t.shape, 0)
    v_ref[0] = jnp.where(row_w % 128 == HEAD_DIM, 1.0, wide_t).astype(BF16)


def _da_prep(p_da, cos_tab, sin_tab, qw_row, kw_row):
    b, lt, width = p_da.shape
    nb = lt // TOKEN_BLOCK
    blk = lambda w=W_GROUP: pl.BlockSpec((1, TOKEN_BLOCK, w), lambda i, t: (i, t, 0))
    return pl.pallas_call(
        _da_prep_kernel,
        out_shape=[jax.ShapeDtypeStruct((b, lt, W_GROUP), BF16)] * 2
                  + [jax.ShapeDtypeStruct((b, 2 * W_GROUP, lt), BF16)],
        grid=(b, nb),
        in_specs=[pl.BlockSpec((1, TOKEN_BLOCK, width), lambda i, t: (i, t, 0)),
                  pl.BlockSpec((TOKEN_BLOCK, W_GROUP), lambda i, t: (t, 0)),
                  pl.BlockSpec((TOKEN_BLOCK, W_GROUP), lambda i, t: (t, 0)),
                  pl.BlockSpec((1, W_GROUP), lambda i, t: (0, 0)),
                  pl.BlockSpec((1, W_GROUP), lambda i, t: (0, 0))],
        out_specs=[blk(), blk(), pl.BlockSpec((1, 2 * W_GROUP, TOKEN_BLOCK), lambda i, t: (i, 0, t))],
        compiler_params=_cparams(2), name="da_prep",
    )(p_da, cos_tab, sin_tab, qw_row, kw_row)


def _da_attn_kernel(q_ref, k_ref, v_ref, lam_ref, sw_ref, o_ref, *, lam_init):
    q = q_ref[0]
    k = k_ref[0]
    lp = lam_ref[...]
    lam = (jnp.exp(jnp.sum(lp[0:1] * lp[1:2], axis=1, keepdims=True))
           - jnp.exp(jnp.sum(lp[2:3] * lp[3:4], axis=1, keepdims=True)) + lam_init)
    lane = lax.broadcasted_iota(jnp.int32, q.shape, 1)

    def scores(m):
        lo = (m // 2) * HEAD_DIM + (m % 2) * DA_DK
        qm = jnp.where(jnp.logical_and(lane >= lo, lane < lo + DA_DK), q, jnp.zeros_like(q))
        return lax.dot_general(k, qm, (((1,), (1,)), ((), ())), preferred_element_type=F32).astype(BF16)

    n_maps = 2 * N_HEADS
    maps = []
    s_next = scores(0)
    for m in range(n_maps):
        s = s_next
        if m + 1 < n_maps:
            s_next = scores(m + 1)
        e = jnp.exp(s - jnp.max(s, axis=0, keepdims=True))
        vt_h = v_ref[0, (m // 2) * 128:(m // 2 + 1) * 128, :]
        acc = jnp.dot(vt_h, e, preferred_element_type=F32)
        maps.append(acc / acc[HEAD_DIM:HEAD_DIM + 1, :])
    outs = [(maps[2 * h] - lam * maps[2 * h + 1])[:HEAD_DIM] for h in range(N_HEADS)]
    o = jnp.concatenate(outs, axis=0).T
    g64 = _group_ones(W_GROUP, HEAD_DIM)
    o_ref[0] = _group_rms(o, g64, HEAD_DIM, sw_ref[...]) * (1.0 - lam_init)


def _da_attention(qh, kh, vh, lam_p, sw_row, lam_init, q_tile, q_blocks, q_off, k_rows, k_blk, name):
    b, lt, _ = qh.shape
    kern = functools.partial(_da_attn_kernel, lam_init=lam_init)
    return pl.pallas_call(
        kern,
        out_shape=jax.ShapeDtypeStruct((b, q_blocks * q_tile, W_GROUP), F32),
        grid=(b, q_blocks),
        in_specs=[pl.BlockSpec((1, q_tile, W_GROUP), lambda i, t: (i, t + q_off, 0)),
                  pl.BlockSpec((1, k_rows, W_GROUP), lambda i, t: (i, k_blk, 0)),
                  pl.BlockSpec((1, 2 * W_GROUP, k_rows), lambda i, t: (i, 0, k_blk)),
                  pl.BlockSpec((4, DA_DK), lambda i, t: (0, 0)),
                  pl.BlockSpec((1, W_GROUP), lambda i, t: (0, 0))],
        out_specs=pl.BlockSpec((1, q_tile, W_GROUP), lambda i, t: (i, t, 0)),
        compiler_params=_cparams(2), name=name,
    )(qh, kh, vh, lam_p, sw_row)


def _hy_conv_kernel(p_ref, hp_ref, hn_ref, cw_ref, v_ref, x1_ref, x2_ref, xp_ref, *, nbl):
    blk = pl.program_id(1)
    xc = _short_conv(p_ref[0], hp_ref.at[0], hn_ref.at[0], cw_ref, xp_ref, blk, nbl, 3 * W_GROUP)
    v_ref[0] = xc[:, :W_GROUP].astype(v_ref.dtype)
    x1_ref[0] = xc[:, W_GROUP:2 * W_GROUP].astype(x1_ref.dtype)
    x2_ref[0] = xc[:, 2 * W_GROUP:].astype(x2_ref.dtype)


def _hy_short_conv(p_hy, conv_w, nbl):
    b, lt, width = p_hy.shape
    nb = nbl + 1
    blk = lambda: pl.BlockSpec((1, TOKEN_BLOCK, W_GROUP), lambda i, t: (i, t, 0))
    return pl.pallas_call(
        functools.partial(_hy_conv_kernel, nbl=nbl),
        out_shape=[jax.ShapeDtypeStruct((b, lt, W_GROUP), BF16)] * 3,
        grid=(b, nb),
        in_specs=[pl.BlockSpec((1, TOKEN_BLOCK, width), lambda i, t: (i, t, 0))]
                 + _halo_specs(width, nbl, lambda s: s, lt)
                 + [pl.BlockSpec((3, width), lambda i, t: (0, 0))],
        out_specs=[blk(), blk(), blk()],
        scratch_shapes=[pltpu.VMEM((TOKEN_BLOCK + 16, width), F32)],
        compiler_params=_cparams(2), name="hy_short_conv",
    )(p_hy, p_hy, p_hy, conv_w)


def _hy_filter_kernel(z_ref, w1_ref, b1_ref, f1_ref, w2_ref, b2_ref, f2_ref, w3_ref, dec_ref, o_ref):
    z = z_ref[...]
    h = jnp.sin(f1_ref[...] * (_hp_dot(z, w1_ref[...]) + b1_ref[...]))
    h = jnp.sin(f2_ref[...] * (_hp_dot(h, w2_ref[...]) + b2_ref[...]))
    h = _hp_dot(h, w3_ref[...])
    h = h * jnp.exp(-z[:, 0:1] * jnp.abs(dec_ref[...]))
    first = (pl.program_id(0) * h.shape[0] + lax.broadcasted_iota(jnp.int32, (h.shape[0], 1), 0)) == 0
    for plane in range(4):
        piece = h[:, plane * W_GROUP:(plane + 1) * W_GROUP]
        o_ref[plane] = jnp.where(first, 0.0, piece) if plane >= 2 else piece


def _hy_filters(zfeat, w1p, b1, f1, w2, b2, f2, w3, dec_row):
    n = zfeat.shape[0]
    tn = min(n, 512)
    full = lambda a: pl.BlockSpec(a.shape, lambda i: (0,) * a.ndim)
    args = [w1p, b1, f1, w2, b2, f2, w3, dec_row]
    return pl.pallas_call(
        _hy_filter_kernel,
        out_shape=jax.ShapeDtypeStruct((4, n, W_GROUP), F32),
        grid=(n // tn,),
        in_specs=[pl.BlockSpec((tn, 128), lambda i: (i, 0))] + [full(a) for a in args],
        out_specs=pl.BlockSpec((4, tn, W_GROUP), lambda i: (0, i, 0)),
        compiler_params=_cparams(1), name="hy_filters",
    )(zfeat, *args)


def _left_split_dot(a, v):
    a_hi = a.astype(BF16)
    a_lo = (a - a_hi.astype(F32)).astype(BF16)
    return jnp.dot(a_hi, v, preferred_element_type=F32) + jnp.dot(a_lo, v, preferred_element_type=F32)


def _fft_stage1_kernel(a_ref, u_ref, o_ref):
    u = u_ref[0]
    y = _left_split_dot(a_ref[...], u) if u.dtype == BF16 else _hp_dot(a_ref[...], u)
    o_ref[0] = y.astype(o_ref.dtype)


def _fft_stage1(a_mat, u_view, row_blk, col_tile):
    nbatch, _, cols = u_view.shape
    m, kdim = a_mat.shape
    return pl.pallas_call(
        _fft_stage1_kernel,
        out_shape=jax.ShapeDtypeStruct((nbatch, m, cols), BF16),
        grid=(nbatch, cols // col_tile),
        in_specs=[pl.BlockSpec((m, kdim), lambda i, j: (0, 0)),
                  pl.BlockSpec((1, kdim, col_tile), lambda i, j: (i, row_blk, j))],
        out_specs=pl.BlockSpec((1, m, col_tile), lambda i, j: (i, 0, j)),
        compiler_params=_cparams(2), name="fft_stage1",
    )(a_mat, u_view)


def _fft_mid_kernel(*refs, n2, multiply):
    if multiply:
        y_ref, g_ref, kf_ref, kb_ref, gi_ref, o_ref = refs
    else:
        y_ref, g_ref, o_ref = refs
    group = range(FFT_K1_GROUP)
    z = [_bdot(g_ref[j], y_ref[0, :, j].reshape(2 * n2, W_GROUP)) for j in group]
    if multiply:
        prod = []
        for j in group:
            zr, zi = z[j][:n2], z[j][n2:]
            kr = kf_ref[0, 0, j].astype(F32) + kb_ref[0, 0, j].astype(F32)
            ki = kf_ref[0, 1, j].astype(F32) - kb_ref[0, 1, j].astype(F32)
            prod.append(jnp.concatenate([zr * kr - zi * ki, zr * ki + zi * kr], axis=0))
        z = [_bdot(gi_ref[j], prod[j]) for j in group]
    for j in group:
        o_ref[0, :, j] = z[j].reshape(2, n2, W_GROUP).astype(o_ref.dtype)


def _fft_mid(y1, g_fwd, kf, filt, g_inv, k1p, n2):
    nbatch = y1.shape[0]
    multiply = kf is not None
    data = lambda plane: pl.BlockSpec((1, 2, FFT_K1_GROUP, n2, W_GROUP),
                                      (lambda k, i: (i, 0, k, 0, 0)) if plane is None
                                      else (lambda k, i: (plane, 0, k, 0, 0)))
    mat = lambda: pl.BlockSpec((FFT_K1_GROUP, 2 * n2, 2 * n2), lambda k, i: (k, 0, 0))
    in_specs, args = [data(None), mat()], [y1, g_fwd]
    if multiply:
        in_specs += [data(filt), data(2 + filt), mat()]
        args += [kf, kf, g_inv]
    return pl.pallas_call(
        functools.partial(_fft_mid_kernel, n2=n2, multiply=multiply),
        out_shape=jax.ShapeDtypeStruct((nbatch, 2, k1p, n2, W_GROUP), BF16),
        grid=(k1p // FFT_K1_GROUP, nbatch),
        in_specs=in_specs,
        out_specs=data(None),
        compiler_params=_cparams(2), name="fft_mid_mul" if multiply else "fft_mid",
    )(*args)


def _fft_stage4_kernel(a_ref, v_ref, u_ref, gate_ref, skip_ref, o_ref):
    y = _left_split_dot(a_ref[...], v_ref[0])
    o_ref[0] = (gate_ref[0].astype(F32) * (y + skip_ref[...] * u_ref[0].astype(F32))).astype(o_ref.dtype)


def _fft_stage4(a_mat, v_view, u_view, u_blk, gate_view, gate_blk, skip_row, col_tile):
    nbatch, kdim, cols = v_view.shape
    m = a_mat.shape[0]
    return pl.pallas_call(
        _fft_stage4_kernel,
        out_shape=jax.ShapeDtypeStruct((nbatch, m, cols), BF16),
        grid=(nbatch, cols // col_tile),
        in_specs=[pl.BlockSpec((m, kdim), lambda i, j: (0, 0)),
                  pl.BlockSpec((1, kdim, col_tile), lambda i, j: (i, 0, j)),
                  pl.BlockSpec((1, m, col_tile), lambda i, j: (i, u_blk, j)),
                  pl.BlockSpec((1, m, col_tile), lambda i, j: (i, gate_blk, j)),
                  pl.BlockSpec((1, col_tile), lambda i, j: (0, j))],
        out_specs=pl.BlockSpec((1, m, col_tile), lambda i, j: (i, 0, j)),
        compiler_params=_cparams(2), name="fft_stage4",
    )(a_mat, v_view, u_view, gate_view, skip_row)


def _fft_tables(n, n2):
    big = 2 * n
    n1 = big // n2
    two_pi = 2.0 * math.pi

    def cs(num, den):
        ang = (num % den).astype(F32) * (two_pi / den)
        return jnp.cos(ang), jnp.sin(ang)

    k1p = -(-(n1 // 2 + 1) // FFT_K1_GROUP) * FFT_K1_GROUP
    k1 = jnp.arange(k1p, dtype=jnp.int32)
    live = (k1 <= n1 // 2).astype(F32)
    weight = jnp.where((k1 == 0) | (k1 == n1 // 2), 1.0, 2.0) * live
    m1 = jnp.arange(n1, dtype=jnp.int32)
    c, s = cs(k1[:, None] * m1[None, :], n1)
    a1_full = jnp.concatenate([c * live[:, None], -s * live[:, None]], axis=0)
    k2 = jnp.arange(n2, dtype=jnp.int32)
    num = (k2[None, :, None] * k2[None, None, :] * n1 + k1[:, None, None] * k2[None, None, :]) % big
    c, s = cs(num, big)
    g_fwd = jnp.concatenate([jnp.concatenate([c, s], axis=2), jnp.concatenate([-s, c], axis=2)], axis=1)
    num = (k2[None, :, None] * k2[None, None, :] * n1 + k1[:, None, None] * k2[None, :, None]) % big
    c, s = cs(num, big)
    g_inv = jnp.concatenate([jnp.concatenate([c, -s], axis=2), jnp.concatenate([s, c], axis=2)], axis=1)
    t1 = jnp.arange(n1 // 2, dtype=jnp.int32)
    c, s = cs(t1[:, None] * k1[None, :], n1)
    a4 = jnp.concatenate([c * weight[None, :], -s * weight[None, :]], axis=1) * (1.0 / big)
    return n1, k1p, a1_full, g_fwd.astype(BF16), g_inv.astype(BF16), a4


def _hy_positional(n):
    pos = jnp.arange(n, dtype=F32)
    t01 = pos / max(n - 1, 1)
    bands = jnp.linspace(1e-4, HY_BANDS - 1, HY_BANDS, dtype=F32)
    ang = (2.0 * math.pi / n) * pos[:, None] * bands
    z = jnp.concatenate([t01[:, None], jnp.cos(ang), -jnp.sin(ang)], axis=-1)
    return jnp.pad(z, ((0, 0), (0, 128 - z.shape[1])))


def _hyena_segment(v_all, x1_all, x2_all, seg_start, n, n2, filt_args, skip):
    b, lt, c = v_all.shape
    n1, k1p, a1_full, g_fwd, g_inv, a4 = _fft_tables(n, n2)
    a1_half = a1_full[:, :n1 // 2]
    cols = n2 * c
    col_tile = min(cols, 8192)
    seg_blk = seg_start // n
    view = lambda a: a.reshape(b, lt // n2, cols)

    h = _hy_filters(_hy_positional(n), *filt_args)
    ky = _fft_stage1(a1_half, h.reshape(4, n1 // 2, cols), 0, col_tile)
    kf = _fft_mid(ky.reshape(4, 2, k1p, n2, c), g_fwd, None, 0, None, k1p, n2)

    def long_conv(u_view, u_blk, gate_all, filt):
        y1 = _fft_stage1(a1_half, u_view, u_blk, col_tile)
        vmid = _fft_mid(y1.reshape(b, 2, k1p, n2, c), g_fwd, kf, filt, g_inv, k1p, n2)
        skip_row = jnp.tile(skip[filt], n2).reshape(1, cols)
        return _fft_stage4(a4, vmid.reshape(b, 2 * k1p, cols), u_view, u_blk, view(gate_all), seg_blk,
                           skip_row, col_tile)

    z = long_conv(view(v_all), seg_blk, x1_all, 0)
    out = long_conv(z, 0, x2_all, 1)
    return out.reshape(b, n, c)


def _rope_tables(rows, lt):
    n_freq = DA_DK // 4
    inv = ROPE_THETA ** (-jnp.arange(n_freq, dtype=F32) / n_freq)
    r = jnp.repeat(jnp.arange(rows, dtype=F32), GRID_W)
    col = jnp.tile(jnp.arange(GRID_W, dtype=F32), rows)
    ang_r, ang_c = r[:, None] * inv, col[:, None] * inv
    cos32 = jnp.concatenate([jnp.cos(ang_r)] * 2 + [jnp.cos(ang_c)] * 2, axis=-1)
    sin32 = jnp.concatenate([-jnp.sin(ang_r), jnp.sin(ang_r), -jnp.sin(ang_c), jnp.sin(ang_c)], axis=-1)
    n = rows * GRID_W
    pad = lt - n
    cos = jnp.concatenate([jnp.tile(cos32, (1, W_GROUP // DA_DK)), jnp.ones((pad, W_GROUP), F32)], axis=0)
    sin = jnp.concatenate([jnp.tile(sin32, (1, W_GROUP // DA_DK)), jnp.zeros((pad, W_GROUP), F32)], axis=0)
    return cos, sin


def kernel(x, c, ctx, c_ctx, mod_w, mod_b, ln1_w, ln2_w, w_in, gdn_conv_w, gdn_a_log, gdn_dt_bias, gdn_norm_w, hy_conv_w, hy_w1, hy_b1, hy_f1, hy_w2, hy_b2, hy_f2, hy_w3, hy_decay, hy_bias, hg_lb_raw, hg_norm_w, da_q_norm, da_k_norm, da_lam, da_subln, w_out, mlp_w1, mlp_w2):
    batch, seq, d = x.shape
    n_ctx = ctx.shape[1]
    depth = mod_w.shape[0]
    assert n_ctx == TOKEN_BLOCK and seq % TOKEN_BLOCK == 0 and d == 4 * W_GROUP
    nbl = seq // TOKEN_BLOCK
    lt = seq + n_ctx

    xs = jnp.concatenate([x, ctx], axis=1)
    cond = jnp.concatenate([c, c_ctx[None, :], jnp.zeros((-(batch + 1) % 8, d), F32)], axis=0)
    mods = _modulation(cond, mod_w, mod_b)[:, :batch + 1].reshape(depth, batch + 1, 6, d)
    mods = jnp.pad(mods, ((0, 0), (0, 0), (0, 2), (0, 0)))

    lb_all = jnp.cumsum(jax.nn.softmax(hg_lb_raw.astype(F32), axis=0), axis=0)
    lb_all = lb_all - lb_all[0]
    cos_tab, sin_tab = _rope_tables(seq // GRID_W, lt)

    gdn_cols = 4 * W_GROUP + 4 * N_HEADS
    hy_cols, hg_cols, da_cols = 3 * W_GROUP, 5 * W_GROUP, 3 * W_GROUP
    splits = (GDN_PAD_COLS, hy_cols, hg_cols, da_cols)

    for l in range(depth):
        want_ctx = l < depth - 1
        modtab = mods[l]
        wl = w_in[l]
        w_pad = jnp.concatenate([wl[:, :gdn_cols], jnp.zeros((d, GDN_PAD_COLS - gdn_cols), F32),
                                 wl[:, gdn_cols:]], axis=1).astype(BF16)
        p_gdn, p_hy, p_hg, p_da = _adaln_proj(xs, modtab, ln1_w[l], w_pad, splits, (F32,) * 4, 0, 1, False,
                                              nbl, "in_proj")

        pad16 = lambda a: jnp.pad(a.reshape(1, 2 * N_HEADS), ((0, 0), (0, 128 - 2 * N_HEADS)))
        alog_row, dtb_row = pad16(gdn_a_log[l]), pad16(gdn_dt_bias[l])
        gnw = jnp.tile(gdn_norm_w[l], N_HEADS).reshape(1, W_GROUP)
        o_b = _gdn_direction(p_gdn, gdn_conv_w[l], alog_row, dtb_row, gnw, None, 1, nbl)
        o_gdn = _gdn_direction(p_gdn, gdn_conv_w[l], alog_row, dtb_row, gnw, o_b, 0, nbl)

        hv, hx1, hx2 = _hy_short_conv(p_hy, hy_conv_w[l], nbl)
        w1p = jnp.pad(hy_w1[l], ((0, 128 - hy_w1.shape[1]), (0, 0)))
        row = lambda a: a.reshape(1, -1)
        dec_row = jnp.tile(hy_decay[l].reshape(1, 2 * W_GROUP), (1, 2))
        filt_args = (w1p, row(hy_b1[l]), row(hy_f1[l]), hy_w2[l], row(hy_b2[l]), row(hy_f2[l]), hy_w3[l], dec_row)
        o_hy_lat = _hyena_segment(hv, hx1, hx2, 0, seq, FFT_N2_LAT, filt_args, hy_bias[l])
        o_hy_ctx = (_hyena_segment(hv, hx1, hx2, seq, n_ctx, FFT_N2_CTX, filt_args, hy_bias[l])
                    if want_ctx else None)

        lb_row = lb_all[l].reshape(1, W_GROUP)
        hnw = jnp.tile(hg_norm_w[l], N_HEADS).reshape(1, W_GROUP)
        o_hb = _hgrn_direction(p_hg, lb_row, hnw, None, 1, nbl)
        o_hg = _hgrn_direction(p_hg, lb_row, hnw, o_hb, 0, nbl)

        lam_init = 0.8 - 0.6 * math.exp(-0.3 * l)
        qw = jnp.tile(da_q_norm[l], W_GROUP // DA_DK).reshape(1, W_GROUP)
        kw = jnp.tile(da_k_norm[l], W_GROUP // DA_DK).reshape(1, W_GROUP)
        sw = jnp.tile(da_subln[l], N_HEADS).reshape(1, W_GROUP)
        qh, kh, vh = _da_prep(p_da, cos_tab, sin_tab, qw, kw)
        o_da = _da_attention(qh, kh, vh, da_lam[l], sw, lam_init, DA_Q_TILE, seq // DA_Q_TILE, 0, lt, 0,
                             "da_attn_lat")
        o_da_ctx = (_da_attention(qh, kh, vh, da_lam[l], sw, lam_init, TOKEN_BLOCK, 1, nbl, TOKEN_BLOCK, nbl,
                                  "da_attn_ctx") if want_ctx else None)

        n_blocks = nbl + 1 if want_ctx else nbl
        xs_mid = _proj_residual(xs, modtab, w_out[l].astype(BF16),
                                [o_gdn, (o_hy_lat, o_hy_ctx), o_hg, (o_da, o_da_ctx)], 2, n_blocks, nbl, "out_proj")
        (hid,) = _adaln_proj(xs_mid, modtab, ln2_w[l], mlp_w1[l].astype(BF16), (mlp_w1.shape[2],), (BF16,),
                             3, 4, True, nbl, "mlp_up")
        xs = _proj_residual(xs_mid, modtab, mlp_w2[l].astype(BF16), [hid], 5, n_blocks, nbl, "mlp_down")
    return xs
```

```python
import functools
import math

import jax
import jax.numpy as jnp
from jax import lax
from jax.experimental import pallas as pl
from jax.experimental.pallas import tpu as pltpu

F32 = jnp.float32
BF16 = jnp.bfloat16

EPS = 1e-6
TOKEN_BLOCK = 256
CHUNK = 64
HEAD_DIM = 64
N_HEADS = 4
W_GROUP = 256
DA_DK = 32
GRID_W = 64
ROPE_THETA = 10000.0
HY_BANDS = 16
FFT_N2_LAT = 128
FFT_N2_CTX = 16
FFT_K1_GROUP = 8
DA_Q_TILE = 256
SCAN_ROWS = 4
GDN_PAD_COLS = 1152
VMEM_LIMIT_BYTES = 56 * 1024 * 1024


def _cparams(n_axes):
    return pltpu.CompilerParams(dimension_semantics=("arbitrary",) * n_axes,
                                vmem_limit_bytes=VMEM_LIMIT_BYTES)


def _sigmoid(x):
    return 1.0 / (1.0 + jnp.exp(-x))


def _silu(x):
    return x * _sigmoid(x)


def _softplus(x):
    return jnp.maximum(x, 0.0) + jnp.log(1.0 + jnp.exp(-jnp.abs(x)))


def _log_sigmoid(x):
    return jnp.minimum(x, 0.0) - jnp.log(1.0 + jnp.exp(-jnp.abs(x)))


def _bdot(a, b):
    return jnp.dot(a.astype(BF16), b.astype(BF16), preferred_element_type=F32)


def _bdot_nt(a, b):
    return lax.dot_general(a.astype(BF16), b.astype(BF16), (((1,), (1,)), ((), ())),
                           preferred_element_type=F32)


def _split3(x):
    hi = x.astype(BF16)
    r = x - hi.astype(F32)
    mid = r.astype(BF16)
    lo = (r - mid.astype(F32)).astype(BF16)
    return hi, mid, lo


def _exact_left(m_bf16, x):
    hi, mid, lo = _split3(x)
    d = lambda p: jnp.dot(m_bf16, p, preferred_element_type=F32)
    return d(hi) + d(mid) + d(lo)


def _exact_right(x, m_bf16):
    hi, mid, lo = _split3(x)
    d = lambda p: jnp.dot(p, m_bf16, preferred_element_type=F32)
    return d(hi) + d(mid) + d(lo)


def _hp_dot(a, b):
    ah = a.astype(BF16)
    al = (a - ah.astype(F32)).astype(BF16)
    bh = b.astype(BF16)
    bl = (b - bh.astype(F32)).astype(BF16)
    d = lambda p, q: jnp.dot(p, q, preferred_element_type=F32)
    return d(ah, bh) + d(ah, bl) + d(al, bh)


def _group_ones(width, group):
    r = lax.broadcasted_iota(jnp.int32, (width, width), 0) // group
    c = lax.broadcasted_iota(jnp.int32, (width, width), 1) // group
    return (r == c).astype(BF16)


def _mod_kernel(s_ref, w_ref, b_ref, o_ref):
    s = _silu(s_ref[...])
    o_ref[0] = _bdot(s, w_ref[0]) + b_ref[0]


def _modulation(cond, mod_w, mod_b):
    depth, d, n = mod_w.shape
    rows = cond.shape[0]
    tn = 1536
    return pl.pallas_call(
        _mod_kernel,
        out_shape=jax.ShapeDtypeStruct((depth, rows, n), F32),
        grid=(depth, n // tn),
        in_specs=[pl.BlockSpec((rows, d), lambda l, j: (0, 0)),
                  pl.BlockSpec((1, d, tn), lambda l, j: (l, 0, j)),
                  pl.BlockSpec((1, 1, tn), lambda l, j: (l, 0, j))],
        out_specs=pl.BlockSpec((1, rows, tn), lambda l, j: (l, 0, j)),
        compiler_params=_cparams(2), name="modulation",
    )(cond, mod_w, mod_b.reshape(depth, 1, n))


def _adaln_proj_kernel(x_ref, mod_ref, lnw_ref, w_ref, *out_refs, shift_row, scale_row, splits, sq_relu):
    x = x_ref[0]
    ms = jnp.mean(x * x, axis=-1, keepdims=True)
    y = x * lax.rsqrt(ms + EPS) * lnw_ref[...]
    mod = mod_ref[0]
    y = y * (1.0 + mod[scale_row:scale_row + 1]) + mod[shift_row:shift_row + 1]
    h = _bdot(y, w_ref[...])
    if sq_relu:
        h = jnp.square(jnp.maximum(h, 0.0))
    off = 0
    for o_ref, n in zip(out_refs, splits):
        o_ref[0] = h[:, off:off + n].astype(o_ref.dtype)
        off += n


def _mod_index(nbl, batch):
    return lambda b, t: (jnp.where(t == nbl, batch, b), 0, 0)


def _adaln_proj(xs, modtab, ln_w, w, splits, out_dtypes, shift_row, scale_row, sq_relu, nbl, name):
    b, lt, d = xs.shape
    nb = lt // TOKEN_BLOCK
    n = w.shape[1]
    kern = functools.partial(_adaln_proj_kernel, shift_row=shift_row, scale_row=scale_row,
                             splits=splits, sq_relu=sq_relu)
    return pl.pallas_call(
        kern,
        out_shape=[jax.ShapeDtypeStruct((b, lt, s), dt) for s, dt in zip(splits, out_dtypes)],
        grid=(b, nb),
        in_specs=[pl.BlockSpec((1, TOKEN_BLOCK, d), lambda i, t: (i, t, 0)),
                  pl.BlockSpec((1, 8, d), _mod_index(nbl, b)),
                  pl.BlockSpec((1, d), lambda i, t: (0, 0)),
                  pl.BlockSpec((d, n), lambda i, t: (0, 0))],
        out_specs=[pl.BlockSpec((1, TOKEN_BLOCK, s), lambda i, t: (i, t, 0)) for s in splits],
        compiler_params=_cparams(2), name=name,
    )(xs, modtab, ln_w.reshape(1, d), w)


def _proj_residual_kernel(x_ref, mod_ref, w_ref, *rest, gate_row, split, nbl):
    a_refs, o_ref = rest[:-1], rest[-1]
    is_ctx = pl.program_id(1) == nbl
    pieces, pos = [], 0
    for two in split:
        lat = a_refs[pos][0].astype(BF16)
        pieces.append(jnp.where(is_ctx, a_refs[pos + 1][0].astype(BF16), lat) if two else lat)
        pos += 2 if two else 1
    a = jnp.concatenate(pieces, axis=1) if len(pieces) > 1 else pieces[0]
    gate = mod_ref[0][gate_row:gate_row + 1]
    o_ref[0] = x_ref[0] + gate * _bdot(a, w_ref[...])


def _proj_residual(xs, modtab, w, acts, gate_row, n_blocks, nbl, name):
    b, lt, d = xs.shape
    split = tuple(isinstance(a, tuple) and a[1] is not None for a in acts)
    specs, args = [], []
    for a, two in zip(acts, split):
        lat = a[0] if isinstance(a, tuple) else a
        width = lat.shape[2]
        if isinstance(a, tuple):
            specs.append(pl.BlockSpec((1, TOKEN_BLOCK, width), lambda i, t: (i, jnp.minimum(t, nbl - 1), 0)))
        else:
            specs.append(pl.BlockSpec((1, TOKEN_BLOCK, width), lambda i, t: (i, t, 0)))
        args.append(lat)
        if two:
            specs.append(pl.BlockSpec((1, TOKEN_BLOCK, width), lambda i, t: (i, 0, 0)))
            args.append(a[1])
    kern = functools.partial(_proj_residual_kernel, gate_row=gate_row, split=split, nbl=nbl)
    return pl.pallas_call(
        kern,
        out_shape=jax.ShapeDtypeStruct((b, n_blocks * TOKEN_BLOCK, d), F32),
        grid=(b, n_blocks),
        in_specs=[pl.BlockSpec((1, TOKEN_BLOCK, d), lambda i, t: (i, t, 0)),
                  pl.BlockSpec((1, 8, d), _mod_index(nbl, b)),
                  pl.BlockSpec(w.shape, lambda i, t: (0, 0))] + specs,
        out_specs=pl.BlockSpec((1, TOKEN_BLOCK, d), lambda i, t: (i, t, 0)),
        compiler_params=_cparams(2), name=name,
    )(xs, modtab, w, *args)


def _scan_block(step, nbl, direction):
    lat = step - 1 if direction == 0 else nbl - step
    return jnp.where(step == 0, nbl, lat)


def _halo_specs(width, nbl, block_of_step, lt, batch_rows=1):
    rows8 = TOKEN_BLOCK // 8

    def prev_map(b, s):
        return (b, jnp.maximum(block_of_step(s) * rows8 - 1, 0), 0)

    def next_map(b, s):
        return (b, jnp.minimum((block_of_step(s) + 1) * rows8, lt // 8 - 1), 0)

    return [pl.BlockSpec((batch_rows, 8, width), prev_map), pl.BlockSpec((batch_rows, 8, width), next_map)]


def _short_conv(cur, hp_ref, hn_ref, cw_ref, xp_ref, blk, nbl, width):
    seg_first = jnp.logical_or(blk == 0, blk == nbl)
    seg_last = jnp.logical_or(blk == nbl - 1, blk == nbl)
    xp_ref[0:8, :] = jnp.where(seg_first, 0.0, hp_ref[:, :width])
    xp_ref[8:8 + TOKEN_BLOCK, :] = cur
    xp_ref[8 + TOKEN_BLOCK:16 + TOKEN_BLOCK, :] = jnp.where(seg_last, 0.0, hn_ref[:, :width])
    cw = cw_ref[...]
    return (cw[0:1] * xp_ref[7:7 + TOKEN_BLOCK, :] + cw[1:2] * xp_ref[8:8 + TOKEN_BLOCK, :]
            + cw[2:3] * xp_ref[9:9 + TOKEN_BLOCK, :])


def _scan_tables(direction, chunk):
    i = jnp.arange(TOKEN_BLOCK, dtype=jnp.int32)[:, None]
    j = jnp.arange(TOKEN_BLOCK, dtype=jnp.int32)[None, :]
    same = (i // chunk) == (j // chunk)
    pos = (lambda t: t % chunk) if direction == 0 else (lambda t: chunk - 1 - t % chunk)
    pi, pj = pos(i), pos(j)
    eye = i == j
    fl = [same & (pj <= pi), same & (pj < pi), eye]
    bl = [same & (pj <= pi), (i // HEAD_DIM) == (j // HEAD_DIM), eye]
    m = 1
    while m < chunk:
        fl.append(same & ((pi // (2 * m)) == (pj // (2 * m))) & ((pi // m) % 2 == 1) & ((pj // m) % 2 == 0))
        boundary = (pi // (2 * m)) * (2 * m) + m - 1
        bl.append((same & (pj <= pi)).astype(F32) - (same & (pj <= boundary)).astype(F32))
        m *= 2
    return jnp.stack([a.astype(F32) for a in fl]), jnp.stack([a.astype(BF16) for a in bl])


def _group_rms(o, gmat, group, weight):
    ms = jnp.dot((o * o).astype(BF16), gmat, preferred_element_type=F32) * (1.0 / group)
    return o * lax.rsqrt(ms + EPS) * weight


def _gdn_kernel(*refs, direction, finish, nbl):
    if finish:
        (p_ref, hp_ref, hn_ref, cw_ref, alog_ref, dtb_ref, nw_ref, bm_ref, other_ref,
         o_ref, s_ref, xp_ref, ob_ref) = refs
    else:
        p_ref, hp_ref, hn_ref, cw_ref, alog_ref, dtb_ref, nw_ref, bm_ref, o_ref, s_ref, xp_ref, ob_ref = refs
    step = pl.program_id(1)
    blk = _scan_block(step, nbl, direction)

    @pl.when(step == 0)
    def _():
        s_ref[...] = jnp.zeros_like(s_ref)

    g64 = bm_ref[1]
    rows = range(SCAN_ROWS)
    p_rows, q_rows, k_rows, v_rows, kt_rows, zc_rows, zr_rows = [], [], [], [], [], [], []
    for r in rows:
        p = p_ref[r]
        xc = _silu(_short_conv(p[:, :3 * W_GROUP], hp_ref.at[r], hn_ref.at[r], cw_ref, xp_ref.at[r], blk, nbl,
                               3 * W_GROUP))
        q = xc[:, :W_GROUP]
        k = xc[:, W_GROUP:2 * W_GROUP]
        q = q * lax.rsqrt(_bdot(q * q, g64) + EPS) * (HEAD_DIM ** -0.5)
        k = k * lax.rsqrt(_bdot(k * k, g64) + EPS)
        ab = p[:, 4 * W_GROUP:4 * W_GROUP + 128]
        lane = lax.broadcasted_iota(jnp.int32, ab.shape, 1)
        g_all = -jnp.exp(alog_ref[...]) * _softplus(ab + dtb_ref[...])
        w_all = jnp.where(lane < 2 * N_HEADS, g_all, _sigmoid(ab))
        z_col = jnp.where(lane < 2 * N_HEADS, _exact_left(bm_ref[0], w_all), w_all)
        p_rows.append(p)
        q_rows.append(q)
        k_rows.append(k)
        v_rows.append(xc[:, 2 * W_GROUP:])
        kt_rows.append(_bdot_nt(bm_ref[2], k))
        zc_rows.append(z_col)
        zr_rows.append(z_col.T)

    n_chunks = TOKEN_BLOCK // CHUNK
    problems = [(r, h) for r in rows for h in range(N_HEADS)]
    heads = range(len(problems))
    lanes = [slice(h * HEAD_DIM, (h + 1) * HEAD_DIM) for _, h in problems]
    gidx = [N_HEADS * direction + h for _, h in problems]
    q = [q_rows[r] for r, _ in problems]
    k = [k_rows[r] for r, _ in problems]
    v = [v_rows[r] for r, _ in problems]
    k_t = [kt_rows[r] for r, _ in problems]
    z_col = [zc_rows[r] for r, _ in problems]
    z_row = [zr_rows[r] for r, _ in problems]
    ci_ = lax.broadcasted_iota(jnp.int32, (CHUNK, CHUNK), 0)
    cj_ = lax.broadcasted_iota(jnp.int32, (CHUNK, CHUNK), 1)
    pi_, pj_ = (ci_, cj_) if direction == 0 else (CHUNK - 1 - ci_, CHUNK - 1 - cj_)
    incl = pj_ <= pi_
    strict_f = (pj_ < pi_).astype(F32)
    eye_f = (ci_ == cj_).astype(F32)

    def half_f(m):
        return (((pi_ // (2 * m)) == (pj_ // (2 * m))) & ((pi_ // m) % 2 == 1) & ((pj_ // m) % 2 == 0)).astype(F32)

    chunked = lambda a: a.reshape(n_chunks, CHUNK, a.shape[1])
    bmm = lambda a, b: jnp.einsum("cij,cjk->cik", a.astype(BF16), b.astype(BF16), preferred_element_type=F32)
    bmm_nt = lambda a, b: jnp.einsum("cid,cjd->cij", a.astype(BF16), b.astype(BF16), preferred_element_type=F32)
    gc_c = [chunked(z_col[i][:, gi:gi + 1]) for i, gi in enumerate(gidx)]
    gc_r = [jnp.stack([z_row[i][gi:gi + 1, c * CHUNK:(c + 1) * CHUNK] for c in range(n_chunks)])
            for i, gi in enumerate(gidx)]
    beta = [chunked(z_col[i][:, 2 * N_HEADS + gi:2 * N_HEADS + gi + 1]) for i, gi in enumerate(gidx)]
    qc = [chunked(q[i][:, ls]) for i, ls in enumerate(lanes)]
    kc = [chunked(k[i][:, ls]) for i, ls in enumerate(lanes)]
    vc = [chunked(v[i][:, ls]) for i, ls in enumerate(lanes)]
    eg = [jnp.exp(g) for g in gc_c]
    decay = [jnp.exp(jnp.where(incl, gc_c[h] - gc_r[h], -jnp.inf)) for h in heads]
    kk = [bmm_nt(kc[h], kc[h]) for h in heads]
    qk = [bmm_nt(qc[h], kc[h]) for h in heads]
    n = [kk[h] * beta[h] * decay[h] * strict_f for h in heads]
    inv = [eye_f - n[h] * half_f(1) for h in heads]
    m = 2
    while m < CHUNK:
        half = half_f(m)
        x = [bmm(inv[h], n[h] * half) for h in heads]
        inv = [inv[h] - bmm(x[h], inv[h]) for h in heads]
        m *= 2
    uw = [bmm(inv[h], jnp.concatenate([vc[h] * beta[h], kc[h] * (beta[h] * eg[h])], axis=2)) for h in heads]
    attn = [qk[h] * decay[h] for h in heads]
    qg = [qc[h] * eg[h] for h in heads]
    state = [s_ref[h] for h in heads]

    last = CHUNK - 1 if direction == 0 else 0
    for ci in range(n_chunks):
        c = ci if direction == 0 else n_chunks - 1 - ci
        rs = slice(c * CHUNK, (c + 1) * CHUNK)
        g_last = [gc_c[h][c, last:last + 1, :] for h in heads]
        v_new = [uw[h][c, :, :HEAD_DIM] - _bdot(uw[h][c, :, HEAD_DIM:], state[h]) for h in heads]
        for h, ls in enumerate(lanes):
            ob_ref[problems[h][0], rs, ls] = _bdot(qg[h][c], state[h]) + _bdot(attn[h][c], v_new[h])
        state = [state[h] * jnp.exp(g_last[h])
                 + _bdot(k_t[h][ls, rs] * jnp.exp(g_last[h] - gc_r[h][c]), v_new[h]) for h, ls in enumerate(lanes)]
    for h in heads:
        s_ref[h] = state[h]

    for r in rows:
        if finish:
            o = ob_ref[r] + other_ref[r]
            gate = _silu(p_rows[r][:, 3 * W_GROUP:4 * W_GROUP])
            o_ref[r] = _group_rms(o, g64, HEAD_DIM, nw_ref[...]) * gate
        else:
            o_ref[r] = ob_ref[r]


def _gdn_direction(p_gdn, conv_w, alog_row, dtb_row, nw_row, other, direction, nbl):
    b, lt, width = p_gdn.shape
    nb = nbl + 1
    blk_of = lambda s: _scan_block(s, nbl, direction)
    finish = other is not None
    bmask = _scan_tables(direction, CHUNK)[1][:3]
    assert b % SCAN_ROWS == 0
    kern = functools.partial(_gdn_kernel, direction=direction, finish=finish, nbl=nbl)
    in_specs = ([pl.BlockSpec((SCAN_ROWS, TOKEN_BLOCK, width), lambda i, s: (i, blk_of(s), 0))]
                + _halo_specs(3 * W_GROUP, nbl, blk_of, lt, SCAN_ROWS)
                + [pl.BlockSpec((3, 3 * W_GROUP), lambda i, s: (0, 0)),
                   pl.BlockSpec((1, 128), lambda i, s: (0, 0)),
                   pl.BlockSpec((1, 128), lambda i, s: (0, 0)),
                   pl.BlockSpec((1, W_GROUP), lambda i, s: (0, 0)),
                   pl.BlockSpec(bmask.shape, lambda i, s: (0, 0, 0))])
    args = [p_gdn, p_gdn, p_gdn, conv_w, alog_row, dtb_row, nw_row, bmask]
    if finish:
        in_specs.append(pl.BlockSpec((SCAN_ROWS, TOKEN_BLOCK, W_GROUP), lambda i, s: (i, blk_of(s), 0)))
        args.append(other)
    return pl.pallas_call(
        kern,
        out_shape=jax.ShapeDtypeStruct((b, lt, W_GROUP), F32),
        grid=(b // SCAN_ROWS, nb),
        in_specs=in_specs,
        out_specs=pl.BlockSpec((SCAN_ROWS, TOKEN_BLOCK, W_GROUP), lambda i, s: (i, blk_of(s), 0)),
        scratch_shapes=[pltpu.VMEM((SCAN_ROWS * N_HEADS, HEAD_DIM, HEAD_DIM), F32),
                        pltpu.VMEM((SCAN_ROWS, TOKEN_BLOCK + 16, 3 * W_GROUP), F32),
                        pltpu.VMEM((SCAN_ROWS, TOKEN_BLOCK, W_GROUP), F32)],
        compiler_params=_cparams(2), name=f"gdn_dir{direction}",
    )(*args)


def _hgrn_kernel(*refs, direction, finish, nbl):
    if finish:
        p_ref, lb_ref, nw_ref, bm_ref, cum_ref, other_ref, o_ref, s_ref, ob_ref = refs
    else:
        p_ref, lb_ref, nw_ref, bm_ref, cum_ref, o_ref, s_ref, ob_ref = refs
    step = pl.program_id(1)

    @pl.when(step == 0)
    def _():
        s_ref[...] = jnp.zeros_like(s_ref)

    lb = lb_ref[...]
    n_levels = cum_ref.shape[0] // TOKEN_BLOCK - 1
    rows = range(SCAN_ROWS)
    p_rows, q_rows, key_rows, v_rows, vt_rows, gcum_rows, qg_rows, lev_rows = [], [], [], [], [], [], [], []
    for r in rows:
        p = p_ref[r]
        q = _silu(p[:, :W_GROUP])
        v = p[:, W_GROUP:2 * W_GROUP]
        fl = p[:, (2 + direction) * W_GROUP:(3 + direction) * W_GROUP]
        a = jnp.log(lb)
        bb = jnp.log(1.0 - lb) + _log_sigmoid(fl)
        logf = jnp.maximum(a, bb) + jnp.log(1.0 + jnp.exp(-jnp.abs(a - bb)))
        logf_hi = logf.astype(BF16)
        logf_lo = (logf - logf_hi.astype(F32)).astype(BF16)
        cums = (jnp.dot(cum_ref[...], logf_hi, preferred_element_type=F32)
                + jnp.dot(cum_ref[...], logf_lo, preferred_element_type=F32))
        gcum = cums[:TOKEN_BLOCK]
        p_rows.append(p)
        q_rows.append(q)
        key_rows.append((1.0 - lb) * _sigmoid(-fl))
        v_rows.append(v)
        vt_rows.append(_bdot_nt(bm_ref[2], v))
        gcum_rows.append(gcum)
        qg_rows.append(q * jnp.exp(gcum))
        lev_rows.append([jnp.exp(-jnp.abs(cums[(lev + 1) * TOKEN_BLOCK:(lev + 2) * TOKEN_BLOCK]))
                         for lev in range(n_levels)])

    n_chunks = TOKEN_BLOCK // CHUNK
    problems = [(r, h) for r in rows for h in range(N_HEADS)]
    heads = range(len(problems))
    lanes = [slice(h * HEAD_DIM, (h + 1) * HEAD_DIM) for _, h in problems]
    ci_ = lax.broadcasted_iota(jnp.int32, (CHUNK, CHUNK), 0)
    cj_ = lax.broadcasted_iota(jnp.int32, (CHUNK, CHUNK), 1)
    pi_, pj_ = (ci_, cj_) if direction == 0 else (CHUNK - 1 - ci_, CHUNK - 1 - cj_)
    chunked = lambda a: a.reshape(n_chunks, CHUNK, a.shape[1])
    bmm = lambda a, b: jnp.einsum("cij,cjk->cik", a.astype(BF16), b.astype(BF16), preferred_element_type=F32)
    bmm_nt = lambda a, b: jnp.einsum("cid,cjd->cij", a.astype(BF16), b.astype(BF16), preferred_element_type=F32)
    qc = [chunked(q_rows[r][:, ls]) for (r, _), ls in zip(problems, lanes)]
    kc = [chunked(key_rows[r][:, ls]) for (r, _), ls in zip(problems, lanes)]
    amat = [bmm_nt(qc[h], kc[h]) * (ci_ == cj_).astype(F32) for h in heads]
    for lev in range(n_levels):
        m = 1 << lev
        half = (((pi_ // (2 * m)) == (pj_ // (2 * m))) & ((pi_ // m) % 2 == 1) & ((pj_ // m) % 2 == 0)).astype(F32)
        e = [chunked(lev_rows[r][lev][:, ls]) for (r, _), ls in zip(problems, lanes)]
        amat = [amat[h] + half * bmm_nt(qc[h] * e[h], kc[h] * e[h]) for h in heads]
    intra = [bmm(amat[h], chunked(v_rows[r][:, ls])) for h, ((r, _), ls) in enumerate(zip(problems, lanes))]
    state = [s_ref[h] for h in heads]

    last = CHUNK - 1 if direction == 0 else 0
    for ci in range(n_chunks):
        c = ci if direction == 0 else n_chunks - 1 - ci
        rs = slice(c * CHUNK, (c + 1) * CHUNK)
        g_end = [gcum_rows[r][c * CHUNK + last:c * CHUNK + last + 1, :] for r in rows]
        kdec = [key_rows[r][rs, :] * jnp.exp(g_end[r] - gcum_rows[r][rs, :]) for r in rows]
        decay_end = [jnp.exp(g_end[r]) for r in rows]
        for h, ((r, _), ls) in enumerate(zip(problems, lanes)):
            ob_ref[r, rs, ls] = _bdot_nt(qg_rows[r][rs, ls], state[h]) + intra[h][c]
        state = [state[h] * decay_end[r][:, ls] + _bdot(vt_rows[r][ls, rs], kdec[r][:, ls])
                 for h, ((r, _), ls) in enumerate(zip(problems, lanes))]
    for h in heads:
        s_ref[h] = state[h]

    for r in rows:
        if finish:
            o = ob_ref[r] + other_ref[r]
            gate = _silu(p_rows[r][:, 4 * W_GROUP:5 * W_GROUP])
            o_ref[r] = _group_rms(o, bm_ref[1], HEAD_DIM, nw_ref[...]) * gate
        else:
            o_ref[r] = ob_ref[r]


def _hgrn_direction(p_hg, lb_row, nw_row, other, direction, nbl):
    b, lt, width = p_hg.shape
    nb = nbl + 1
    blk_of = lambda s: _scan_block(s, nbl, direction)
    finish = other is not None
    tables = _scan_tables(direction, CHUNK)[1]
    bmask = tables[:3]
    cum_tab = jnp.concatenate([tables[0:1], tables[3:]], axis=0).reshape(-1, TOKEN_BLOCK)
    assert b % SCAN_ROWS == 0
    kern = functools.partial(_hgrn_kernel, direction=direction, finish=finish, nbl=nbl)
    in_specs = [pl.BlockSpec((SCAN_ROWS, TOKEN_BLOCK, width), lambda i, s: (i, blk_of(s), 0)),
                pl.BlockSpec((1, W_GROUP), lambda i, s: (0, 0)),
                pl.BlockSpec((1, W_GROUP), lambda i, s: (0, 0)),
                pl.BlockSpec(bmask.shape, lambda i, s: (0, 0, 0)),
                pl.BlockSpec(cum_tab.shape, lambda i, s: (0, 0))]
    args = [p_hg, lb_row, nw_row, bmask, cum_tab]
    if finish:
        in_specs.append(pl.BlockSpec((SCAN_ROWS, TOKEN_BLOCK, W_GROUP), lambda i, s: (i, blk_of(s), 0)))
        args.append(other)
    return pl.pallas_call(
        kern,
        out_shape=jax.ShapeDtypeStruct((b, lt, W_GROUP), F32),
        grid=(b // SCAN_ROWS, nb),
        in_specs=in_specs,
        out_specs=pl.BlockSpec((SCAN_ROWS, TOKEN_BLOCK, W_GROUP), lambda i, s: (i, blk_of(s), 0)),
        scratch_shapes=[pltpu.VMEM((SCAN_ROWS * N_HEADS, HEAD_DIM, HEAD_DIM), F32),
                        pltpu.VMEM((SCAN_ROWS, TOKEN_BLOCK, W_GROUP), F32)],
        compiler_params=_cparams(2), name=f"hgrn_dir{direction}",
    )(*args)


def _da_prep_kernel(p_ref, cos_ref, sin_ref, qw_ref, kw_ref, q_ref, k_ref, v_ref):
    p = p_ref[0]
    g32 = _group_ones(W_GROUP, DA_DK)
    lane = lax.broadcasted_iota(jnp.int32, (TOKEN_BLOCK, W_GROUP), 1)
    first_half = (lane % 16) < 8
    cos, sin = cos_ref[...], sin_ref[...]

    def norm_rope(x, w):
        ms = _bdot(x * x, g32) * (1.0 / DA_DK)
        y = x * lax.rsqrt(ms + EPS) * w
        partner = jnp.where(first_half, pltpu.roll(y, W_GROUP - 8, axis=1), pltpu.roll(y, 8, axis=1))
        return y * cos + partner * sin

    q_ref[0] = (norm_rope(p[:, :W_GROUP], qw_ref[...]) * (DA_DK ** -0.5)).astype(BF16)
    k_ref[0] = norm_rope(p[:, W_GROUP:2 * W_GROUP], kw_ref[...]).astype(BF16)
    dst = lax.broadcasted_iota(jnp.int32, (2 * W_GROUP, W_GROUP), 0)
    src = lax.broadcasted_iota(jnp.int32, (2 * W_GROUP, W_GROUP), 1)
    spread = (dst == (src // HEAD_DIM) * 128 + src % HEAD_DIM).astype(BF16)
    wide_t = _bdot_nt(spread, p[:, 2 * W_GROUP:])
    row_w = lax.broadcasted_iota(jnp.int32, wide_t.shape, 0)
    v_ref[0] = jnp.where(row_w % 128 == HEAD_DIM, 1.0, wide_t).astype(BF16)


def _da_prep(p_da, cos_tab, sin_tab, qw_row, kw_row):
    b, lt, width = p_da.shape
    nb = lt // TOKEN_BLOCK
    blk = lambda w=W_GROUP: pl.BlockSpec((1, TOKEN_BLOCK, w), lambda i, t: (i, t, 0))
    return pl.pallas_call(
        _da_prep_kernel,
        out_shape=[jax.ShapeDtypeStruct((b, lt, W_GROUP), BF16)] * 2
                  + [jax.ShapeDtypeStruct((b, 2 * W_GROUP, lt), BF16)],
        grid=(b, nb),
        in_specs=[pl.BlockSpec((1, TOKEN_BLOCK, width), lambda i, t: (i, t, 0)),
                  pl.BlockSpec((TOKEN_BLOCK, W_GROUP), lambda i, t: (t, 0)),
                  pl.BlockSpec((TOKEN_BLOCK, W_GROUP), lambda i, t: (t, 0)),
                  pl.BlockSpec((1, W_GROUP), lambda i, t: (0, 0)),
                  pl.BlockSpec((1, W_GROUP), lambda i, t: (0, 0))],
        out_specs=[blk(), blk(), pl.BlockSpec((1, 2 * W_GROUP, TOKEN_BLOCK), lambda i, t: (i, 0, t))],
        compiler_params=_cparams(2), name="da_prep",
    )(p_da, cos_tab, sin_tab, qw_row, kw_row)


def _da_attn_kernel(q_ref, k_ref, v_ref, lam_ref, sw_ref, o_ref, *, lam_init):
    q = q_ref[0]
    k = k_ref[0]
    lp = lam_ref[...]
    lam = (jnp.exp(jnp.sum(lp[0:1] * lp[1:2], axis=1, keepdims=True))
           - jnp.exp(jnp.sum(lp[2:3] * lp[3:4], axis=1, keepdims=True)) + lam_init)
    lane = lax.broadcasted_iota(jnp.int32, q.shape, 1)

    def scores(m):
        lo = (m // 2) * HEAD_DIM + (m % 2) * DA_DK
        qm = jnp.where(jnp.logical_and(lane >= lo, lane < lo + DA_DK), q, jnp.zeros_like(q))
        return lax.dot_general(k, qm, (((1,), (1,)), ((), ())), preferred_element_type=F32).astype(BF16)

    n_maps = 2 * N_HEADS
    maps = []
    s_next = scores(0)
    for m in range(n_maps):
        s = s_next
        if m + 1 < n_maps:
            s_next = scores(m + 1)
        e = jnp.exp(s - jnp.max(s, axis=0, keepdims=True))
        vt_h = v_ref[0, (m // 2) * 128:(m // 2 + 1) * 128, :]
        acc = jnp.dot(vt_h, e, preferred_element_type=F32)
        maps.append(acc / acc[HEAD_DIM:HEAD_DIM + 1, :])
    outs = [(maps[2 * h] - lam * maps[2 * h + 1])[:HEAD_DIM] for h in range(N_HEADS)]
    o = jnp.concatenate(outs, axis=0).T
    g64 = _group_ones(W_GROUP, HEAD_DIM)
    o_ref[0] = _group_rms(o, g64, HEAD_DIM, sw_ref[...]) * (1.0 - lam_init)


def _da_attention(qh, kh, vh, lam_p, sw_row, lam_init, q_tile, q_blocks, q_off, k_rows, k_blk, name):
    b, lt, _ = qh.shape
    kern = functools.partial(_da_attn_kernel, lam_init=lam_init)
    return pl.pallas_call(
        kern,
        out_shape=jax.ShapeDtypeStruct((b, q_blocks * q_tile, W_GROUP), F32),
        grid=(b, q_blocks),
        in_specs=[pl.BlockSpec((1, q_tile, W_GROUP), lambda i, t: (i, t + q_off, 0)),
                  pl.BlockSpec((1, k_rows, W_GROUP), lambda i, t: (i, k_blk, 0)),
                  pl.BlockSpec((1, 2 * W_GROUP, k_rows), lambda i, t: (i, 0, k_blk)),
                  pl.BlockSpec((4, DA_DK), lambda i, t: (0, 0)),
                  pl.BlockSpec((1, W_GROUP), lambda i, t: (0, 0))],
        out_specs=pl.BlockSpec((1, q_tile, W_GROUP), lambda i, t: (i, t, 0)),
        compiler_params=_cparams(2), name=name,
    )(qh, kh, vh, lam_p, sw_row)


def _hy_conv_kernel(p_ref, hp_ref, hn_ref, cw_ref, v_ref, x1_ref, x2_ref, xp_ref, *, nbl):
    blk = pl.program_id(1)
    xc = _short_conv(p_ref[0], hp_ref.at[0], hn_ref.at[0], cw_ref, xp_ref, blk, nbl, 3 * W_GROUP)
    v_ref[0] = xc[:, :W_GROUP].astype(v_ref.dtype)
    x1_ref[0] = xc[:, W_GROUP:2 * W_GROUP].astype(x1_ref.dtype)
    x2_ref[0] = xc[:, 2 * W_GROUP:].astype(x2_ref.dtype)


def _hy_short_conv(p_hy, conv_w, nbl):
    b, lt, width = p_hy.shape
    nb = nbl + 1
    blk = lambda: pl.BlockSpec((1, TOKEN_BLOCK, W_GROUP), lambda i, t: (i, t, 0))
    return pl.pallas_call(
        functools.partial(_hy_conv_kernel, nbl=nbl),
        out_shape=[jax.ShapeDtypeStruct((b, lt, W_GROUP), BF16)] * 3,
        grid=(b, nb),
        in_specs=[pl.BlockSpec((1, TOKEN_BLOCK, width), lambda i, t: (i, t, 0))]
                 + _halo_specs(width, nbl, lambda s: s, lt)
                 + [pl.BlockSpec((3, width), lambda i, t: (0, 0))],
        out_specs=[blk(), blk(), blk()],
        scratch_shapes=[pltpu.VMEM((TOKEN_BLOCK + 16, width), F32)],
        compiler_params=_cparams(2), name="hy_short_conv",
    )(p_hy, p_hy, p_hy, conv_w)


def _hy_filter_kernel(z_ref, w1_ref, b1_ref, f1_ref, w2_ref, b2_ref, f2_ref, w3_ref, dec_ref, o_ref):
    z = z_ref[...]
    h = jnp.sin(f1_ref[...] * (_hp_dot(z, w1_ref[...]) + b1_ref[...]))
    h = jnp.sin(f2_ref[...] * (_hp_dot(h, w2_ref[...]) + b2_ref[...]))
    h = _hp_dot(h, w3_ref[...])
    h = h * jnp.exp(-z[:, 0:1] * jnp.abs(dec_ref[...]))
    first = (pl.program_id(0) * h.shape[0] + lax.broadcasted_iota(jnp.int32, (h.shape[0], 1), 0)) == 0
    for plane in range(4):
        piece = h[:, plane * W_GROUP:(plane + 1) * W_GROUP]
        o_ref[plane] = jnp.where(first, 0.0, piece) if plane >= 2 else piece


def _hy_filters(zfeat, w1p, b1, f1, w2, b2, f2, w3, dec_row):
    n = zfeat.shape[0]
    tn = min(n, 512)
    full = lambda a: pl.BlockSpec(a.shape, lambda i: (0,) * a.ndim)
    args = [w1p, b1, f1, w2, b2, f2, w3, dec_row]
    return pl.pallas_call(
        _hy_filter_kernel,
        out_shape=jax.ShapeDtypeStruct((4, n, W_GROUP), F32),
        grid=(n // tn,),
        in_specs=[pl.BlockSpec((tn, 128), lambda i: (i, 0))] + [full(a) for a in args],
        out_specs=pl.BlockSpec((4, tn, W_GROUP), lambda i: (0, i, 0)),
        compiler_params=_cparams(1), name="hy_filters",
    )(zfeat, *args)


def _left_split_dot(a, v):
    a_hi = a.astype(BF16)
    a_lo = (a - a_hi.astype(F32)).astype(BF16)
    return jnp.dot(a_hi, v, preferred_element_type=F32) + jnp.dot(a_lo, v, preferred_element_type=F32)


def _fft_stage1_kernel(a_ref, u_ref, o_ref):
    u = u_ref[0]
    y = _left_split_dot(a_ref[...], u) if u.dtype == BF16 else _hp_dot(a_ref[...], u)
    o_ref[0] = y.astype(o_ref.dtype)


def _fft_stage1(a_mat, u_view, row_blk, col_tile):
    nbatch, _, cols = u_view.shape
    m, kdim = a_mat.shape
    return pl.pallas_call(
        _fft_stage1_kernel,
        out_shape=jax.ShapeDtypeStruct((nbatch, m, cols), BF16),
        grid=(nbatch, cols // col_tile),
        in_specs=[pl.BlockSpec((m, kdim), lambda i, j: (0, 0)),
                  pl.BlockSpec((1, kdim, col_tile), lambda i, j: (i, row_blk, j))],
        out_specs=pl.BlockSpec((1, m, col_tile), lambda i, j: (i, 0, j)),
        compiler_params=_cparams(2), name="fft_stage1",
    )(a_mat, u_view)


def _fft_mid_kernel(*refs, n2, multiply):
    if multiply:
        y_ref, g_ref, kf_ref, kb_ref, gi_ref, o_ref = refs
    else:
        y_ref, g_ref, o_ref = refs
    group = range(FFT_K1_GROUP)
    z = [_bdot(g_ref[j], y_ref[0, :, j].reshape(2 * n2, W_GROUP)) for j in group]
    if multiply:
        prod = []
        for j in group:
            zr, zi = z[j][:n2], z[j][n2:]
            kr = kf_ref[0, 0, j].astype(F32) + kb_ref[0, 0, j].astype(F32)
            ki = kf_ref[0, 1, j].astype(F32) - kb_ref[0, 1, j].astype(F32)
            prod.append(jnp.concatenate([zr * kr - zi * ki, zr * ki + zi * kr], axis=0))
        z = [_bdot(gi_ref[j], prod[j]) for j in group]
    for j in group:
        o_ref[0, :, j] = z[j].reshape(2, n2, W_GROUP).astype(o_ref.dtype)


def _fft_mid(y1, g_fwd, kf, filt, g_inv, k1p, n2):
    nbatch = y1.shape[0]
    multiply = kf is not None
    data = lambda plane: pl.BlockSpec((1, 2, FFT_K1_GROUP, n2, W_GROUP),
                                      (lambda k, i: (i, 0, k, 0, 0)) if plane is None
                                      else (lambda k, i: (plane, 0, k, 0, 0)))
    mat = lambda: pl.BlockSpec((FFT_K1_GROUP, 2 * n2, 2 * n2), lambda k, i: (k, 0, 0))
    in_specs, args = [data(None), mat()], [y1, g_fwd]
    if multiply:
        in_specs += [data(filt), data(2 + filt), mat()]
        args += [kf, kf, g_inv]
    return pl.pallas_call(
        functools.partial(_fft_mid_kernel, n2=n2, multiply=multiply),
        out_shape=jax.ShapeDtypeStruct((nbatch, 2, k1p, n2, W_GROUP), BF16),
        grid=(k1p // FFT_K1_GROUP, nbatch),
        in_specs=in_specs,
        out_specs=data(None),
        compiler_params=_cparams(2), name="fft_mid_mul" if multiply else "fft_mid",
    )(*args)


def _fft_stage4_kernel(a_ref, v_ref, u_ref, gate_ref, skip_ref, o_ref):
    y = _left_split_dot(a_ref[...], v_ref[0])
    o_ref[0] = (gate_ref[0].astype(F32) * (y + skip_ref[...] * u_ref[0].astype(F32))).astype(o_ref.dtype)


def _fft_stage4(a_mat, v_view, u_view, u_blk, gate_view, gate_blk, skip_row, col_tile):
    nbatch, kdim, cols = v_view.shape
    m = a_mat.shape[0]
    return pl.pallas_call(
        _fft_stage4_kernel,
        out_shape=jax.ShapeDtypeStruct((nbatch, m, cols), BF16),
        grid=(nbatch, cols // col_tile),
        in_specs=[pl.BlockSpec((m, kdim), lambda i, j: (0, 0)),
                  pl.BlockSpec((1, kdim, col_tile), lambda i, j: (i, 0, j)),
                  pl.BlockSpec((1, m, col_tile), lambda i, j: (i, u_blk, j)),
                  pl.BlockSpec((1, m, col_tile), lambda i, j: (i, gate_blk, j)),
                  pl.BlockSpec((1, col_tile), lambda i, j: (0, j))],
        out_specs=pl.BlockSpec((1, m, col_tile), lambda i, j: (i, 0, j)),
        compiler_params=_cparams(2), name="fft_stage4",
    )(a_mat, v_view, u_view, gate_view, skip_row)


def _fft_tables(n, n2):
    big = 2 * n
    n1 = big // n2
    two_pi = 2.0 * math.pi

    def cs(num, den):
        ang = (num % den).astype(F32) * (two_pi / den)
        return jnp.cos(ang), jnp.sin(ang)

    k1p = -(-(n1 // 2 + 1) // FFT_K1_GROUP) * FFT_K1_GROUP
    k1 = jnp.arange(k1p, dtype=jnp.int32)
    live = (k1 <= n1 // 2).astype(F32)
    weight = jnp.where((k1 == 0) | (k1 == n1 // 2), 1.0, 2.0) * live
    m1 = jnp.arange(n1, dtype=jnp.int32)
    c, s = cs(k1[:, None] * m1[None, :], n1)
    a1_full = jnp.concatenate([c * live[:, None], -s * live[:, None]], axis=0)
    k2 = jnp.arange(n2, dtype=jnp.int32)
    num = (k2[None, :, None] * k2[None, None, :] * n1 + k1[:, None, None] * k2[None, None, :]) % big
    c, s = cs(num, big)
    g_fwd = jnp.concatenate([jnp.concatenate([c, s], axis=2), jnp.concatenate([-s, c], axis=2)], axis=1)
    num = (k2[None, :, None] * k2[None, None, :] * n1 + k1[:, None, None] * k2[None, :, None]) % big
    c, s = cs(num, big)
    g_inv = jnp.concatenate([jnp.concatenate([c, -s], axis=2), jnp.concatenate([s, c], axis=2)], axis=1)
    t1 = jnp.arange(n1 // 2, dtype=jnp.int32)
    c, s = cs(t1[:, None] * k1[None, :], n1)
    a4 = jnp.concatenate([c * weight[None, :], -s * weight[None, :]], axis=1) * (1.0 / big)
    return n1, k1p, a1_full, g_fwd.astype(BF16), g_inv.astype(BF16), a4


def _hy_positional(n):
    pos = jnp.arange(n, dtype=F32)
    t01 = pos / max(n - 1, 1)
    bands = jnp.linspace(1e-4, HY_BANDS - 1, HY_BANDS, dtype=F32)
    ang = (2.0 * math.pi / n) * pos[:, None] * bands
    z = jnp.concatenate([t01[:, None], jnp.cos(ang), -jnp.sin(ang)], axis=-1)
    return jnp.pad(z, ((0, 0), (0, 128 - z.shape[1])))


def _hyena_segment(v_all, x1_all, x2_all, seg_start, n, n2, filt_args, skip):
    b, lt, c = v_all.shape
    n1, k1p, a1_full, g_fwd, g_inv, a4 = _fft_tables(n, n2)
    a1_half = a1_full[:, :n1 // 2]
    cols = n2 * c
    col_tile = min(cols, 8192)
    seg_blk = seg_start // n
    view = lambda a: a.reshape(b, lt // n2, cols)

    h = _hy_filters(_hy_positional(n), *filt_args)
    ky = _fft_stage1(a1_half, h.reshape(4, n1 // 2, cols), 0, col_tile)
    kf = _fft_mid(ky.reshape(4, 2, k1p, n2, c), g_fwd, None, 0, None, k1p, n2)

    def long_conv(u_view, u_blk, gate_all, filt):
        y1 = _fft_stage1(a1_half, u_view, u_blk, col_tile)
        vmid = _fft_mid(y1.reshape(b, 2, k1p, n2, c), g_fwd, kf, filt, g_inv, k1p, n2)
        skip_row = jnp.tile(skip[filt], n2).reshape(1, cols)
        return _fft_stage4(a4, vmid.reshape(b, 2 * k1p, cols), u_view, u_blk, view(gate_all), seg_blk,
                           skip_row, col_tile)

    z = long_conv(view(v_all), seg_blk, x1_all, 0)
    out = long_conv(z, 0, x2_all, 1)
    return out.reshape(b, n, c)


def _rope_tables(rows, lt):
    n_freq = DA_DK // 4
    inv = ROPE_THETA ** (-jnp.arange(n_freq, dtype=F32) / n_freq)
    r = jnp.repeat(jnp.arange(rows, dtype=F32), GRID_W)
    col = jnp.tile(jnp.arange(GRID_W, dtype=F32), rows)
    ang_r, ang_c = r[:, None] * inv, col[:, None] * inv
    cos32 = jnp.concatenate([jnp.cos(ang_r)] * 2 + [jnp.cos(ang_c)] * 2, axis=-1)
    sin32 = jnp.concatenate([-jnp.sin(ang_r), jnp.sin(ang_r), -jnp.sin(ang_c), jnp.sin(ang_c)], axis=-1)
    n = rows * GRID_W
    pad = lt - n
    cos = jnp.concatenate([jnp.tile(cos32, (1, W_GROUP // DA_DK)), jnp.ones((pad, W_GROUP), F32)], axis=0)
    sin = jnp.concatenate([jnp.tile(sin32, (1, W_GROUP // DA_DK)), jnp.zeros((pad, W_GROUP), F32)], axis=0)
    return cos, sin


def kernel(x, c, ctx, c_ctx, mod_w, mod_b, ln1_w, ln2_w, w_in, gdn_conv_w, gdn_a_log, gdn_dt_bias, gdn_norm_w, hy_conv_w, hy_w1, hy_b1, hy_f1, hy_w2, hy_b2, hy_f2, hy_w3, hy_decay, hy_bias, hg_lb_raw, hg_norm_w, da_q_norm, da_k_norm, da_lam, da_subln, w_out, mlp_w1, mlp_w2):
    batch, seq, d = x.shape
    n_ctx = ctx.shape[1]
    depth = mod_w.shape[0]
    assert n_ctx == TOKEN_BLOCK and seq % TOKEN_BLOCK == 0 and d == 4 * W_GROUP
    nbl = seq // TOKEN_BLOCK
    lt = seq + n_ctx

    xs = jnp.concatenate([x, ctx], axis=1)
    cond = jnp.concatenate([c, c_ctx[None, :], jnp.zeros((-(batch + 1) % 8, d), F32)], axis=0)
    mods = _modulation(cond, mod_w, mod_b)[:, :batch + 1].reshape(depth, batch + 1, 6, d)
    mods = jnp.pad(mods, ((0, 0), (0, 0), (0, 2), (0, 0)))

    lb_all = jnp.cumsum(jax.nn.softmax(hg_lb_raw.astype(F32), axis=0), axis=0)
    lb_all = lb_all - lb_all[0]
    cos_tab, sin_tab = _rope_tables(seq // GRID_W, lt)

    gdn_cols = 4 * W_GROUP + 4 * N_HEADS
    hy_cols, hg_cols, da_cols = 3 * W_GROUP, 5 * W_GROUP, 3 * W_GROUP
    splits = (GDN_PAD_COLS, hy_cols, hg_cols, da_cols)

    for l in range(depth):
        want_ctx = l < depth - 1
        modtab = mods[l]
        wl = w_in[l]
        w_pad = jnp.concatenate([wl[:, :gdn_cols], jnp.zeros((d, GDN_PAD_COLS - gdn_cols), F32),
                                 wl[:, gdn_cols:]], axis=1).astype(BF16)
        p_gdn, p_hy, p_hg, p_da = _adaln_proj(xs, modtab, ln1_w[l], w_pad, splits, (F32,) * 4, 0, 1, False,
                                              nbl, "in_proj")

        pad16 = lambda a: jnp.pad(a.reshape(1, 2 * N_HEADS), ((0, 0), (0, 128 - 2 * N_HEADS)))
        alog_row, dtb_row = pad16(gdn_a_log[l]), pad16(gdn_dt_bias[l])
        gnw = jnp.tile(gdn_norm_w[l], N_HEADS).reshape(1, W_GROUP)
        o_b = _gdn_direction(p_gdn, gdn_conv_w[l], alog_row, dtb_row, gnw, None, 1, nbl)
        o_gdn = _gdn_direction(p_gdn, gdn_conv_w[l], alog_row, dtb_row, gnw, o_b, 0, nbl)

        hv, hx1, hx2 = _hy_short_conv(p_hy, hy_conv_w[l], nbl)
        w1p = jnp.pad(hy_w1[l], ((0, 128 - hy_w1.shape[1]), (0, 0)))
        row = lambda a: a.reshape(1, -1)
        dec_row = jnp.tile(hy_decay[l].reshape(1, 2 * W_GROUP), (1, 2))
        filt_args = (w1p, row(hy_b1[l]), row(hy_f1[l]), hy_w2[l], row(hy_b2[l]), row(hy_f2[l]), hy_w3[l], dec_row)
        o_hy_lat = _hyena_segment(hv, hx1, hx2, 0, seq, FFT_N2_LAT, filt_args, hy_bias[l])
        o_hy_ctx = (_hyena_segment(hv, hx1, hx2, seq, n_ctx, FFT_N2_CTX, filt_args, hy_bias[l])
                    if want_ctx else None)

        lb_row = lb_all[l].reshape(1, W_GROUP)
        hnw = jnp.tile(hg_norm_w[l], N_HEADS).reshape(1, W_GROUP)
        o_hb = _hgrn_direction(p_hg, lb_row, hnw, None, 1, nbl)
        o_hg = _hgrn_direction(p_hg, lb_row, hnw, o_hb, 0, nbl)

        lam_init = 0.8 - 0.6 * math.exp(-0.3 * l)
        qw = jnp.tile(da_q_norm[l], W_GROUP // DA_DK).reshape(1, W_GROUP)
        kw = jnp.tile(da_k_norm[l], W_GROUP // DA_DK).reshape(1, W_GROUP)
        sw = jnp.tile(da_subln[l], N_HEADS).reshape(1, W_GROUP)
        qh, kh, vh = _da_prep(p_da, cos_tab, sin_tab, qw, kw)
        o_da = _da_attention(qh, kh, vh, da_lam[l], sw, lam_init, DA_Q_TILE, seq // DA_Q_TILE, 0, lt, 0,
                             "da_attn_lat")
        o_da_ctx = (_da_attention(qh, kh, vh, da_lam[l], sw, lam_init, TOKEN_BLOCK, 1, nbl, TOKEN_BLOCK, nbl,
                                  "da_attn_ctx") if want_ctx else None)

        n_blocks = nbl + 1 if want_ctx else nbl
        xs_mid = _proj_residual(xs, modtab, w_out[l].astype(BF16),
                                [o_gdn, (o_hy_lat, o_hy_ctx), o_hg, (o_da, o_da_ctx)], 2, n_blocks, nbl, "out_proj")
        (hid,) = _adaln_proj(xs_mid, modtab, ln2_w[l], mlp_w1[l].astype(BF16), (mlp_w1.shape[2],), (BF16,),
                             3, 4, True, nbl, "mlp_up")
        xs = _proj_residual(xs_mid, modtab, mlp_w2[l].astype(BF16), [hid], 5, n_blocks, nbl, "mlp_down")
    return xs
```

```python
import functools
import math

import jax
import jax.numpy as jnp
from jax import lax
from jax.experimental import pallas as pl
from jax.experimental.pallas import tpu as pltpu

F32 = jnp.float32
BF16 = jnp.bfloat16

EPS = 1e-6
TOKEN_BLOCK = 256
CHUNK = 64
HEAD_DIM = 64
N_HEADS = 4
W_GROUP = 256
DA_DK = 32
GRID_W = 64
ROPE_THETA = 10000.0
HY_BANDS = 16
FFT_N2_LAT = 128
FFT_N2_CTX = 16
FFT_K1_GROUP = 8
DA_Q_TILE = 256
SCAN_ROWS = 4
GDN_PAD_COLS = 1152
VMEM_LIMIT_BYTES = 56 * 1024 * 1024


def _cparams(n_axes):
    return pltpu.CompilerParams(dimension_semantics=("arbitrary",) * n_axes,
                                vmem_limit_bytes=VMEM_LIMIT_BYTES)


def _sigmoid(x):
    return 1.0 / (1.0 + jnp.exp(-x))


def _silu(x):
    return x * _sigmoid(x)


def _softplus(x):
    return jnp.maximum(x, 0.0) + jnp.log(1.0 + jnp.exp(-jnp.abs(x)))


def _log_sigmoid(x):
    return jnp.minimum(x, 0.0) - jnp.log(1.0 + jnp.exp(-jnp.abs(x)))


def _bdot(a, b):
    return jnp.dot(a.astype(BF16), b.astype(BF16), preferred_element_type=F32)


def _bdot_nt(a, b):
    return lax.dot_general(a.astype(BF16), b.astype(BF16), (((1,), (1,)), ((), ())),
                           preferred_element_type=F32)


def _split3(x):
    hi = x.astype(BF16)
    r = x - hi.astype(F32)
    mid = r.astype(BF16)
    lo = (r - mid.astype(F32)).astype(BF16)
    return hi, mid, lo


def _exact_left(m_bf16, x):
    hi, mid, lo = _split3(x)
    d = lambda p: jnp.dot(m_bf16, p, preferred_element_type=F32)
    return d(hi) + d(mid) + d(lo)


def _hp_dot(a, b):
    ah = a.astype(BF16)
    al = (a - ah.astype(F32)).astype(BF16)
    bh = b.astype(BF16)
    bl = (b - bh.astype(F32)).astype(BF16)
    d = lambda p, q: jnp.dot(p, q, preferred_element_type=F32)
    return d(ah, bh) + d(ah, bl) + d(al, bh)


def _group_ones(width, group):
    r = lax.broadcasted_iota(jnp.int32, (width, width), 0) // group
    c = lax.broadcasted_iota(jnp.int32, (width, width), 1) // group
    return (r == c).astype(BF16)


def _mod_kernel(s_ref, w_ref, b_ref, o_ref):
    s = _silu(s_ref[...])
    o_ref[0] = _bdot(s, w_ref[0]) + b_ref[0]


def _modulation(cond, mod_w, mod_b):
    depth, d, n = mod_w.shape
    rows = cond.shape[0]
    tn = 1536
    return pl.pallas_call(
        _mod_kernel,
        out_shape=jax.ShapeDtypeStruct((depth, rows, n), F32),
        grid=(depth, n // tn),
        in_specs=[pl.BlockSpec((rows, d), lambda l, j: (0, 0)),
                  pl.BlockSpec((1, d, tn), lambda l, j: (l, 0, j)),
                  pl.BlockSpec((1, 1, tn), lambda l, j: (l, 0, j))],
        out_specs=pl.BlockSpec((1, rows, tn), lambda l, j: (l, 0, j)),
        compiler_params=_cparams(2), name="modulation",
    )(cond, mod_w, mod_b.reshape(depth, 1, n))


def _adaln_proj_kernel(x_ref, mod_ref, lnw_ref, w_ref, *out_refs, shift_row, scale_row, splits, sq_relu):
    x = x_ref[0]
    ms = jnp.mean(x * x, axis=-1, keepdims=True)
    y = x * lax.rsqrt(ms + EPS) * lnw_ref[...]
    mod = mod_ref[0]
    y = y * (1.0 + mod[scale_row:scale_row + 1]) + mod[shift_row:shift_row + 1]
    h = _bdot(y, w_ref[...])
    if sq_relu:
        h = jnp.square(jnp.maximum(h, 0.0))
    off = 0
    for o_ref, n in zip(out_refs, splits):
        o_ref[0] = h[:, off:off + n].astype(o_ref.dtype)
        off += n


def _mod_index(nbl, batch):
    return lambda b, t: (jnp.where(t == nbl, batch, b), 0, 0)


def _adaln_proj(xs, modtab, ln_w, w, splits, out_dtypes, shift_row, scale_row, sq_relu, nbl, name):
    b, lt, d = xs.shape
    nb = lt // TOKEN_BLOCK
    n = w.shape[1]
    kern = functools.partial(_adaln_proj_kernel, shift_row=shift_row, scale_row=scale_row,
                             splits=splits, sq_relu=sq_relu)
    return pl.pallas_call(
        kern,
        out_shape=[jax.ShapeDtypeStruct((b, lt, s), dt) for s, dt in zip(splits, out_dtypes)],
        grid=(b, nb),
        in_specs=[pl.BlockSpec((1, TOKEN_BLOCK, d), lambda i, t: (i, t, 0)),
                  pl.BlockSpec((1, 8, d), _mod_index(nbl, b)),
                  pl.BlockSpec((1, d), lambda i, t: (0, 0)),
                  pl.BlockSpec((d, n), lambda i, t: (0, 0))],
        out_specs=[pl.BlockSpec((1, TOKEN_BLOCK, s), lambda i, t: (i, t, 0)) for s in splits],
        compiler_params=_cparams(2), name=name,
    )(xs, modtab, ln_w.reshape(1, d), w)


def _proj_residual_kernel(x_ref, mod_ref, w_ref, *rest, gate_row, split, nbl):
    a_refs, o_ref = rest[:-1], rest[-1]
    is_ctx = pl.program_id(1) == nbl
    pieces, pos = [], 0
    for two in split:
        lat = a_refs[pos][0].astype(BF16)
        pieces.append(jnp.where(is_ctx, a_refs[pos + 1][0].astype(BF16), lat) if two else lat)
        pos += 2 if two else 1
    a = jnp.concatenate(pieces, axis=1) if len(pieces) > 1 else pieces[0]
    gate = mod_ref[0][gate_row:gate_row + 1]
    o_ref[0] = x_ref[0] + gate * _bdot(a, w_ref[...])


def _proj_residual(xs, modtab, w, acts, gate_row, n_blocks, nbl, name):
    b, lt, d = xs.shape
    split = tuple(isinstance(a, tuple) and a[1] is not None for a in acts)
    specs, args = [], []
    for a, two in zip(acts, split):
        lat = a[0] if isinstance(a, tuple) else a
        width = lat.shape[2]
        if isinstance(a, tuple):
            specs.append(pl.BlockSpec((1, TOKEN_BLOCK, width), lambda i, t: (i, jnp.minimum(t, nbl - 1), 0)))
        else:
            specs.append(pl.BlockSpec((1, TOKEN_BLOCK, width), lambda i, t: (i, t, 0)))
        args.append(lat)
        if two:
            specs.append(pl.BlockSpec((1, TOKEN_BLOCK, width), lambda i, t: (i, 0, 0)))
            args.append(a[1])
    kern = functools.partial(_proj_residual_kernel, gate_row=gate_row, split=split, nbl=nbl)
    return pl.pallas_call(
        kern,
        out_shape=jax.ShapeDtypeStruct((b, n_blocks * TOKEN_BLOCK, d), F32),
        grid=(b, n_blocks),
        in_specs=[pl.BlockSpec((1, TOKEN_BLOCK, d), lambda i, t: (i, t, 0)),
                  pl.BlockSpec((1, 8, d), _mod_index(nbl, b)),
                  pl.BlockSpec(w.shape, lambda i, t: (0, 0))] + specs,
        out_specs=pl.BlockSpec((1, TOKEN_BLOCK, d), lambda i, t: (i, t, 0)),
        compiler_params=_cparams(2), name=name,
    )(xs, modtab, w, *args)


def _scan_block(step, nbl, direction):
    lat = step - 1 if direction == 0 else nbl - step
    return jnp.where(step == 0, nbl, lat)


def _halo_specs(width, nbl, block_of_step, lt, batch_rows=1):
    rows8 = TOKEN_BLOCK // 8

    def prev_map(b, s):
        return (b, jnp.maximum(block_of_step(s) * rows8 - 1, 0), 0)

    def next_map(b, s):
        return (b, jnp.minimum((block_of_step(s) + 1) * rows8, lt // 8 - 1), 0)

    return [pl.BlockSpec((batch_rows, 8, width), prev_map), pl.BlockSpec((batch_rows, 8, width), next_map)]


def _short_conv(cur, hp_ref, hn_ref, cw_ref, xp_ref, blk, nbl, width):
    seg_first = jnp.logical_or(blk == 0, blk == nbl)
    seg_last = jnp.logical_or(blk == nbl - 1, blk == nbl)
    xp_ref[0:8, :] = jnp.where(seg_first, 0.0, hp_ref[:, :width])
    xp_ref[8:8 + TOKEN_BLOCK, :] = cur
    xp_ref[8 + TOKEN_BLOCK:16 + TOKEN_BLOCK, :] = jnp.where(seg_last, 0.0, hn_ref[:, :width])
    cw = cw_ref[...]
    return (cw[0:1] * xp_ref[7:7 + TOKEN_BLOCK, :] + cw[1:2] * xp_ref[8:8 + TOKEN_BLOCK, :]
            + cw[2:3] * xp_ref[9:9 + TOKEN_BLOCK, :])


def _scan_tables(direction, chunk):
    i = jnp.arange(TOKEN_BLOCK, dtype=jnp.int32)[:, None]
    j = jnp.arange(TOKEN_BLOCK, dtype=jnp.int32)[None, :]
    same = (i // chunk) == (j // chunk)
    pos = (lambda t: t % chunk) if direction == 0 else (lambda t: chunk - 1 - t % chunk)
    pi, pj = pos(i), pos(j)
    incl = same & (pj <= pi)
    tables = [incl, (i // HEAD_DIM) == (j // HEAD_DIM), i == j]
    m = 1
    while m < chunk:
        boundary = (pi // (2 * m)) * (2 * m) + m - 1
        tables.append(incl.astype(F32) - (same & (pj <= boundary)).astype(F32))
        m *= 2
    return jnp.stack([a.astype(BF16) for a in tables])


def _group_rms(o, gmat, group, weight):
    ms = jnp.dot((o * o).astype(BF16), gmat, preferred_element_type=F32) * (1.0 / group)
    return o * lax.rsqrt(ms + EPS) * weight


def _gdn_kernel(*refs, direction, finish, nbl):
    if finish:
        (p_ref, hp_ref, hn_ref, cw_ref, alog_ref, dtb_ref, nw_ref, bm_ref, other_ref,
         o_ref, s_ref, xp_ref, ob_ref) = refs
    else:
        p_ref, hp_ref, hn_ref, cw_ref, alog_ref, dtb_ref, nw_ref, bm_ref, o_ref, s_ref, xp_ref, ob_ref = refs
    step = pl.program_id(1)
    blk = _scan_block(step, nbl, direction)

    @pl.when(step == 0)
    def _():
        s_ref[...] = jnp.zeros_like(s_ref)

    g64 = bm_ref[1]
    rows = range(SCAN_ROWS)
    p_rows, q_rows, k_rows, v_rows, kt_rows, zc_rows, zr_rows = [], [], [], [], [], [], []
    for r in rows:
        p = p_ref[r]
        xc = _silu(_short_conv(p[:, :3 * W_GROUP], hp_ref.at[r], hn_ref.at[r], cw_ref, xp_ref.at[r], blk, nbl,
                               3 * W_GROUP))
        q = xc[:, :W_GROUP]
        k = xc[:, W_GROUP:2 * W_GROUP]
        q = q * lax.rsqrt(_bdot(q * q, g64) + EPS) * (HEAD_DIM ** -0.5)
        k = k * lax.rsqrt(_bdot(k * k, g64) + EPS)
        ab = p[:, 4 * W_GROUP:4 * W_GROUP + 128]
        lane = lax.broadcasted_iota(jnp.int32, ab.shape, 1)
        g_all = -jnp.exp(alog_ref[...]) * _softplus(ab + dtb_ref[...])
        w_all = jnp.where(lane < 2 * N_HEADS, g_all, _sigmoid(ab))
        z_col = jnp.where(lane < 2 * N_HEADS, _exact_left(bm_ref[0], w_all), w_all)
        p_rows.append(p)
        q_rows.append(q)
        k_rows.append(k)
        v_rows.append(xc[:, 2 * W_GROUP:])
        kt_rows.append(_bdot_nt(bm_ref[2], k))
        zc_rows.append(z_col)
        zr_rows.append(z_col.T)

    n_chunks = TOKEN_BLOCK // CHUNK
    problems = [(r, h) for r in rows for h in range(N_HEADS)]
    heads = range(len(problems))
    lanes = [slice(h * HEAD_DIM, (h + 1) * HEAD_DIM) for _, h in problems]
    gidx = [N_HEADS * direction + h for _, h in problems]
    q = [q_rows[r] for r, _ in problems]
    k = [k_rows[r] for r, _ in problems]
    v = [v_rows[r] for r, _ in problems]
    k_t = [kt_rows[r] for r, _ in problems]
    z_col = [zc_rows[r] for r, _ in problems]
    z_row = [zr_rows[r] for r, _ in problems]
    ci_ = lax.broadcasted_iota(jnp.int32, (CHUNK, CHUNK), 0)
    cj_ = lax.broadcasted_iota(jnp.int32, (CHUNK, CHUNK), 1)
    pi_, pj_ = (ci_, cj_) if direction == 0 else (CHUNK - 1 - ci_, CHUNK - 1 - cj_)
    incl = pj_ <= pi_
    strict_f = (pj_ < pi_).astype(F32)
    eye_f = (ci_ == cj_).astype(F32)

    def half_f(m):
        return (((pi_ // (2 * m)) == (pj_ // (2 * m))) & ((pi_ // m) % 2 == 1) & ((pj_ // m) % 2 == 0)).astype(F32)

    chunked = lambda a: a.reshape(n_chunks, CHUNK, a.shape[1])
    bmm = lambda a, b: jnp.einsum("cij,cjk->cik", a.astype(BF16), b.astype(BF16), preferred_element_type=F32)
    bmm_nt = lambda a, b: jnp.einsum("cid,cjd->cij", a.astype(BF16), b.astype(BF16), preferred_element_type=F32)
    gc_c = [chunked(z_col[i][:, gi:gi + 1]) for i, gi in enumerate(gidx)]
    gc_r = [jnp.stack([z_row[i][gi:gi + 1, c * CHUNK:(c + 1) * CHUNK] for c in range(n_chunks)])
            for i, gi in enumerate(gidx)]
    beta = [chunked(z_col[i][:, 2 * N_HEADS + gi:2 * N_HEADS + gi + 1]) for i, gi in enumerate(gidx)]
    qc = [chunked(q[i][:, ls]) for i, ls in enumerate(lanes)]
    kc = [chunked(k[i][:, ls]) for i, ls in enumerate(lanes)]
    vc = [chunked(v[i][:, ls]) for i, ls in enumerate(lanes)]
    eg = [jnp.exp(g) for g in gc_c]
    decay = [jnp.exp(jnp.where(incl, gc_c[h] - gc_r[h], -jnp.inf)) for h in heads]
    kk = [bmm_nt(kc[h], kc[h]) for h in heads]
    qk = [bmm_nt(qc[h], kc[h]) for h in heads]
    n = [kk[h] * beta[h] * decay[h] * strict_f for h in heads]
    inv = [eye_f - n[h] * half_f(1) for h in heads]
    m = 2
    while m < CHUNK:
        half = half_f(m)
        x = [bmm(inv[h], n[h] * half) for h in heads]
        inv = [inv[h] - bmm(x[h], inv[h]) for h in heads]
        m *= 2
    uw = [bmm(inv[h], jnp.concatenate([vc[h] * beta[h], kc[h] * (beta[h] * eg[h])], axis=2)) for h in heads]
    attn = [qk[h] * decay[h] for h in heads]
    qg = [qc[h] * eg[h] for h in heads]
    state = [s_ref[h] for h in heads]

    last = CHUNK - 1 if direction == 0 else 0
    for ci in range(n_chunks):
        c = ci if direction == 0 else n_chunks - 1 - ci
        rs = slice(c * CHUNK, (c + 1) * CHUNK)
        g_last = [gc_c[h][c, last:last + 1, :] for h in heads]
        v_new = [uw[h][c, :, :HEAD_DIM] - _bdot(uw[h][c, :, HEAD_DIM:], state[h]) for h in heads]
        for h, ls in enumerate(lanes):
            ob_ref[problems[h][0], rs, ls] = _bdot(qg[h][c], state[h]) + _bdot(attn[h][c], v_new[h])
        state = [state[h] * jnp.exp(g_last[h])
                 + _bdot(k_t[h][ls, rs] * jnp.exp(g_last[h] - gc_r[h][c]), v_new[h]) for h, ls in enumerate(lanes)]
    for h in heads:
        s_ref[h] = state[h]

    for r in rows:
        if finish:
            o = ob_ref[r] + other_ref[r]
            gate = _silu(p_rows[r][:, 3 * W_GROUP:4 * W_GROUP])
            o_ref[r] = _group_rms(o, g64, HEAD_DIM, nw_ref[...]) * gate
        else:
            o_ref[r] = ob_ref[r]


def _gdn_direction(p_gdn, conv_w, alog_row, dtb_row, nw_row, other, direction, nbl):
    b, lt, width = p_gdn.shape
    nb = nbl + 1
    blk_of = lambda s: _scan_block(s, nbl, direction)
    finish = other is not None
    bmask = _scan_tables(direction, CHUNK)[:3]
    assert b % SCAN_ROWS == 0
    kern = functools.partial(_gdn_kernel, direction=direction, finish=finish, nbl=nbl)
    in_specs = ([pl.BlockSpec((SCAN_ROWS, TOKEN_BLOCK, width), lambda i, s: (i, blk_of(s), 0))]
                + _halo_specs(3 * W_GROUP, nbl, blk_of, lt, SCAN_ROWS)
                + [pl.BlockSpec((3, 3 * W_GROUP), lambda i, s: (0, 0)),
                   pl.BlockSpec((1, 128), lambda i, s: (0, 0)),
                   pl.BlockSpec((1, 128), lambda i, s: (0, 0)),
                   pl.BlockSpec((1, W_GROUP), lambda i, s: (0, 0)),
                   pl.BlockSpec(bmask.shape, lambda i, s: (0, 0, 0))])
    args = [p_gdn, p_gdn, p_gdn, conv_w, alog_row, dtb_row, nw_row, bmask]
    if finish:
        in_specs.append(pl.BlockSpec((SCAN_ROWS, TOKEN_BLOCK, W_GROUP), lambda i, s: (i, blk_of(s), 0)))
        args.append(other)
    return pl.pallas_call(
        kern,
        out_shape=jax.ShapeDtypeStruct((b, lt, W_GROUP), F32),
        grid=(b // SCAN_ROWS, nb),
        in_specs=in_specs,
        out_specs=pl.BlockSpec((SCAN_ROWS, TOKEN_BLOCK, W_GROUP), lambda i, s: (i, blk_of(s), 0)),
        scratch_shapes=[pltpu.VMEM((SCAN_ROWS * N_HEADS, HEAD_DIM, HEAD_DIM), F32),
                        pltpu.VMEM((SCAN_ROWS, TOKEN_BLOCK + 16, 3 * W_GROUP), F32),
                        pltpu.VMEM((SCAN_ROWS, TOKEN_BLOCK, W_GROUP), F32)],
        compiler_params=_cparams(2), name=f"gdn_dir{direction}",
    )(*args)


def _hgrn_kernel(*refs, direction, finish, nbl):
    if finish:
        p_ref, lb_ref, nw_ref, bm_ref, cum_ref, other_ref, o_ref, s_ref, ob_ref = refs
    else:
        p_ref, lb_ref, nw_ref, bm_ref, cum_ref, o_ref, s_ref, ob_ref = refs
    step = pl.program_id(1)

    @pl.when(step == 0)
    def _():
        s_ref[...] = jnp.zeros_like(s_ref)

    lb = lb_ref[...]
    n_levels = cum_ref.shape[0] // TOKEN_BLOCK - 1
    rows = range(SCAN_ROWS)
    p_rows, q_rows, key_rows, v_rows, vt_rows, gcum_rows, qg_rows, lev_rows = [], [], [], [], [], [], [], []
    for r in rows:
        p = p_ref[r]
        q = _silu(p[:, :W_GROUP])
        v = p[:, W_GROUP:2 * W_GROUP]
        fl = p[:, (2 + direction) * W_GROUP:(3 + direction) * W_GROUP]
        a = jnp.log(lb)
        bb = jnp.log(1.0 - lb) + _log_sigmoid(fl)
        logf = jnp.maximum(a, bb) + jnp.log(1.0 + jnp.exp(-jnp.abs(a - bb)))
        logf_hi = logf.astype(BF16)
        logf_lo = (logf - logf_hi.astype(F32)).astype(BF16)
        cums = (jnp.dot(cum_ref[...], logf_hi, preferred_element_type=F32)
                + jnp.dot(cum_ref[...], logf_lo, preferred_element_type=F32))
        gcum = cums[:TOKEN_BLOCK]
        p_rows.append(p)
        q_rows.append(q)
        key_rows.append((1.0 - lb) * _sigmoid(-fl))
        v_rows.append(v)
        vt_rows.append(_bdot_nt(bm_ref[2], v))
        gcum_rows.append(gcum)
        qg_rows.append(q * jnp.exp(gcum))
        lev_rows.append([jnp.exp(-jnp.abs(cums[(lev + 1) * TOKEN_BLOCK:(lev + 2) * TOKEN_BLOCK]))
                         for lev in range(n_levels)])

    n_chunks = TOKEN_BLOCK // CHUNK
    problems = [(r, h) for r in rows for h in range(N_HEADS)]
    heads = range(len(problems))
    lanes = [slice(h * HEAD_DIM, (h + 1) * HEAD_DIM) for _, h in problems]
    ci_ = lax.broadcasted_iota(jnp.int32, (CHUNK, CHUNK), 0)
    cj_ = lax.broadcasted_iota(jnp.int32, (CHUNK, CHUNK), 1)
    pi_, pj_ = (ci_, cj_) if direction == 0 else (CHUNK - 1 - ci_, CHUNK - 1 - cj_)
    chunked = lambda a: a.reshape(n_chunks, CHUNK, a.shape[1])
    bmm = lambda a, b: jnp.einsum("cij,cjk->cik", a.astype(BF16), b.astype(BF16), preferred_element_type=F32)
    bmm_nt = lambda a, b: jnp.einsum("cid,cjd->cij", a.astype(BF16), b.astype(BF16), preferred_element_type=F32)
    qc = [chunked(q_rows[r][:, ls]) for (r, _), ls in zip(problems, lanes)]
    kc = [chunked(key_rows[r][:, ls]) for (r, _), ls in zip(problems, lanes)]
    amat = [bmm_nt(qc[h], kc[h]) * (ci_ == cj_).astype(F32) for h in heads]
    for lev in range(n_levels):
        m = 1 << lev
        half = (((pi_ // (2 * m)) == (pj_ // (2 * m))) & ((pi_ // m) % 2 == 1) & ((pj_ // m) % 2 == 0)).astype(F32)
        e = [chunked(lev_rows[r][lev][:, ls]) for (r, _), ls in zip(problems, lanes)]
        amat = [amat[h] + half * bmm_nt(qc[h] * e[h], kc[h] * e[h]) for h in heads]
    intra = [bmm(amat[h], chunked(v_rows[r][:, ls])) for h, ((r, _), ls) in enumerate(zip(problems, lanes))]
    state = [s_ref[h] for h in heads]

    last = CHUNK - 1 if direction == 0 else 0
    for ci in range(n_chunks):
        c = ci if direction == 0 else n_chunks - 1 - ci
        rs = slice(c * CHUNK, (c + 1) * CHUNK)
        g_end = [gcum_rows[r][c * CHUNK + last:c * CHUNK + last + 1, :] for r in rows]
        kdec = [key_rows[r][rs, :] * jnp.exp(g_end[r] - gcum_rows[r][rs, :]) for r in rows]
        decay_end = [jnp.exp(g_end[r]) for r in rows]
        for h, ((r, _), ls) in enumerate(zip(problems, lanes)):
            ob_ref[r, rs, ls] = _bdot_nt(qg_rows[r][rs, ls], state[h]) + intra[h][c]
        state = [state[h] * decay_end[r][:, ls] + _bdot(vt_rows[r][ls, rs], kdec[r][:, ls])
                 for h, ((r, _), ls) in enumerate(zip(problems, lanes))]
    for h in heads:
        s_ref[h] = state[h]

    for r in rows:
        if finish:
            o = ob_ref[r] + other_ref[r]
            gate = _silu(p_rows[r][:, 4 * W_GROUP:5 * W_GROUP])
            o_ref[r] = _group_rms(o, bm_ref[1], HEAD_DIM, nw_ref[...]) * gate
        else:
            o_ref[r] = ob_ref[r]


def _hgrn_direction(p_hg, lb_row, nw_row, other, direction, nbl):
    b, lt, width = p_hg.shape
    nb = nbl + 1
    blk_of = lambda s: _scan_block(s, nbl, direction)
    finish = other is not None
    tables = _scan_tables(direction, CHUNK)
    bmask = tables[:3]
    cum_tab = jnp.concatenate([tables[0:1], tables[3:]], axis=0).reshape(-1, TOKEN_BLOCK)
    assert b % SCAN_ROWS == 0
    kern = functools.partial(_hgrn_kernel, direction=direction, finish=finish, nbl=nbl)
    in_specs = [pl.BlockSpec((SCAN_ROWS, TOKEN_BLOCK, width), lambda i, s: (i, blk_of(s), 0)),
                pl.BlockSpec((1, W_GROUP), lambda i, s: (0, 0)),
                pl.BlockSpec((1, W_GROUP), lambda i, s: (0, 0)),
                pl.BlockSpec(bmask.shape, lambda i, s: (0, 0, 0)),
                pl.BlockSpec(cum_tab.shape, lambda i, s: (0, 0))]
    args = [p_hg, lb_row, nw_row, bmask, cum_tab]
    if finish:
        in_specs.append(pl.BlockSpec((SCAN_ROWS, TOKEN_BLOCK, W_GROUP), lambda i, s: (i, blk_of(s), 0)))
        args.append(other)
    return pl.pallas_call(
        kern,
        out_shape=jax.ShapeDtypeStruct((b, lt, W_GROUP), F32),
        grid=(b // SCAN_ROWS, nb),
        in_specs=in_specs,
        out_specs=pl.BlockSpec((SCAN_ROWS, TOKEN_BLOCK, W_GROUP), lambda i, s: (i, blk_of(s), 0)),
        scratch_shapes=[pltpu.VMEM((SCAN_ROWS * N_HEADS, HEAD_DIM, HEAD_DIM), F32),
                        pltpu.VMEM((SCAN_ROWS, TOKEN_BLOCK, W_GROUP), F32)],
        compiler_params=_cparams(2), name=f"hgrn_dir{direction}",
    )(*args)


def _da_prep(p, cos, sin, qw, kw):
    g32 = _group_ones(W_GROUP, DA_DK)
    lane = lax.broadcasted_iota(jnp.int32, (TOKEN_BLOCK, W_GROUP), 1)
    first_half = (lane % 16) < 8

    def norm_rope(x, w):
        ms = _bdot(x * x, g32) * (1.0 / DA_DK)
        y = x * lax.rsqrt(ms + EPS) * w
        partner = jnp.where(first_half, pltpu.roll(y, W_GROUP - 8, axis=1), pltpu.roll(y, 8, axis=1))
        return y * cos + partner * sin

    q = (norm_rope(p[:, :W_GROUP], qw) * (DA_DK ** -0.5)).astype(BF16)
    k = norm_rope(p[:, W_GROUP:2 * W_GROUP], kw).astype(BF16)
    dst = lax.broadcasted_iota(jnp.int32, (2 * W_GROUP, W_GROUP), 0)
    src = lax.broadcasted_iota(jnp.int32, (2 * W_GROUP, W_GROUP), 1)
    spread = (dst == (src // HEAD_DIM) * 128 + src % HEAD_DIM).astype(BF16)
    wide_t = _bdot_nt(spread, p[:, 2 * W_GROUP:])
    row_w = lax.broadcasted_iota(jnp.int32, wide_t.shape, 0)
    return q, k, jnp.where(row_w % 128 == HEAD_DIM, 1.0, wide_t).astype(BF16)


def _in_proj_kernel(x_ref, mod_ref, lnw_ref, w_ref, cos_ref, sin_ref, qw_ref, kw_ref,
                    gdn_ref, hy_ref, hg_ref, q_ref, k_ref, vt_ref, *, splits):
    x = x_ref[0]
    ms = jnp.mean(x * x, axis=-1, keepdims=True)
    y = x * lax.rsqrt(ms + EPS) * lnw_ref[...]
    mod = mod_ref[0]
    y = y * (1.0 + mod[1:2]) + mod[0:1]
    h = _bdot(y, w_ref[...])
    off = 0
    for o_ref, n in zip((gdn_ref, hy_ref, hg_ref), splits[:3]):
        o_ref[0] = h[:, off:off + n]
        off += n
    q, k, vt = _da_prep(h[:, off:], cos_ref[...], sin_ref[...], qw_ref[...], kw_ref[...])
    q_ref[0] = q
    k_ref[0] = k
    vt_ref[0] = vt


def _in_proj(xs, modtab, ln_w, w, splits, cos_tab, sin_tab, qw_row, kw_row, nbl):
    b, lt, d = xs.shape
    nb = lt // TOKEN_BLOCK
    n = w.shape[1]
    tok = lambda width: pl.BlockSpec((1, TOKEN_BLOCK, width), lambda i, t: (i, t, 0))
    row = lambda width: pl.BlockSpec((1, width), lambda i, t: (0, 0))
    tab = lambda: pl.BlockSpec((TOKEN_BLOCK, W_GROUP), lambda i, t: (t, 0))
    return pl.pallas_call(
        functools.partial(_in_proj_kernel, splits=splits),
        out_shape=[jax.ShapeDtypeStruct((b, lt, s), F32) for s in splits[:3]]
                  + [jax.ShapeDtypeStruct((b, lt, W_GROUP), BF16)] * 2
                  + [jax.ShapeDtypeStruct((b, 2 * W_GROUP, lt), BF16)],
        grid=(b, nb),
        in_specs=[tok(d), pl.BlockSpec((1, 8, d), _mod_index(nbl, b)), row(d),
                  pl.BlockSpec((d, n), lambda i, t: (0, 0)), tab(), tab(), row(W_GROUP), row(W_GROUP)],
        out_specs=[tok(s) for s in splits[:3]] + [tok(W_GROUP), tok(W_GROUP),
                   pl.BlockSpec((1, 2 * W_GROUP, TOKEN_BLOCK), lambda i, t: (i, 0, t))],
        compiler_params=_cparams(2), name="in_proj",
    )(xs, modtab, ln_w.reshape(1, d), w, cos_tab, sin_tab, qw_row, kw_row)


def _da_attn_kernel(q_ref, k_ref, v_ref, lam_ref, sw_ref, o_ref, *, lam_init):
    q = q_ref[0]
    k = k_ref[0]
    lp = lam_ref[...]
    lam = (jnp.exp(jnp.sum(lp[0:1] * lp[1:2], axis=1, keepdims=True))
           - jnp.exp(jnp.sum(lp[2:3] * lp[3:4], axis=1, keepdims=True)) + lam_init)
    lane = lax.broadcasted_iota(jnp.int32, q.shape, 1)

    def scores(m):
        lo = (m // 2) * HEAD_DIM + (m % 2) * DA_DK
        qm = jnp.where(jnp.logical_and(lane >= lo, lane < lo + DA_DK), q, jnp.zeros_like(q))
        return lax.dot_general(k, qm, (((1,), (1,)), ((), ())), preferred_element_type=F32).astype(BF16)

    n_maps = 2 * N_HEADS
    maps = []
    s_next = scores(0)
    for m in range(n_maps):
        s = s_next
        if m + 1 < n_maps:
            s_next = scores(m + 1)
        e = jnp.exp(s - jnp.max(s, axis=0, keepdims=True))
        vt_h = v_ref[0, (m // 2) * 128:(m // 2 + 1) * 128, :]
        acc = jnp.dot(vt_h, e, preferred_element_type=F32)
        maps.append(acc / acc[HEAD_DIM:HEAD_DIM + 1, :])
    outs = [(maps[2 * h] - lam * maps[2 * h + 1])[:HEAD_DIM] for h in range(N_HEADS)]
    o = jnp.concatenate(outs, axis=0).T
    g64 = _group_ones(W_GROUP, HEAD_DIM)
    o_ref[0] = _group_rms(o, g64, HEAD_DIM, sw_ref[...]) * (1.0 - lam_init)


def _da_attention(qh, kh, vh, lam_p, sw_row, lam_init, q_tile, q_blocks, q_off, k_rows, k_blk, name):
    b, lt, _ = qh.shape
    kern = functools.partial(_da_attn_kernel, lam_init=lam_init)
    return pl.pallas_call(
        kern,
        out_shape=jax.ShapeDtypeStruct((b, q_blocks * q_tile, W_GROUP), F32),
        grid=(b, q_blocks),
        in_specs=[pl.BlockSpec((1, q_tile, W_GROUP), lambda i, t: (i, t + q_off, 0)),
                  pl.BlockSpec((1, k_rows, W_GROUP), lambda i, t: (i, k_blk, 0)),
                  pl.BlockSpec((1, 2 * W_GROUP, k_rows), lambda i, t: (i, 0, k_blk)),
                  pl.BlockSpec((4, DA_DK), lambda i, t: (0, 0)),
                  pl.BlockSpec((1, W_GROUP), lambda i, t: (0, 0))],
        out_specs=pl.BlockSpec((1, q_tile, W_GROUP), lambda i, t: (i, t, 0)),
        compiler_params=_cparams(2), name=name,
    )(qh, kh, vh, lam_p, sw_row)


def _hy_conv_kernel(p_ref, hp_ref, hn_ref, cw_ref, v_ref, x1_ref, x2_ref, xp_ref, *, nbl):
    blk = pl.program_id(1)
    xc = _short_conv(p_ref[0], hp_ref.at[0], hn_ref.at[0], cw_ref, xp_ref, blk, nbl, 3 * W_GROUP)
    v_ref[0] = xc[:, :W_GROUP].astype(v_ref.dtype)
    x1_ref[0] = xc[:, W_GROUP:2 * W_GROUP].astype(x1_ref.dtype)
    x2_ref[0] = xc[:, 2 * W_GROUP:].astype(x2_ref.dtype)


def _hy_short_conv(p_hy, conv_w, nbl):
    b, lt, width = p_hy.shape
    nb = nbl + 1
    blk = lambda: pl.BlockSpec((1, TOKEN_BLOCK, W_GROUP), lambda i, t: (i, t, 0))
    return pl.pallas_call(
        functools.partial(_hy_conv_kernel, nbl=nbl),
        out_shape=[jax.ShapeDtypeStruct((b, lt, W_GROUP), BF16)] * 3,
        grid=(b, nb),
        in_specs=[pl.BlockSpec((1, TOKEN_BLOCK, width), lambda i, t: (i, t, 0))]
                 + _halo_specs(width, nbl, lambda s: s, lt)
                 + [pl.BlockSpec((3, width), lambda i, t: (0, 0))],
        out_specs=[blk(), blk(), blk()],
        scratch_shapes=[pltpu.VMEM((TOKEN_BLOCK + 16, width), F32)],
        compiler_params=_cparams(2), name="hy_short_conv",
    )(p_hy, p_hy, p_hy, conv_w)


def _hy_filter_kernel(z_ref, w1_ref, b1_ref, f1_ref, w2_ref, b2_ref, f2_ref, w3_ref, dec_ref, o_ref):
    z = z_ref[...]
    h = jnp.sin(f1_ref[...] * (_hp_dot(z, w1_ref[...]) + b1_ref[...]))
    h = jnp.sin(f2_ref[...] * (_hp_dot(h, w2_ref[...]) + b2_ref[...]))
    h = _hp_dot(h, w3_ref[...])
    h = h * jnp.exp(-z[:, 0:1] * jnp.abs(dec_ref[...]))
    first = (pl.program_id(0) * h.shape[0] + lax.broadcasted_iota(jnp.int32, (h.shape[0], 1), 0)) == 0
    for plane in range(4):
        piece = h[:, plane * W_GROUP:(plane + 1) * W_GROUP]
        o_ref[plane] = jnp.where(first, 0.0, piece) if plane >= 2 else piece


def _hy_filters(zfeat, w1p, b1, f1, w2, b2, f2, w3, dec_row):
    n = zfeat.shape[0]
    tn = min(n, 512)
    full = lambda a: pl.BlockSpec(a.shape, lambda i: (0,) * a.ndim)
    args = [w1p, b1, f1, w2, b2, f2, w3, dec_row]
    return pl.pallas_call(
        _hy_filter_kernel,
        out_shape=jax.ShapeDtypeStruct((4, n, W_GROUP), F32),
        grid=(n // tn,),
        in_specs=[pl.BlockSpec((tn, 128), lambda i: (i, 0))] + [full(a) for a in args],
        out_specs=pl.BlockSpec((4, tn, W_GROUP), lambda i: (0, i, 0)),
        compiler_params=_cparams(1), name="hy_filters",
    )(zfeat, *args)


def _left_split_dot(a, v):
    a_hi = a.astype(BF16)
    a_lo = (a - a_hi.astype(F32)).astype(BF16)
    return jnp.dot(a_hi, v, preferred_element_type=F32) + jnp.dot(a_lo, v, preferred_element_type=F32)


def _fft_stage1_kernel(a_ref, u_ref, o_ref):
    u = u_ref[0]
    y = _left_split_dot(a_ref[...], u) if u.dtype == BF16 else _hp_dot(a_ref[...], u)
    o_ref[0] = y.astype(o_ref.dtype)


def _fft_stage1(a_mat, u_view, row_blk, col_tile):
    nbatch, _, cols = u_view.shape
    m, kdim = a_mat.shape
    return pl.pallas_call(
        _fft_stage1_kernel,
        out_shape=jax.ShapeDtypeStruct((nbatch, m, cols), BF16),
        grid=(nbatch, cols // col_tile),
        in_specs=[pl.BlockSpec((m, kdim), lambda i, j: (0, 0)),
                  pl.BlockSpec((1, kdim, col_tile), lambda i, j: (i, row_blk, j))],
        out_specs=pl.BlockSpec((1, m, col_tile), lambda i, j: (i, 0, j)),
        compiler_params=_cparams(2), name="fft_stage1",
    )(a_mat, u_view)


def _fft_mid_kernel(*refs, n2, multiply):
    if multiply:
        y_ref, g_ref, kf_ref, kb_ref, gi_ref, o_ref = refs
    else:
        y_ref, g_ref, o_ref = refs
    group = range(FFT_K1_GROUP)
    z = [_bdot(g_ref[j], y_ref[0, :, j].reshape(2 * n2, W_GROUP)) for j in group]
    if multiply:
        prod = []
        for j in group:
            zr, zi = z[j][:n2], z[j][n2:]
            kr = kf_ref[0, 0, j].astype(F32) + kb_ref[0, 0, j].astype(F32)
            ki = kf_ref[0, 1, j].astype(F32) - kb_ref[0, 1, j].astype(F32)
            prod.append(jnp.concatenate([zr * kr - zi * ki, zr * ki + zi * kr], axis=0))
        z = [_bdot(gi_ref[j], prod[j]) for j in group]
    for j in group:
        o_ref[0, :, j] = z[j].reshape(2, n2, W_GROUP).astype(o_ref.dtype)


def _fft_mid(y1, g_fwd, kf, filt, g_inv, k1p, n2):
    nbatch = y1.shape[0]
    multiply = kf is not None
    data = lambda plane: pl.BlockSpec((1, 2, FFT_K1_GROUP, n2, W_GROUP),
                                      (lambda k, i: (i, 0, k, 0, 0)) if plane is None
                                      else (lambda k, i: (plane, 0, k, 0, 0)))
    mat = lambda: pl.BlockSpec((FFT_K1_GROUP, 2 * n2, 2 * n2), lambda k, i: (k, 0, 0))
    in_specs, args = [data(None), mat()], [y1, g_fwd]
    if multiply:
        in_specs += [data(filt), data(2 + filt), mat()]
        args += [kf, kf, g_inv]
    return pl.pallas_call(
        functools.partial(_fft_mid_kernel, n2=n2, multiply=multiply),
        out_shape=jax.ShapeDtypeStruct((nbatch, 2, k1p, n2, W_GROUP), BF16),
        grid=(k1p // FFT_K1_GROUP, nbatch),
        in_specs=in_specs,
        out_specs=data(None),
        compiler_params=_cparams(2), name="fft_mid_mul" if multiply else "fft_mid",
    )(*args)


def _fft_stage4_kernel(a_ref, v_ref, u_ref, gate_ref, skip_ref, o_ref):
    y = _left_split_dot(a_ref[...], v_ref[0])
    o_ref[0] = (gate_ref[0].astype(F32) * (y + skip_ref[...] * u_ref[0].astype(F32))).astype(o_ref.dtype)


def _fft_stage4(a_mat, v_view, u_view, u_blk, gate_view, gate_blk, skip_row, col_tile):
    nbatch, kdim, cols = v_view.shape
    m = a_mat.shape[0]
    return pl.pallas_call(
        _fft_stage4_kernel,
        out_shape=jax.ShapeDtypeStruct((nbatch, m, cols), BF16),
        grid=(nbatch, cols // col_tile),
        in_specs=[pl.BlockSpec((m, kdim), lambda i, j: (0, 0)),
                  pl.BlockSpec((1, kdim, col_tile), lambda i, j: (i, 0, j)),
                  pl.BlockSpec((1, m, col_tile), lambda i, j: (i, u_blk, j)),
                  pl.BlockSpec((1, m, col_tile), lambda i, j: (i, gate_blk, j)),
                  pl.BlockSpec((1, col_tile), lambda i, j: (0, j))],
        out_specs=pl.BlockSpec((1, m, col_tile), lambda i, j: (i, 0, j)),
        compiler_params=_cparams(2), name="fft_stage4",
    )(a_mat, v_view, u_view, gate_view, skip_row)


def _fft_tables(n, n2):
    big = 2 * n
    n1 = big // n2
    two_pi = 2.0 * math.pi

    def cs(num, den):
        ang = (num % den).astype(F32) * (two_pi / den)
        return jnp.cos(ang), jnp.sin(ang)

    k1p = -(-(n1 // 2 + 1) // FFT_K1_GROUP) * FFT_K1_GROUP
    k1 = jnp.arange(k1p, dtype=jnp.int32)
    live = (k1 <= n1 // 2).astype(F32)
    weight = jnp.where((k1 == 0) | (k1 == n1 // 2), 1.0, 2.0) * live
    m1 = jnp.arange(n1, dtype=jnp.int32)
    c, s = cs(k1[:, None] * m1[None, :], n1)
    a1_full = jnp.concatenate([c * live[:, None], -s * live[:, None]], axis=0)
    k2 = jnp.arange(n2, dtype=jnp.int32)
    num = (k2[None, :, None] * k2[None, None, :] * n1 + k1[:, None, None] * k2[None, None, :]) % big
    c, s = cs(num, big)
    g_fwd = jnp.concatenate([jnp.concatenate([c, s], axis=2), jnp.concatenate([-s, c], axis=2)], axis=1)
    num = (k2[None, :, None] * k2[None, None, :] * n1 + k1[:, None, None] * k2[None, :, None]) % big
    c, s = cs(num, big)
    g_inv = jnp.concatenate([jnp.concatenate([c, -s], axis=2), jnp.concatenate([s, c], axis=2)], axis=1)
    t1 = jnp.arange(n1 // 2, dtype=jnp.int32)
    c, s = cs(t1[:, None] * k1[None, :], n1)
    a4 = jnp.concatenate([c * weight[None, :], -s * weight[None, :]], axis=1) * (1.0 / big)
    return n1, k1p, a1_full, g_fwd.astype(BF16), g_inv.astype(BF16), a4


def _hy_positional(n):
    pos = jnp.arange(n, dtype=F32)
    t01 = pos / max(n - 1, 1)
    bands = jnp.linspace(1e-4, HY_BANDS - 1, HY_BANDS, dtype=F32)
    ang = (2.0 * math.pi / n) * pos[:, None] * bands
    z = jnp.concatenate([t01[:, None], jnp.cos(ang), -jnp.sin(ang)], axis=-1)
    return jnp.pad(z, ((0, 0), (0, 128 - z.shape[1])))


def _hyena_segment(v_all, x1_all, x2_all, seg_start, n, n2, filt_args, skip):
    b, lt, c = v_all.shape
    n1, k1p, a1_full, g_fwd, g_inv, a4 = _fft_tables(n, n2)
    a1_half = a1_full[:, :n1 // 2]
    cols = n2 * c
    col_tile = min(cols, 8192)
    seg_blk = seg_start // n
    view = lambda a: a.reshape(b, lt // n2, cols)

    h = _hy_filters(_hy_positional(n), *filt_args)
    ky = _fft_stage1(a1_half, h.reshape(4, n1 // 2, cols), 0, col_tile)
    kf = _fft_mid(ky.reshape(4, 2, k1p, n2, c), g_fwd, None, 0, None, k1p, n2)

    def long_conv(u_view, u_blk, gate_all, filt):
        y1 = _fft_stage1(a1_half, u_view, u_blk, col_tile)
        vmid = _fft_mid(y1.reshape(b, 2, k1p, n2, c), g_fwd, kf, filt, g_inv, k1p, n2)
        skip_row = jnp.tile(skip[filt], n2).reshape(1, cols)
        return _fft_stage4(a4, vmid.reshape(b, 2 * k1p, cols), u_view, u_blk, view(gate_all), seg_blk,
                           skip_row, col_tile)

    z = long_conv(view(v_all), seg_blk, x1_all, 0)
    out = long_conv(z, 0, x2_all, 1)
    return out.reshape(b, n, c)


def _rope_tables(rows, lt):
    n_freq = DA_DK // 4
    inv = ROPE_THETA ** (-jnp.arange(n_freq, dtype=F32) / n_freq)
    r = jnp.repeat(jnp.arange(rows, dtype=F32), GRID_W)
    col = jnp.tile(jnp.arange(GRID_W, dtype=F32), rows)
    ang_r, ang_c = r[:, None] * inv, col[:, None] * inv
    cos32 = jnp.concatenate([jnp.cos(ang_r)] * 2 + [jnp.cos(ang_c)] * 2, axis=-1)
    sin32 = jnp.concatenate([-jnp.sin(ang_r), jnp.sin(ang_r), -jnp.sin(ang_c), jnp.sin(ang_c)], axis=-1)
    n = rows * GRID_W
    pad = lt - n
    cos = jnp.concatenate([jnp.tile(cos32, (1, W_GROUP // DA_DK)), jnp.ones((pad, W_GROUP), F32)], axis=0)
    sin = jnp.concatenate([jnp.tile(sin32, (1, W_GROUP // DA_DK)), jnp.zeros((pad, W_GROUP), F32)], axis=0)
    return cos, sin


def kernel(x, c, ctx, c_ctx, mod_w, mod_b, ln1_w, ln2_w, w_in, gdn_conv_w, gdn_a_log, gdn_dt_bias, gdn_norm_w, hy_conv_w, hy_w1, hy_b1, hy_f1, hy_w2, hy_b2, hy_f2, hy_w3, hy_decay, hy_bias, hg_lb_raw, hg_norm_w, da_q_norm, da_k_norm, da_lam, da_subln, w_out, mlp_w1, mlp_w2):
    batch, seq, d = x.shape
    n_ctx = ctx.shape[1]
    depth = mod_w.shape[0]
    assert n_ctx == TOKEN_BLOCK and seq % TOKEN_BLOCK == 0 and d == 4 * W_GROUP
    nbl = seq // TOKEN_BLOCK
    lt = seq + n_ctx

    xs = jnp.concatenate([x, ctx], axis=1)
    cond = jnp.concatenate([c, c_ctx[None, :], jnp.zeros((-(batch + 1) % 8, d), F32)], axis=0)
    mods = _modulation(cond, mod_w, mod_b)[:, :batch + 1].reshape(depth, batch + 1, 6, d)
    mods = jnp.pad(mods, ((0, 0), (0, 0), (0, 2), (0, 0)))

    lb_all = jnp.cumsum(jax.nn.softmax(hg_lb_raw.astype(F32), axis=0), axis=0)
    lb_all = lb_all - lb_all[0]
    cos_tab, sin_tab = _rope_tables(seq // GRID_W, lt)

    gdn_cols = 4 * W_GROUP + 4 * N_HEADS
    hy_cols, hg_cols, da_cols = 3 * W_GROUP, 5 * W_GROUP, 3 * W_GROUP
    splits = (GDN_PAD_COLS, hy_cols, hg_cols, da_cols)

    for l in range(depth):
        want_ctx = l < depth - 1
        modtab = mods[l]
        wl = w_in[l]
        w_pad = jnp.concatenate([wl[:, :gdn_cols], jnp.zeros((d, GDN_PAD_COLS - gdn_cols), F32),
                                 wl[:, gdn_cols:]], axis=1).astype(BF16)
        qw = jnp.tile(da_q_norm[l], W_GROUP // DA_DK).reshape(1, W_GROUP)
        kw = jnp.tile(da_k_norm[l], W_GROUP // DA_DK).reshape(1, W_GROUP)
        p_gdn, p_hy, p_hg, qh, kh, vh = _in_proj(xs, modtab, ln1_w[l], w_pad, splits, cos_tab, sin_tab, qw, kw, nbl)

        pad16 = lambda a: jnp.pad(a.reshape(1, 2 * N_HEADS), ((0, 0), (0, 128 - 2 * N_HEADS)))
        alog_row, dtb_row = pad16(gdn_a_log[l]), pad16(gdn_dt_bias[l])
        gnw = jnp.tile(gdn_norm_w[l], N_HEADS).reshape(1, W_GROUP)
        o_b = _gdn_direction(p_gdn, gdn_conv_w[l], alog_row, dtb_row, gnw, None, 1, nbl)
        o_gdn = _gdn_direction(p_gdn, gdn_conv_w[l], alog_row, dtb_row, gnw, o_b, 0, nbl)

        hv, hx1, hx2 = _hy_short_conv(p_hy, hy_conv_w[l], nbl)
        w1p = jnp.pad(hy_w1[l], ((0, 128 - hy_w1.shape[1]), (0, 0)))
        row = lambda a: a.reshape(1, -1)
        dec_row = jnp.tile(hy_decay[l].reshape(1, 2 * W_GROUP), (1, 2))
        filt_args = (w1p, row(hy_b1[l]), row(hy_f1[l]), hy_w2[l], row(hy_b2[l]), row(hy_f2[l]), hy_w3[l], dec_row)
        o_hy_lat = _hyena_segment(hv, hx1, hx2, 0, seq, FFT_N2_LAT, filt_args, hy_bias[l])
        o_hy_ctx = (_hyena_segment(hv, hx1, hx2, seq, n_ctx, FFT_N2_CTX, filt_args, hy_bias[l])
                    if want_ctx else None)

        lb_row = lb_all[l].reshape(1, W_GROUP)
        hnw = jnp.tile(hg_norm_w[l], N_HEADS).reshape(1, W_GROUP)
        o_hb = _hgrn_direction(p_hg, lb_row, hnw, None, 1, nbl)
        o_hg = _hgrn_direction(p_hg, lb_row, hnw, o_hb, 0, nbl)

        lam_init = 0.8 - 0.6 * math.exp(-0.3 * l)
        sw = jnp.tile(da_subln[l], N_HEADS).reshape(1, W_GROUP)
        o_da = _da_attention(qh, kh, vh, da_lam[l], sw, lam_init, DA_Q_TILE, seq // DA_Q_TILE, 0, lt, 0,
                             "da_attn_lat")
        o_da_ctx = (_da_attention(qh, kh, vh, da_lam[l], sw, lam_init, TOKEN_BLOCK, 1, nbl, TOKEN_BLOCK, nbl,
                                  "da_attn_ctx") if want_ctx else None)

        n_blocks = nbl + 1 if want_ctx else nbl
        xs_mid = _proj_residual(xs, modtab, w_out[l].astype(BF16),
                                [o_gdn, (o_hy_lat, o_hy_ctx), o_hg, (o_da, o_da_ctx)], 2, n_blocks, nbl, "out_proj")
        (hid,) = _adaln_proj(xs_mid, modtab, ln2_w[l], mlp_w1[l].astype(BF16), (mlp_w1.shape[2],), (BF16,),
                             3, 4, True, nbl, "mlp_up")
        xs = _proj_residual(xs_mid, modtab, mlp_w2[l].astype(BF16), [hid], 5, n_blocks, nbl, "mlp_down")
    return xs
```

```python
import functools
import math

import jax
import jax.numpy as jnp
from jax import lax
from jax.experimental import pallas as pl
from jax.experimental.pallas import tpu as pltpu

F32 = jnp.float32
BF16 = jnp.bfloat16

EPS = 1e-6
TOKEN_BLOCK = 256
CHUNK = 64
HEAD_DIM = 64
N_HEADS = 4
W_GROUP = 256
DA_DK = 32
GRID_W = 64
ROPE_THETA = 10000.0
HY_BANDS = 16
FFT_N2_LAT = 128
FFT_N2_CTX = 16
FFT_K1_GROUP = 8
DA_Q_TILE = 256
SCAN_ROWS = 4
GDN_PAD_COLS = 1152
VMEM_LIMIT_BYTES = 56 * 1024 * 1024


def _cparams(n_axes):
    return pltpu.CompilerParams(dimension_semantics=("arbitrary",) * n_axes,
                                vmem_limit_bytes=VMEM_LIMIT_BYTES)


def _sigmoid(x):
    return 1.0 / (1.0 + jnp.exp(-x))


def _silu(x):
    return x * _sigmoid(x)


def _softplus(x):
    return jnp.maximum(x, 0.0) + jnp.log(1.0 + jnp.exp(-jnp.abs(x)))


def _log_sigmoid(x):
    return jnp.minimum(x, 0.0) - jnp.log(1.0 + jnp.exp(-jnp.abs(x)))


def _bdot(a, b):
    return jnp.dot(a.astype(BF16), b.astype(BF16), preferred_element_type=F32)


def _bdot_nt(a, b):
    return lax.dot_general(a.astype(BF16), b.astype(BF16), (((1,), (1,)), ((), ())),
                           preferred_element_type=F32)


def _split3(x):
    hi = x.astype(BF16)
    r = x - hi.astype(F32)
    mid = r.astype(BF16)
    lo = (r - mid.astype(F32)).astype(BF16)
    return hi, mid, lo


def _exact_left(m_bf16, x):
    hi, mid, lo = _split3(x)
    d = lambda p: jnp.dot(m_bf16, p, preferred_element_type=F32)
    return d(hi) + d(mid) + d(lo)


def _hp_dot(a, b):
    ah = a.astype(BF16)
    al = (a - ah.astype(F32)).astype(BF16)
    bh = b.astype(BF16)
    bl = (b - bh.astype(F32)).astype(BF16)
    d = lambda p, q: jnp.dot(p, q, preferred_element_type=F32)
    return d(ah, bh) + d(ah, bl) + d(al, bh)


def _group_ones(width, group):
    r = lax.broadcasted_iota(jnp.int32, (width, width), 0) // group
    c = lax.broadcasted_iota(jnp.int32, (width, width), 1) // group
    return (r == c).astype(BF16)


def _mod_kernel(s_ref, w_ref, b_ref, o_ref):
    s = _silu(s_ref[...])
    o_ref[0] = _bdot(s, w_ref[0]) + b_ref[0]


def _modulation(cond, mod_w, mod_b):
    depth, d, n = mod_w.shape
    rows = cond.shape[0]
    tn = 1536
    return pl.pallas_call(
        _mod_kernel,
        out_shape=jax.ShapeDtypeStruct((depth, rows, n), F32),
        grid=(depth, n // tn),
        in_specs=[pl.BlockSpec((rows, d), lambda l, j: (0, 0)),
                  pl.BlockSpec((1, d, tn), lambda l, j: (l, 0, j)),
                  pl.BlockSpec((1, 1, tn), lambda l, j: (l, 0, j))],
        out_specs=pl.BlockSpec((1, rows, tn), lambda l, j: (l, 0, j)),
        compiler_params=_cparams(2), name="modulation",
    )(cond, mod_w, mod_b.reshape(depth, 1, n))


def _adaln_proj_kernel(x_ref, mod_ref, lnw_ref, w_ref, *out_refs, shift_row, scale_row, splits, sq_relu):
    x = x_ref[0]
    ms = jnp.mean(x * x, axis=-1, keepdims=True)
    y = x * lax.rsqrt(ms + EPS) * lnw_ref[...]
    mod = mod_ref[0]
    y = y * (1.0 + mod[scale_row:scale_row + 1]) + mod[shift_row:shift_row + 1]
    h = _bdot(y, w_ref[...])
    if sq_relu:
        h = jnp.square(jnp.maximum(h, 0.0))
    off = 0
    for o_ref, n in zip(out_refs, splits):
        o_ref[0] = h[:, off:off + n].astype(o_ref.dtype)
        off += n


def _mod_index(nbl, batch):
    return lambda b, t: (jnp.where(t == nbl, batch, b), 0, 0)


def _adaln_proj(xs, modtab, ln_w, w, splits, out_dtypes, shift_row, scale_row, sq_relu, nbl, name):
    b, lt, d = xs.shape
    nb = lt // TOKEN_BLOCK
    n = w.shape[1]
    kern = functools.partial(_adaln_proj_kernel, shift_row=shift_row, scale_row=scale_row,
                             splits=splits, sq_relu=sq_relu)
    return pl.pallas_call(
        kern,
        out_shape=[jax.ShapeDtypeStruct((b, lt, s), dt) for s, dt in zip(splits, out_dtypes)],
        grid=(b, nb),
        in_specs=[pl.BlockSpec((1, TOKEN_BLOCK, d), lambda i, t: (i, t, 0)),
                  pl.BlockSpec((1, 8, d), _mod_index(nbl, b)),
                  pl.BlockSpec((1, d), lambda i, t: (0, 0)),
                  pl.BlockSpec((d, n), lambda i, t: (0, 0))],
        out_specs=[pl.BlockSpec((1, TOKEN_BLOCK, s), lambda i, t: (i, t, 0)) for s in splits],
        compiler_params=_cparams(2), name=name,
    )(xs, modtab, ln_w.reshape(1, d), w)


def _proj_residual_kernel(x_ref, mod_ref, w_ref, *rest, gate_row, split, nbl):
    a_refs, o_ref = rest[:-1], rest[-1]
    is_ctx = pl.program_id(1) == nbl
    pieces, pos = [], 0
    for two in split:
        lat = a_refs[pos][0].astype(BF16)
        pieces.append(jnp.where(is_ctx, a_refs[pos + 1][0].astype(BF16), lat) if two else lat)
        pos += 2 if two else 1
    a = jnp.concatenate(pieces, axis=1) if len(pieces) > 1 else pieces[0]
    gate = mod_ref[0][gate_row:gate_row + 1]
    o_ref[0] = x_ref[0] + gate * _bdot(a, w_ref[...])


def _proj_residual(xs, modtab, w, acts, gate_row, n_blocks, nbl, name):
    b, lt, d = xs.shape
    split = tuple(isinstance(a, tuple) and a[1] is not None for a in acts)
    specs, args = [], []
    for a, two in zip(acts, split):
        lat = a[0] if isinstance(a, tuple) else a
        width = lat.shape[2]
        if isinstance(a, tuple):
            specs.append(pl.BlockSpec((1, TOKEN_BLOCK, width), lambda i, t: (i, jnp.minimum(t, nbl - 1), 0)))
        else:
            specs.append(pl.BlockSpec((1, TOKEN_BLOCK, width), lambda i, t: (i, t, 0)))
        args.append(lat)
        if two:
            specs.append(pl.BlockSpec((1, TOKEN_BLOCK, width), lambda i, t: (i, 0, 0)))
            args.append(a[1])
    kern = functools.partial(_proj_residual_kernel, gate_row=gate_row, split=split, nbl=nbl)
    return pl.pallas_call(
        kern,
        out_shape=jax.ShapeDtypeStruct((b, n_blocks * TOKEN_BLOCK, d), F32),
        grid=(b, n_blocks),
        in_specs=[pl.BlockSpec((1, TOKEN_BLOCK, d), lambda i, t: (i, t, 0)),
                  pl.BlockSpec((1, 8, d), _mod_index(nbl, b)),
                  pl.BlockSpec(w.shape, lambda i, t: (0, 0))] + specs,
        out_specs=pl.BlockSpec((1, TOKEN_BLOCK, d), lambda i, t: (i, t, 0)),
        compiler_params=_cparams(2), name=name,
    )(xs, modtab, w, *args)


def _scan_block(step, nbl, direction):
    lat = step - 1 if direction == 0 else nbl - step
    return jnp.where(step == 0, nbl, lat)


def _halo_specs(width, nbl, block_of_step, lt, batch_rows=1):
    rows8 = TOKEN_BLOCK // 8

    def prev_map(b, s):
        return (b, jnp.maximum(block_of_step(s) * rows8 - 1, 0), 0)

    def next_map(b, s):
        return (b, jnp.minimum((block_of_step(s) + 1) * rows8, lt // 8 - 1), 0)

    return [pl.BlockSpec((batch_rows, 8, width), prev_map), pl.BlockSpec((batch_rows, 8, width), next_map)]


def _short_conv(cur, hp_ref, hn_ref, cw_ref, xp_ref, blk, nbl, width):
    seg_first = jnp.logical_or(blk == 0, blk == nbl)
    seg_last = jnp.logical_or(blk == nbl - 1, blk == nbl)
    xp_ref[0:8, :] = jnp.where(seg_first, 0.0, hp_ref[:, :width])
    xp_ref[8:8 + TOKEN_BLOCK, :] = cur
    xp_ref[8 + TOKEN_BLOCK:16 + TOKEN_BLOCK, :] = jnp.where(seg_last, 0.0, hn_ref[:, :width])
    cw = cw_ref[...]
    return (cw[0:1] * xp_ref[7:7 + TOKEN_BLOCK, :] + cw[1:2] * xp_ref[8:8 + TOKEN_BLOCK, :]
            + cw[2:3] * xp_ref[9:9 + TOKEN_BLOCK, :])


def _scan_tables(direction, chunk):
    i = jnp.arange(TOKEN_BLOCK, dtype=jnp.int32)[:, None]
    j = jnp.arange(TOKEN_BLOCK, dtype=jnp.int32)[None, :]
    same = (i // chunk) == (j // chunk)
    pos = (lambda t: t % chunk) if direction == 0 else (lambda t: chunk - 1 - t % chunk)
    pi, pj = pos(i), pos(j)
    incl = same & (pj <= pi)
    tables = [incl, (i // HEAD_DIM) == (j // HEAD_DIM), i == j]
    m = 1
    while m < chunk:
        boundary = (pi // (2 * m)) * (2 * m) + m - 1
        tables.append(incl.astype(F32) - (same & (pj <= boundary)).astype(F32))
        m *= 2
    return jnp.stack([a.astype(BF16) for a in tables])


def _group_rms(o, gmat, group, weight):
    ms = jnp.dot((o * o).astype(BF16), gmat, preferred_element_type=F32) * (1.0 / group)
    return o * lax.rsqrt(ms + EPS) * weight


def _gdn_kernel(*refs, direction, finish, nbl):
    if finish:
        (p_ref, hp_ref, hn_ref, cw_ref, alog_ref, dtb_ref, nw_ref, bm_ref, other_ref,
         o_ref, s_ref, xp_ref, ob_ref) = refs
    else:
        p_ref, hp_ref, hn_ref, cw_ref, alog_ref, dtb_ref, nw_ref, bm_ref, o_ref, s_ref, xp_ref, ob_ref = refs
    step = pl.program_id(1)
    blk = _scan_block(step, nbl, direction)

    @pl.when(step == 0)
    def _():
        s_ref[...] = jnp.zeros_like(s_ref)

    g64 = bm_ref[1]
    rows = range(SCAN_ROWS)
    p_rows, q_rows, k_rows, v_rows, kt_rows, zc_rows, zr_rows = [], [], [], [], [], [], []
    for r in rows:
        p = p_ref[r]
        xc = _silu(_short_conv(p[:, :3 * W_GROUP], hp_ref.at[r], hn_ref.at[r], cw_ref, xp_ref.at[r], blk, nbl,
                               3 * W_GROUP))
        q = xc[:, :W_GROUP]
        k = xc[:, W_GROUP:2 * W_GROUP]
        q = q * lax.rsqrt(_bdot(q * q, g64) + EPS) * (HEAD_DIM ** -0.5)
        k = k * lax.rsqrt(_bdot(k * k, g64) + EPS)
        ab = p[:, 4 * W_GROUP:4 * W_GROUP + 128]
        lane = lax.broadcasted_iota(jnp.int32, ab.shape, 1)
        g_all = -jnp.exp(alog_ref[...]) * _softplus(ab + dtb_ref[...])
        w_all = jnp.where(lane < 2 * N_HEADS, g_all, _sigmoid(ab))
        z_col = jnp.where(lane < 2 * N_HEADS, _exact_left(bm_ref[0], w_all), w_all)
        p_rows.append(p)
        q_rows.append(q)
        k_rows.append(k)
        v_rows.append(xc[:, 2 * W_GROUP:])
        kt_rows.append(_bdot_nt(bm_ref[2], k))
        zc_rows.append(z_col)
        zr_rows.append(z_col.T)

    n_chunks = TOKEN_BLOCK // CHUNK
    problems = [(r, h) for r in rows for h in range(N_HEADS)]
    heads = range(len(problems))
    lanes = [slice(h * HEAD_DIM, (h + 1) * HEAD_DIM) for _, h in problems]
    gidx = [N_HEADS * direction + h for _, h in problems]
    q = [q_rows[r] for r, _ in problems]
    k = [k_rows[r] for r, _ in problems]
    v = [v_rows[r] for r, _ in problems]
    k_t = [kt_rows[r] for r, _ in problems]
    z_col = [zc_rows[r] for r, _ in problems]
    z_row = [zr_rows[r] for r, _ in problems]
    ci_ = lax.broadcasted_iota(jnp.int32, (CHUNK, CHUNK), 0)
    cj_ = lax.broadcasted_iota(jnp.int32, (CHUNK, CHUNK), 1)
    pi_, pj_ = (ci_, cj_) if direction == 0 else (CHUNK - 1 - ci_, CHUNK - 1 - cj_)
    incl = pj_ <= pi_
    strict_f = (pj_ < pi_).astype(F32)
    eye_f = (ci_ == cj_).astype(F32)

    def half_f(m):
        return (((pi_ // (2 * m)) == (pj_ // (2 * m))) & ((pi_ // m) % 2 == 1) & ((pj_ // m) % 2 == 0)).astype(F32)

    chunked = lambda a: a.reshape(n_chunks, CHUNK, a.shape[1])
    bmm = lambda a, b: jnp.einsum("cij,cjk->cik", a.astype(BF16), b.astype(BF16), preferred_element_type=F32)
    bmm_nt = lambda a, b: jnp.einsum("cid,cjd->cij", a.astype(BF16), b.astype(BF16), preferred_element_type=F32)
    gc_c = [chunked(z_col[i][:, gi:gi + 1]) for i, gi in enumerate(gidx)]
    gc_r = [jnp.stack([z_row[i][gi:gi + 1, c * CHUNK:(c + 1) * CHUNK] for c in range(n_chunks)])
            for i, gi in enumerate(gidx)]
    beta = [chunked(z_col[i][:, 2 * N_HEADS + gi:2 * N_HEADS + gi + 1]) for i, gi in enumerate(gidx)]
    qc = [chunked(q[i][:, ls]) for i, ls in enumerate(lanes)]
    kc = [chunked(k[i][:, ls]) for i, ls in enumerate(lanes)]
    vc = [chunked(v[i][:, ls]) for i, ls in enumerate(lanes)]
    eg = [jnp.exp(g) for g in gc_c]
    decay = [jnp.exp(jnp.where(incl, gc_c[h] - gc_r[h], -jnp.inf)) for h in heads]
    kk = [bmm_nt(kc[h], kc[h]) for h in heads]
    qk = [bmm_nt(qc[h], kc[h]) for h in heads]
    n = [kk[h] * beta[h] * decay[h] * strict_f for h in heads]
    inv = [eye_f - n[h] * half_f(1) for h in heads]
    n_bf = [n[h].astype(BF16) for h in heads]
    m = 2
    while m < CHUNK:
        half = half_f(m).astype(BF16)
        x = [bmm(inv[h], n_bf[h] * half) for h in heads]
        inv = [inv[h] - bmm(x[h], inv[h]) for h in heads]
        m *= 2
    uw = [bmm(inv[h], jnp.concatenate([vc[h] * beta[h], kc[h] * (beta[h] * eg[h])], axis=2)) for h in heads]
    attn = [qk[h] * decay[h] for h in heads]
    qg = [qc[h] * eg[h] for h in heads]
    state = [s_ref[h] for h in heads]

    last = CHUNK - 1 if direction == 0 else 0
    for ci in range(n_chunks):
        c = ci if direction == 0 else n_chunks - 1 - ci
        rs = slice(c * CHUNK, (c + 1) * CHUNK)
        g_last = [gc_c[h][c, last:last + 1, :] for h in heads]
        v_new = [uw[h][c, :, :HEAD_DIM] - _bdot(uw[h][c, :, HEAD_DIM:], state[h]) for h in heads]
        for h, ls in enumerate(lanes):
            ob_ref[problems[h][0], rs, ls] = _bdot(qg[h][c], state[h]) + _bdot(attn[h][c], v_new[h])
        state = [state[h] * jnp.exp(g_last[h])
                 + _bdot(k_t[h][ls, rs] * jnp.exp(g_last[h] - gc_r[h][c]), v_new[h]) for h, ls in enumerate(lanes)]
    for h in heads:
        s_ref[h] = state[h]

    for r in rows:
        if finish:
            o = ob_ref[r] + other_ref[r]
            gate = _silu(p_rows[r][:, 3 * W_GROUP:4 * W_GROUP])
            o_ref[r] = _group_rms(o, g64, HEAD_DIM, nw_ref[...]) * gate
        else:
            o_ref[r] = ob_ref[r]


def _gdn_direction(p_gdn, conv_w, alog_row, dtb_row, nw_row, other, direction, nbl):
    b, lt, width = p_gdn.shape
    nb = nbl + 1
    blk_of = lambda s: _scan_block(s, nbl, direction)
    finish = other is not None
    bmask = _scan_tables(direction, CHUNK)[:3]
    assert b % SCAN_ROWS == 0
    kern = functools.partial(_gdn_kernel, direction=direction, finish=finish, nbl=nbl)
    in_specs = ([pl.BlockSpec((SCAN_ROWS, TOKEN_BLOCK, width), lambda i, s: (i, blk_of(s), 0))]
                + _halo_specs(3 * W_GROUP, nbl, blk_of, lt, SCAN_ROWS)
                + [pl.BlockSpec((3, 3 * W_GROUP), lambda i, s: (0, 0)),
                   pl.BlockSpec((1, 128), lambda i, s: (0, 0)),
                   pl.BlockSpec((1, 128), lambda i, s: (0, 0)),
                   pl.BlockSpec((1, W_GROUP), lambda i, s: (0, 0)),
                   pl.BlockSpec(bmask.shape, lambda i, s: (0, 0, 0))])
    args = [p_gdn, p_gdn, p_gdn, conv_w, alog_row, dtb_row, nw_row, bmask]
    if finish:
        in_specs.append(pl.BlockSpec((SCAN_ROWS, TOKEN_BLOCK, W_GROUP), lambda i, s: (i, blk_of(s), 0)))
        args.append(other)
    return pl.pallas_call(
        kern,
        out_shape=jax.ShapeDtypeStruct((b, lt, W_GROUP), F32),
        grid=(b // SCAN_ROWS, nb),
        in_specs=in_specs,
        out_specs=pl.BlockSpec((SCAN_ROWS, TOKEN_BLOCK, W_GROUP), lambda i, s: (i, blk_of(s), 0)),
        scratch_shapes=[pltpu.VMEM((SCAN_ROWS * N_HEADS, HEAD_DIM, HEAD_DIM), F32),
                        pltpu.VMEM((SCAN_ROWS, TOKEN_BLOCK + 16, 3 * W_GROUP), F32),
                        pltpu.VMEM((SCAN_ROWS, TOKEN_BLOCK, W_GROUP), F32)],
        compiler_params=_cparams(2), name=f"gdn_dir{direction}",
    )(*args)


def _hgrn_kernel(*refs, direction, finish, nbl):
    if finish:
        p_ref, lb_ref, nw_ref, bm_ref, cum_ref, other_ref, o_ref, s_ref, ob_ref = refs
    else:
        p_ref, lb_ref, nw_ref, bm_ref, cum_ref, o_ref, s_ref, ob_ref = refs
    step = pl.program_id(1)

    @pl.when(step == 0)
    def _():
        s_ref[...] = jnp.zeros_like(s_ref)

    lb = lb_ref[...]
    n_levels = cum_ref.shape[0] // TOKEN_BLOCK - 1
    rows = range(SCAN_ROWS)
    p_rows, q_rows, key_rows, v_rows, vt_rows, gcum_rows, qg_rows, lev_rows = [], [], [], [], [], [], [], []
    for r in rows:
        p = p_ref[r]
        q = _silu(p[:, :W_GROUP])
        v = p[:, W_GROUP:2 * W_GROUP]
        fl = p[:, (2 + direction) * W_GROUP:(3 + direction) * W_GROUP]
        a = jnp.log(lb)
        bb = jnp.log(1.0 - lb) + _log_sigmoid(fl)
        logf = jnp.maximum(a, bb) + jnp.log(1.0 + jnp.exp(-jnp.abs(a - bb)))
        logf_hi = logf.astype(BF16)
        logf_lo = (logf - logf_hi.astype(F32)).astype(BF16)
        cums = (jnp.dot(cum_ref[...], logf_hi, preferred_element_type=F32)
                + jnp.dot(cum_ref[...], logf_lo, preferred_element_type=F32))
        gcum = cums[:TOKEN_BLOCK]
        p_rows.append(p)
        q_rows.append(q)
        key_rows.append((1.0 - lb) * _sigmoid(-fl))
        v_rows.append(v)
        vt_rows.append(_bdot_nt(bm_ref[2], v))
        gcum_rows.append(gcum)
        qg_rows.append(q * jnp.exp(gcum))
        lev_rows.append([jnp.exp(-jnp.abs(cums[(lev + 1) * TOKEN_BLOCK:(lev + 2) * TOKEN_BLOCK]))
                         for lev in range(n_levels)])

    n_chunks = TOKEN_BLOCK // CHUNK
    problems = [(r, h) for r in rows for h in range(N_HEADS)]
    heads = range(len(problems))
    lanes = [slice(h * HEAD_DIM, (h + 1) * HEAD_DIM) for _, h in problems]
    ci_ = lax.broadcasted_iota(jnp.int32, (CHUNK, CHUNK), 0)
    cj_ = lax.broadcasted_iota(jnp.int32, (CHUNK, CHUNK), 1)
    pi_, pj_ = (ci_, cj_) if direction == 0 else (CHUNK - 1 - ci_, CHUNK - 1 - cj_)
    chunked = lambda a: a.reshape(n_chunks, CHUNK, a.shape[1])
    bmm = lambda a, b: jnp.einsum("cij,cjk->cik", a.astype(BF16), b.astype(BF16), preferred_element_type=F32)
    bmm_nt = lambda a, b: jnp.einsum("cid,cjd->cij", a.astype(BF16), b.astype(BF16), preferred_element_type=F32)
    qc = [chunked(q_rows[r][:, ls]) for (r, _), ls in zip(problems, lanes)]
    kc = [chunked(key_rows[r][:, ls]) for (r, _), ls in zip(problems, lanes)]
    amat = [bmm_nt(qc[h], kc[h]) * (ci_ == cj_).astype(F32) for h in heads]
    qc_bf = [a.astype(BF16) for a in qc]
    kc_bf = [a.astype(BF16) for a in kc]
    for lev in range(n_levels):
        m = 1 << lev
        half = (((pi_ // (2 * m)) == (pj_ // (2 * m))) & ((pi_ // m) % 2 == 1) & ((pj_ // m) % 2 == 0)).astype(F32)
        e_bf = [lev_rows[r][lev].astype(BF16) for r in rows]
        e = [chunked(e_bf[r][:, ls]) for (r, _), ls in zip(problems, lanes)]
        amat = [amat[h] + half * bmm_nt(qc_bf[h] * e[h], kc_bf[h] * e[h]) for h in heads]
    intra = [bmm(amat[h], chunked(v_rows[r][:, ls])) for h, ((r, _), ls) in enumerate(zip(problems, lanes))]
    state = [s_ref[h] for h in heads]

    last = CHUNK - 1 if direction == 0 else 0
    for ci in range(n_chunks):
        c = ci if direction == 0 else n_chunks - 1 - ci
        rs = slice(c * CHUNK, (c + 1) * CHUNK)
        g_end = [gcum_rows[r][c * CHUNK + last:c * CHUNK + last + 1, :] for r in rows]
        kdec = [key_rows[r][rs, :] * jnp.exp(g_end[r] - gcum_rows[r][rs, :]) for r in rows]
        decay_end = [jnp.exp(g_end[r]) for r in rows]
        for h, ((r, _), ls) in enumerate(zip(problems, lanes)):
            ob_ref[r, rs, ls] = _bdot_nt(qg_rows[r][rs, ls], state[h]) + intra[h][c]
        state = [state[h] * decay_end[r][:, ls] + _bdot(vt_rows[r][ls, rs], kdec[r][:, ls])
                 for h, ((r, _), ls) in enumerate(zip(problems, lanes))]
    for h in heads:
        s_ref[h] = state[h]

    for r in rows:
        if finish:
            o = ob_ref[r] + other_ref[r]
            gate = _silu(p_rows[r][:, 4 * W_GROUP:5 * W_GROUP])
            o_ref[r] = _group_rms(o, bm_ref[1], HEAD_DIM, nw_ref[...]) * gate
        else:
            o_ref[r] = ob_ref[r]


def _hgrn_direction(p_hg, lb_row, nw_row, other, direction, nbl):
    b, lt, width = p_hg.shape
    nb = nbl + 1
    blk_of = lambda s: _scan_block(s, nbl, direction)
    finish = other is not None
    tables = _scan_tables(direction, CHUNK)
    bmask = tables[:3]
    cum_tab = jnp.concatenate([tables[0:1], tables[3:]], axis=0).reshape(-1, TOKEN_BLOCK)
    assert b % SCAN_ROWS == 0
    kern = functools.partial(_hgrn_kernel, direction=direction, finish=finish, nbl=nbl)
    in_specs = [pl.BlockSpec((SCAN_ROWS, TOKEN_BLOCK, width), lambda i, s: (i, blk_of(s), 0)),
                pl.BlockSpec((1, W_GROUP), lambda i, s: (0, 0)),
                pl.BlockSpec((1, W_GROUP), lambda i, s: (0, 0)),
                pl.BlockSpec(bmask.shape, lambda i, s: (0, 0, 0)),
                pl.BlockSpec(cum_tab.shape, lambda i, s: (0, 0))]
    args = [p_hg, lb_row, nw_row, bmask, cum_tab]
    if finish:
        in_specs.append(pl.BlockSpec((SCAN_ROWS, TOKEN_BLOCK, W_GROUP), lambda i, s: (i, blk_of(s), 0)))
        args.append(other)
    return pl.pallas_call(
        kern,
        out_shape=jax.ShapeDtypeStruct((b, lt, W_GROUP), F32),
        grid=(b // SCAN_ROWS, nb),
        in_specs=in_specs,
        out_specs=pl.BlockSpec((SCAN_ROWS, TOKEN_BLOCK, W_GROUP), lambda i, s: (i, blk_of(s), 0)),
        scratch_shapes=[pltpu.VMEM((SCAN_ROWS * N_HEADS, HEAD_DIM, HEAD_DIM), F32),
                        pltpu.VMEM((SCAN_ROWS, TOKEN_BLOCK, W_GROUP), F32)],
        compiler_params=_cparams(2), name=f"hgrn_dir{direction}",
    )(*args)


def _da_prep(p, cos, sin, qw, kw):
    g32 = _group_ones(W_GROUP, DA_DK)
    lane = lax.broadcasted_iota(jnp.int32, (TOKEN_BLOCK, W_GROUP), 1)
    first_half = (lane % 16) < 8

    def norm_rope(x, w):
        ms = _bdot(x * x, g32) * (1.0 / DA_DK)
        y = x * lax.rsqrt(ms + EPS) * w
        partner = jnp.where(first_half, pltpu.roll(y, W_GROUP - 8, axis=1), pltpu.roll(y, 8, axis=1))
        return y * cos + partner * sin

    q = (norm_rope(p[:, :W_GROUP], qw) * (DA_DK ** -0.5)).astype(BF16)
    k = norm_rope(p[:, W_GROUP:2 * W_GROUP], kw).astype(BF16)
    dst = lax.broadcasted_iota(jnp.int32, (2 * W_GROUP, W_GROUP), 0)
    src = lax.broadcasted_iota(jnp.int32, (2 * W_GROUP, W_GROUP), 1)
    spread = (dst == (src // HEAD_DIM) * 128 + src % HEAD_DIM).astype(BF16)
    wide_t = _bdot_nt(spread, p[:, 2 * W_GROUP:])
    row_w = lax.broadcasted_iota(jnp.int32, wide_t.shape, 0)
    return q, k, jnp.where(row_w % 128 == HEAD_DIM, 1.0, wide_t).astype(BF16)


def _in_proj_kernel(x_ref, mod_ref, lnw_ref, w_ref, cos_ref, sin_ref, qw_ref, kw_ref,
                    gdn_ref, hy_ref, hg_ref, q_ref, k_ref, vt_ref, *, splits):
    x = x_ref[0]
    ms = jnp.mean(x * x, axis=-1, keepdims=True)
    y = x * lax.rsqrt(ms + EPS) * lnw_ref[...]
    mod = mod_ref[0]
    y = y * (1.0 + mod[1:2]) + mod[0:1]
    h = _bdot(y, w_ref[...])
    off = 0
    for o_ref, n in zip((gdn_ref, hy_ref, hg_ref), splits[:3]):
        o_ref[0] = h[:, off:off + n]
        off += n
    q, k, vt = _da_prep(h[:, off:], cos_ref[...], sin_ref[...], qw_ref[...], kw_ref[...])
    q_ref[0] = q
    k_ref[0] = k
    vt_ref[0] = vt


def _in_proj(xs, modtab, ln_w, w, splits, cos_tab, sin_tab, qw_row, kw_row, nbl):
    b, lt, d = xs.shape
    nb = lt // TOKEN_BLOCK
    n = w.shape[1]
    tok = lambda width: pl.BlockSpec((1, TOKEN_BLOCK, width), lambda i, t: (i, t, 0))
    row = lambda width: pl.BlockSpec((1, width), lambda i, t: (0, 0))
    tab = lambda: pl.BlockSpec((TOKEN_BLOCK, W_GROUP), lambda i, t: (t, 0))
    return pl.pallas_call(
        functools.partial(_in_proj_kernel, splits=splits),
        out_shape=[jax.ShapeDtypeStruct((b, lt, s), F32) for s in splits[:3]]
                  + [jax.ShapeDtypeStruct((b, lt, W_GROUP), BF16)] * 2
                  + [jax.ShapeDtypeStruct((b, 2 * W_GROUP, lt), BF16)],
        grid=(b, nb),
        in_specs=[tok(d), pl.BlockSpec((1, 8, d), _mod_index(nbl, b)), row(d),
                  pl.BlockSpec((d, n), lambda i, t: (0, 0)), tab(), tab(), row(W_GROUP), row(W_GROUP)],
        out_specs=[tok(s) for s in splits[:3]] + [tok(W_GROUP), tok(W_GROUP),
                   pl.BlockSpec((1, 2 * W_GROUP, TOKEN_BLOCK), lambda i, t: (i, 0, t))],
        compiler_params=_cparams(2), name="in_proj",
    )(xs, modtab, ln_w.reshape(1, d), w, cos_tab, sin_tab, qw_row, kw_row)


def _da_attn_kernel(q_ref, k_ref, v_ref, lam_ref, sw_ref, o_ref, *, lam_init):
    q = q_ref[0]
    k = k_ref[0]
    lp = lam_ref[...]
    lam = (jnp.exp(jnp.sum(lp[0:1] * lp[1:2], axis=1, keepdims=True))
           - jnp.exp(jnp.sum(lp[2:3] * lp[3:4], axis=1, keepdims=True)) + lam_init)
    lane = lax.broadcasted_iota(jnp.int32, q.shape, 1)

    def scores(m):
        lo = (m // 2) * HEAD_DIM + (m % 2) * DA_DK
        qm = jnp.where(jnp.logical_and(lane >= lo, lane < lo + DA_DK), q, jnp.zeros_like(q))
        return lax.dot_general(k, qm, (((1,), (1,)), ((), ())), preferred_element_type=F32).astype(BF16)

    n_maps = 2 * N_HEADS
    maps = []
    s_next = scores(0)
    for m in range(n_maps):
        s = s_next
        if m + 1 < n_maps:
            s_next = scores(m + 1)
        e = jnp.exp(s - jnp.max(s, axis=0, keepdims=True))
        vt_h = v_ref[0, (m // 2) * 128:(m // 2 + 1) * 128, :]
        acc = jnp.dot(vt_h, e, preferred_element_type=F32)
        maps.append(acc / acc[HEAD_DIM:HEAD_DIM + 1, :])
    outs = [(maps[2 * h] - lam * maps[2 * h + 1])[:HEAD_DIM] for h in range(N_HEADS)]
    o = jnp.concatenate(outs, axis=0).T
    g64 = _group_ones(W_GROUP, HEAD_DIM)
    o_ref[0] = _group_rms(o, g64, HEAD_DIM, sw_ref[...]) * (1.0 - lam_init)


def _da_attention(qh, kh, vh, lam_p, sw_row, lam_init, q_tile, q_blocks, q_off, k_rows, k_blk, name):
    b, lt, _ = qh.shape
    kern = functools.partial(_da_attn_kernel, lam_init=lam_init)
    return pl.pallas_call(
        kern,
        out_shape=jax.ShapeDtypeStruct((b, q_blocks * q_tile, W_GROUP), F32),
        grid=(b, q_blocks),
        in_specs=[pl.BlockSpec((1, q_tile, W_GROUP), lambda i, t: (i, t + q_off, 0)),
                  pl.BlockSpec((1, k_rows, W_GROUP), lambda i, t: (i, k_blk, 0)),
                  pl.BlockSpec((1, 2 * W_GROUP, k_rows), lambda i, t: (i, 0, k_blk)),
                  pl.BlockSpec((4, DA_DK), lambda i, t: (0, 0)),
                  pl.BlockSpec((1, W_GROUP), lambda i, t: (0, 0))],
        out_specs=pl.BlockSpec((1, q_tile, W_GROUP), lambda i, t: (i, t, 0)),
        compiler_params=_cparams(2), name=name,
    )(qh, kh, vh, lam_p, sw_row)


def _hy_conv_kernel(p_ref, hp_ref, hn_ref, cw_ref, v_ref, x1_ref, x2_ref, xp_ref, *, nbl):
    blk = pl.program_id(1)
    xc = _short_conv(p_ref[0], hp_ref.at[0], hn_ref.at[0], cw_ref, xp_ref, blk, nbl, 3 * W_GROUP)
    v_ref[0] = xc[:, :W_GROUP].astype(v_ref.dtype)
    x1_ref[0] = xc[:, W_GROUP:2 * W_GROUP].astype(x1_ref.dtype)
    x2_ref[0] = xc[:, 2 * W_GROUP:].astype(x2_ref.dtype)


def _hy_short_conv(p_hy, conv_w, nbl):
    b, lt, width = p_hy.shape
    nb = nbl + 1
    blk = lambda: pl.BlockSpec((1, TOKEN_BLOCK, W_GROUP), lambda i, t: (i, t, 0))
    return pl.pallas_call(
        functools.partial(_hy_conv_kernel, nbl=nbl),
        out_shape=[jax.ShapeDtypeStruct((b, lt, W_GROUP), BF16)] * 3,
        grid=(b, nb),
        in_specs=[pl.BlockSpec((1, TOKEN_BLOCK, width), lambda i, t: (i, t, 0))]
                 + _halo_specs(width, nbl, lambda s: s, lt)
                 + [pl.BlockSpec((3, width), lambda i, t: (0, 0))],
        out_specs=[blk(), blk(), blk()],
        scratch_shapes=[pltpu.VMEM((TOKEN_BLOCK + 16, width), F32)],
        compiler_params=_cparams(2), name="hy_short_conv",
    )(p_hy, p_hy, p_hy, conv_w)


def _hy_filter_kernel(z_ref, w1_ref, b1_ref, f1_ref, w2_ref, b2_ref, f2_ref, w3_ref, dec_ref, o_ref):
    z = z_ref[...]
    h = jnp.sin(f1_ref[...] * (_hp_dot(z, w1_ref[...]) + b1_ref[...]))
    h = jnp.sin(f2_ref[...] * (_hp_dot(h, w2_ref[...]) + b2_ref[...]))
    h = _hp_dot(h, w3_ref[...])
    h = h * jnp.exp(-z[:, 0:1] * jnp.abs(dec_ref[...]))
    first = (pl.program_id(0) * h.shape[0] + lax.broadcasted_iota(jnp.int32, (h.shape[0], 1), 0)) == 0
    for plane in range(4):
        piece = h[:, plane * W_GROUP:(plane + 1) * W_GROUP]
        o_ref[plane] = jnp.where(first, 0.0, piece) if plane >= 2 else piece


def _hy_filters(zfeat, w1p, b1, f1, w2, b2, f2, w3, dec_row):
    n = zfeat.shape[0]
    tn = min(n, 512)
    full = lambda a: pl.BlockSpec(a.shape, lambda i: (0,) * a.ndim)
    args = [w1p, b1, f1, w2, b2, f2, w3, dec_row]
    return pl.pallas_call(
        _hy_filter_kernel,
        out_shape=jax.ShapeDtypeStruct((4, n, W_GROUP), F32),
        grid=(n // tn,),
        in_specs=[pl.BlockSpec((tn, 128), lambda i: (i, 0))] + [full(a) for a in args],
        out_specs=pl.BlockSpec((4, tn, W_GROUP), lambda i: (0, i, 0)),
        compiler_params=_cparams(1), name="hy_filters",
    )(zfeat, *args)


def _left_split_dot(a, v):
    a_hi = a.astype(BF16)
    a_lo = (a - a_hi.astype(F32)).astype(BF16)
    return jnp.dot(a_hi, v, preferred_element_type=F32) + jnp.dot(a_lo, v, preferred_element_type=F32)


def _fft_stage1_kernel(a_ref, u_ref, o_ref):
    u = u_ref[0]
    y = _left_split_dot(a_ref[...], u) if u.dtype == BF16 else _hp_dot(a_ref[...], u)
    o_ref[0] = y.astype(o_ref.dtype)


def _fft_stage1(a_mat, u_view, row_blk, col_tile):
    nbatch, _, cols = u_view.shape
    m, kdim = a_mat.shape
    return pl.pallas_call(
        _fft_stage1_kernel,
        out_shape=jax.ShapeDtypeStruct((nbatch, m, cols), BF16),
        grid=(nbatch, cols // col_tile),
        in_specs=[pl.BlockSpec((m, kdim), lambda i, j: (0, 0)),
                  pl.BlockSpec((1, kdim, col_tile), lambda i, j: (i, row_blk, j))],
        out_specs=pl.BlockSpec((1, m, col_tile), lambda i, j: (i, 0, j)),
        compiler_params=_cparams(2), name="fft_stage1",
    )(a_mat, u_view)


def _fft_mid_kernel(*refs, n2, multiply):
    if multiply:
        y_ref, g_ref, kf_ref, kb_ref, gi_ref, o_ref = refs
    else:
        y_ref, g_ref, o_ref = refs
    group = range(g_ref.shape[0])
    z = [_bdot(g_ref[j], y_ref[0, :, j].reshape(2 * n2, W_GROUP)) for j in group]
    if multiply:
        prod = []
        for j in group:
            zr, zi = z[j][:n2], z[j][n2:]
            kr = kf_ref[0, 0, j].astype(F32) + kb_ref[0, 0, j].astype(F32)
            ki = kf_ref[0, 1, j].astype(F32) - kb_ref[0, 1, j].astype(F32)
            prod.append(jnp.concatenate([zr * kr - zi * ki, zr * ki + zi * kr], axis=0))
        z = [_bdot(gi_ref[j], prod[j]) for j in group]
    for j in group:
        o_ref[0, :, j] = z[j].reshape(2, n2, W_GROUP).astype(o_ref.dtype)


def _fft_mid(y1, g_fwd, kf, filt, g_inv, k1p, n2):
    nbatch = y1.shape[0]
    multiply = kf is not None
    group = k1p // 2
    data = lambda plane: pl.BlockSpec((1, 2, group, n2, W_GROUP),
                                      (lambda k, i: (i, 0, k, 0, 0)) if plane is None
                                      else (lambda k, i: (plane, 0, k, 0, 0)))
    mat = lambda: pl.BlockSpec((group, 2 * n2, 2 * n2), lambda k, i: (k, 0, 0))
    in_specs, args = [data(None), mat()], [y1, g_fwd]
    if multiply:
        in_specs += [data(filt), data(2 + filt), mat()]
        args += [kf, kf, g_inv]
    return pl.pallas_call(
        functools.partial(_fft_mid_kernel, n2=n2, multiply=multiply),
        out_shape=jax.ShapeDtypeStruct((nbatch, 2, k1p, n2, W_GROUP), BF16),
        grid=(k1p // group, nbatch),
        in_specs=in_specs,
        out_specs=data(None),
        compiler_params=_cparams(2), name="fft_mid_mul" if multiply else "fft_mid",
    )(*args)


def _fft_stage4_kernel(a_ref, v_ref, u_ref, gate_ref, skip_ref, o_ref):
    y = _left_split_dot(a_ref[...], v_ref[0])
    o_ref[0] = (gate_ref[0].astype(F32) * (y + skip_ref[...] * u_ref[0].astype(F32))).astype(o_ref.dtype)


def _fft_stage4(a_mat, v_view, u_view, u_blk, gate_view, gate_blk, skip_row, col_tile):
    nbatch, kdim, cols = v_view.shape
    m = a_mat.shape[0]
    return pl.pallas_call(
        _fft_stage4_kernel,
        out_shape=jax.ShapeDtypeStruct((nbatch, m, cols), BF16),
        grid=(nbatch, cols // col_tile),
        in_specs=[pl.BlockSpec((m, kdim), lambda i, j: (0, 0)),
                  pl.BlockSpec((1, kdim, col_tile), lambda i, j: (i, 0, j)),
                  pl.BlockSpec((1, m, col_tile), lambda i, j: (i, u_blk, j)),
                  pl.BlockSpec((1, m, col_tile), lambda i, j: (i, gate_blk, j)),
                  pl.BlockSpec((1, col_tile), lambda i, j: (0, j))],
        out_specs=pl.BlockSpec((1, m, col_tile), lambda i, j: (i, 0, j)),
        compiler_params=_cparams(2), name="fft_stage4",
    )(a_mat, v_view, u_view, gate_view, skip_row)


def _fft_tables(n, n2):
    big = 2 * n
    n1 = big // n2
    two_pi = 2.0 * math.pi

    def cs(num, den):
        ang = (num % den).astype(F32) * (two_pi / den)
        return jnp.cos(ang), jnp.sin(ang)

    k1p = -(-(n1 // 2 + 1) // FFT_K1_GROUP) * FFT_K1_GROUP
    k1 = jnp.arange(k1p, dtype=jnp.int32)
    live = (k1 <= n1 // 2).astype(F32)
    weight = jnp.where((k1 == 0) | (k1 == n1 // 2), 1.0, 2.0) * live
    m1 = jnp.arange(n1, dtype=jnp.int32)
    c, s = cs(k1[:, None] * m1[None, :], n1)
    a1_full = jnp.concatenate([c * live[:, None], -s * live[:, None]], axis=0)
    k2 = jnp.arange(n2, dtype=jnp.int32)
    num = (k2[None, :, None] * k2[None, None, :] * n1 + k1[:, None, None] * k2[None, None, :]) % big
    c, s = cs(num, big)
    g_fwd = jnp.concatenate([jnp.concatenate([c, s], axis=2), jnp.concatenate([-s, c], axis=2)], axis=1)
    num = (k2[None, :, None] * k2[None, None, :] * n1 + k1[:, None, None] * k2[None, :, None]) % big
    c, s = cs(num, big)
    g_inv = jnp.concatenate([jnp.concatenate([c, -s], axis=2), jnp.concatenate([s, c], axis=2)], axis=1)
    t1 = jnp.arange(n1 // 2, dtype=jnp.int32)
    c, s = cs(t1[:, None] * k1[None, :], n1)
    a4 = jnp.concatenate([c * weight[None, :], -s * weight[None, :]], axis=1) * (1.0 / big)
    return n1, k1p, a1_full, g_fwd.astype(BF16), g_inv.astype(BF16), a4


def _hy_positional(n):
    pos = jnp.arange(n, dtype=F32)
    t01 = pos / max(n - 1, 1)
    bands = jnp.linspace(1e-4, HY_BANDS - 1, HY_BANDS, dtype=F32)
    ang = (2.0 * math.pi / n) * pos[:, None] * bands
    z = jnp.concatenate([t01[:, None], jnp.cos(ang), -jnp.sin(ang)], axis=-1)
    return jnp.pad(z, ((0, 0), (0, 128 - z.shape[1])))


def _hyena_segment(v_all, x1_all, x2_all, seg_start, n, n2, filt_args, skip):
    b, lt, c = v_all.shape
    n1, k1p, a1_full, g_fwd, g_inv, a4 = _fft_tables(n, n2)
    a1_half = a1_full[:, :n1 // 2]
    cols = n2 * c
    col_tile = min(cols, 16384)
    seg_blk = seg_start // n
    view = lambda a: a.reshape(b, lt // n2, cols)

    h = _hy_filters(_hy_positional(n), *filt_args)
    ky = _fft_stage1(a1_half, h.reshape(4, n1 // 2, cols), 0, col_tile)
    kf = _fft_mid(ky.reshape(4, 2, k1p, n2, c), g_fwd, None, 0, None, k1p, n2)

    def long_conv(u_view, u_blk, gate_all, filt):
        y1 = _fft_stage1(a1_half, u_view, u_blk, col_tile)
        vmid = _fft_mid(y1.reshape(b, 2, k1p, n2, c), g_fwd, kf, filt, g_inv, k1p, n2)
        skip_row = jnp.tile(skip[filt], n2).reshape(1, cols)
        return _fft_stage4(a4, vmid.reshape(b, 2 * k1p, cols), u_view, u_blk, view(gate_all), seg_blk,
                           skip_row, col_tile)

    z = long_conv(view(v_all), seg_blk, x1_all, 0)
    out = long_conv(z, 0, x2_all, 1)
    return out.reshape(b, n, c)


def _rope_tables(rows, lt):
    n_freq = DA_DK // 4
    inv = ROPE_THETA ** (-jnp.arange(n_freq, dtype=F32) / n_freq)
    r = jnp.repeat(jnp.arange(rows, dtype=F32), GRID_W)
    col = jnp.tile(jnp.arange(GRID_W, dtype=F32), rows)
    ang_r, ang_c = r[:, None] * inv, col[:, None] * inv
    cos32 = jnp.concatenate([jnp.cos(ang_r)] * 2 + [jnp.cos(ang_c)] * 2, axis=-1)
    sin32 = jnp.concatenate([-jnp.sin(ang_r), jnp.sin(ang_r), -jnp.sin(ang_c), jnp.sin(ang_c)], axis=-1)
    n = rows * GRID_W
    pad = lt - n
    cos = jnp.concatenate([jnp.tile(cos32, (1, W_GROUP // DA_DK)), jnp.ones((pad, W_GROUP), F32)], axis=0)
    sin = jnp.concatenate([jnp.tile(sin32, (1, W_GROUP // DA_DK)), jnp.zeros((pad, W_GROUP), F32)], axis=0)
    return cos, sin


def kernel(x, c, ctx, c_ctx, mod_w, mod_b, ln1_w, ln2_w, w_in, gdn_conv_w, gdn_a_log, gdn_dt_bias, gdn_norm_w, hy_conv_w, hy_w1, hy_b1, hy_f1, hy_w2, hy_b2, hy_f2, hy_w3, hy_decay, hy_bias, hg_lb_raw, hg_norm_w, da_q_norm, da_k_norm, da_lam, da_subln, w_out, mlp_w1, mlp_w2):
    batch, seq, d = x.shape
    n_ctx = ctx.shape[1]
    depth = mod_w.shape[0]
    assert n_ctx == TOKEN_BLOCK and seq % TOKEN_BLOCK == 0 and d == 4 * W_GROUP
    nbl = seq // TOKEN_BLOCK
    lt = seq + n_ctx

    xs = jnp.concatenate([x, ctx], axis=1)
    cond = jnp.concatenate([c, c_ctx[None, :], jnp.zeros((-(batch + 1) % 8, d), F32)], axis=0)
    mods = _modulation(cond, mod_w, mod_b)[:, :batch + 1].reshape(depth, batch + 1, 6, d)
    mods = jnp.pad(mods, ((0, 0), (0, 0), (0, 2), (0, 0)))

    lb_all = jnp.cumsum(jax.nn.softmax(hg_lb_raw.astype(F32), axis=0), axis=0)
    lb_all = lb_all - lb_all[0]
    cos_tab, sin_tab = _rope_tables(seq // GRID_W, lt)

    gdn_cols = 4 * W_GROUP + 4 * N_HEADS
    hy_cols, hg_cols, da_cols = 3 * W_GROUP, 5 * W_GROUP, 3 * W_GROUP
    splits = (GDN_PAD_COLS, hy_cols, hg_cols, da_cols)

    for l in range(depth):
        want_ctx = l < depth - 1
        modtab = mods[l]
        wl = w_in[l]
        w_pad = jnp.concatenate([wl[:, :gdn_cols], jnp.zeros((d, GDN_PAD_COLS - gdn_cols), F32),
                                 wl[:, gdn_cols:]], axis=1).astype(BF16)
        qw = jnp.tile(da_q_norm[l], W_GROUP // DA_DK).reshape(1, W_GROUP)
        kw = jnp.tile(da_k_norm[l], W_GROUP // DA_DK).reshape(1, W_GROUP)
        p_gdn, p_hy, p_hg, qh, kh, vh = _in_proj(xs, modtab, ln1_w[l], w_pad, splits, cos_tab, sin_tab, qw, kw, nbl)

        pad16 = lambda a: jnp.pad(a.reshape(1, 2 * N_HEADS), ((0, 0), (0, 128 - 2 * N_HEADS)))
        alog_row, dtb_row = pad16(gdn_a_log[l]), pad16(gdn_dt_bias[l])
        gnw = jnp.tile(gdn_norm_w[l], N_HEADS).reshape(1, W_GROUP)
        o_b = _gdn_direction(p_gdn, gdn_conv_w[l], alog_row, dtb_row, gnw, None, 1, nbl)
        o_gdn = _gdn_direction(p_gdn, gdn_conv_w[l], alog_row, dtb_row, gnw, o_b, 0, nbl)

        hv, hx1, hx2 = _hy_short_conv(p_hy, hy_conv_w[l], nbl)
        w1p = jnp.pad(hy_w1[l], ((0, 128 - hy_w1.shape[1]), (0, 0)))
        row = lambda a: a.reshape(1, -1)
        dec_row = jnp.tile(hy_decay[l].reshape(1, 2 * W_GROUP), (1, 2))
        filt_args = (w1p, row(hy_b1[l]), row(hy_f1[l]), hy_w2[l], row(hy_b2[l]), row(hy_f2[l]), hy_w3[l], dec_row)
        o_hy_lat = _hyena_segment(hv, hx1, hx2, 0, seq, FFT_N2_LAT, filt_args, hy_bias[l])
        o_hy_ctx = (_hyena_segment(hv, hx1, hx2, seq, n_ctx, FFT_N2_CTX, filt_args, hy_bias[l])
                    if want_ctx else None)

        lb_row = lb_all[l].reshape(1, W_GROUP)
        hnw = jnp.tile(hg_norm_w[l], N_HEADS).reshape(1, W_GROUP)
        o_hb = _hgrn_direction(p_hg, lb_row, hnw, None, 1, nbl)
        o_hg = _hgrn_direction(p_hg, lb_row, hnw, o_hb, 0, nbl)

        lam_init = 0.8 - 0.6 * math.exp(-0.3 * l)
        sw = jnp.tile(da_subln[l], N_HEADS).reshape(1, W_GROUP)
        o_da = _da_attention(qh, kh, vh, da_lam[l], sw, lam_init, DA_Q_TILE, seq // DA_Q_TILE, 0, lt, 0,
                             "da_attn_lat")
        o_da_ctx = (_da_attention(qh, kh, vh, da_lam[l], sw, lam_init, TOKEN_BLOCK, 1, nbl, TOKEN_BLOCK, nbl,
                                  "da_attn_ctx") if want_ctx else None)

        n_blocks = nbl + 1 if want_ctx else nbl
        xs_mid = _proj_residual(xs, modtab, w_out[l].astype(BF16),
                                [o_gdn, (o_hy_lat, o_hy_ctx), o_hg, (o_da, o_da_ctx)], 2, n_blocks, nbl, "out_proj")
        (hid,) = _adaln_proj(xs_mid, modtab, ln2_w[l], mlp_w1[l].astype(BF16), (mlp_w1.shape[2],), (BF16,),
                             3, 4, True, nbl, "mlp_up")
        xs = _proj_residual(xs_mid, modtab, mlp_w2[l].astype(BF16), [hid], 5, n_blocks, nbl, "mlp_down")
    return xs
```

```python
import functools
import math

import jax
import jax.numpy as jnp
from jax import lax
from jax.experimental import pallas as pl
from jax.experimental.pallas import tpu as pltpu

F32 = jnp.float32
BF16 = jnp.bfloat16

EPS = 1e-6
TOKEN_BLOCK = 256
CHUNK = 64
HEAD_DIM = 64
N_HEADS = 4
W_GROUP = 256
DA_DK = 32
GRID_W = 64
ROPE_THETA = 10000.0
HY_BANDS = 16
FFT_N2_LAT = 128
FFT_N2_CTX = 16
FFT_K1_GROUP = 8
DA_Q_TILE = 256
SCAN_ROWS = 4
GDN_PAD_COLS = 1152
VMEM_LIMIT_BYTES = 56 * 1024 * 1024


def _cparams(n_axes):
    return pltpu.CompilerParams(dimension_semantics=("arbitrary",) * n_axes,
                                vmem_limit_bytes=VMEM_LIMIT_BYTES)


def _sigmoid(x):
    return 1.0 / (1.0 + jnp.exp(-x))


def _silu(x):
    return x * _sigmoid(x)


def _softplus(x):
    return jnp.maximum(x, 0.0) + jnp.log(1.0 + jnp.exp(-jnp.abs(x)))


def _log_sigmoid(x):
    return jnp.minimum(x, 0.0) - jnp.log(1.0 + jnp.exp(-jnp.abs(x)))


def _bdot(a, b):
    return jnp.dot(a.astype(BF16), b.astype(BF16), preferred_element_type=F32)


def _bdot_nt(a, b):
    return lax.dot_general(a.astype(BF16), b.astype(BF16), (((1,), (1,)), ((), ())),
                           preferred_element_type=F32)


def _split3(x):
    hi = x.astype(BF16)
    r = x - hi.astype(F32)
    mid = r.astype(BF16)
    lo = (r - mid.astype(F32)).astype(BF16)
    return hi, mid, lo


def _exact_left(m_bf16, x):
    hi, mid, lo = _split3(x)
    d = lambda p: jnp.dot(m_bf16, p, preferred_element_type=F32)
    return d(hi) + d(mid) + d(lo)


def _hp_dot(a, b):
    ah = a.astype(BF16)
    al = (a - ah.astype(F32)).astype(BF16)
    bh = b.astype(BF16)
    bl = (b - bh.astype(F32)).astype(BF16)
    d = lambda p, q: jnp.dot(p, q, preferred_element_type=F32)
    return d(ah, bh) + d(ah, bl) + d(al, bh)


def _group_ones(width, group):
    r = lax.broadcasted_iota(jnp.int32, (width, width), 0) // group
    c = lax.broadcasted_iota(jnp.int32, (width, width), 1) // group
    return (r == c).astype(BF16)


def _mod_kernel(s_ref, w_ref, b_ref, o_ref):
    s = _silu(s_ref[...])
    o_ref[0] = _bdot(s, w_ref[0]) + b_ref[0]


def _modulation(cond, mod_w, mod_b):
    depth, d, n = mod_w.shape
    rows = cond.shape[0]
    tn = 1536
    return pl.pallas_call(
        _mod_kernel,
        out_shape=jax.ShapeDtypeStruct((depth, rows, n), F32),
        grid=(depth, n // tn),
        in_specs=[pl.BlockSpec((rows, d), lambda l, j: (0, 0)),
                  pl.BlockSpec((1, d, tn), lambda l, j: (l, 0, j)),
                  pl.BlockSpec((1, 1, tn), lambda l, j: (l, 0, j))],
        out_specs=pl.BlockSpec((1, rows, tn), lambda l, j: (l, 0, j)),
        compiler_params=_cparams(2), name="modulation",
    )(cond, mod_w, mod_b.reshape(depth, 1, n))


def _adaln_proj_kernel(x_ref, mod_ref, lnw_ref, w_ref, *out_refs, shift_row, scale_row, splits, sq_relu):
    x = x_ref[0]
    ms = jnp.mean(x * x, axis=-1, keepdims=True)
    y = x * lax.rsqrt(ms + EPS) * lnw_ref[...]
    mod = mod_ref[0]
    y = y * (1.0 + mod[scale_row:scale_row + 1]) + mod[shift_row:shift_row + 1]
    h = _bdot(y, w_ref[...])
    if sq_relu:
        h = jnp.square(jnp.maximum(h, 0.0))
    off = 0
    for o_ref, n in zip(out_refs, splits):
        o_ref[0] = h[:, off:off + n].astype(o_ref.dtype)
        off += n


def _mod_index(nbl, batch):
    return lambda b, t: (jnp.where(t == nbl, batch, b), 0, 0)


def _adaln_proj(xs, modtab, ln_w, w, splits, out_dtypes, shift_row, scale_row, sq_relu, nbl, name):
    b, lt, d = xs.shape
    nb = lt // TOKEN_BLOCK
    n = w.shape[1]
    kern = functools.partial(_adaln_proj_kernel, shift_row=shift_row, scale_row=scale_row,
                             splits=splits, sq_relu=sq_relu)
    return pl.pallas_call(
        kern,
        out_shape=[jax.ShapeDtypeStruct((b, lt, s), dt) for s, dt in zip(splits, out_dtypes)],
        grid=(b, nb),
        in_specs=[pl.BlockSpec((1, TOKEN_BLOCK, d), lambda i, t: (i, t, 0)),
                  pl.BlockSpec((1, 8, d), _mod_index(nbl, b)),
                  pl.BlockSpec((1, d), lambda i, t: (0, 0)),
                  pl.BlockSpec((d, n), lambda i, t: (0, 0))],
        out_specs=[pl.BlockSpec((1, TOKEN_BLOCK, s), lambda i, t: (i, t, 0)) for s in splits],
        compiler_params=_cparams(2), name=name,
    )(xs, modtab, ln_w.reshape(1, d), w)


def _proj_residual_kernel(x_ref, mod_ref, w_ref, *rest, gate_row, split, nbl):
    a_refs, o_ref = rest[:-1], rest[-1]
    is_ctx = pl.program_id(1) == nbl
    pieces, pos = [], 0
    for two in split:
        lat = a_refs[pos][0].astype(BF16)
        pieces.append(jnp.where(is_ctx, a_refs[pos + 1][0].astype(BF16), lat) if two else lat)
        pos += 2 if two else 1
    a = jnp.concatenate(pieces, axis=1) if len(pieces) > 1 else pieces[0]
    gate = mod_ref[0][gate_row:gate_row + 1]
    o_ref[0] = x_ref[0] + gate * _bdot(a, w_ref[...])


def _proj_residual(xs, modtab, w, acts, gate_row, n_blocks, nbl, name):
    b, lt, d = xs.shape
    split = tuple(isinstance(a, tuple) and a[1] is not None for a in acts)
    specs, args = [], []
    for a, two in zip(acts, split):
        lat = a[0] if isinstance(a, tuple) else a
        width = lat.shape[2]
        if isinstance(a, tuple):
            specs.append(pl.BlockSpec((1, TOKEN_BLOCK, width), lambda i, t: (i, jnp.minimum(t, nbl - 1), 0)))
        else:
            specs.append(pl.BlockSpec((1, TOKEN_BLOCK, width), lambda i, t: (i, t, 0)))
        args.append(lat)
        if two:
            specs.append(pl.BlockSpec((1, TOKEN_BLOCK, width), lambda i, t: (i, 0, 0)))
            args.append(a[1])
    kern = functools.partial(_proj_residual_kernel, gate_row=gate_row, split=split, nbl=nbl)
    return pl.pallas_call(
        kern,
        out_shape=jax.ShapeDtypeStruct((b, n_blocks * TOKEN_BLOCK, d), F32),
        grid=(b, n_blocks),
        in_specs=[pl.BlockSpec((1, TOKEN_BLOCK, d), lambda i, t: (i, t, 0)),
                  pl.BlockSpec((1, 8, d), _mod_index(nbl, b)),
                  pl.BlockSpec(w.shape, lambda i, t: (0, 0))] + specs,
        out_specs=pl.BlockSpec((1, TOKEN_BLOCK, d), lambda i, t: (i, t, 0)),
        compiler_params=_cparams(2), name=name,
    )(xs, modtab, w, *args)


def _scan_block(step, nbl, direction):
    lat = step - 1 if direction == 0 else nbl - step
    return jnp.where(step == 0, nbl, lat)


def _halo_specs(width, nbl, block_of_step, lt, batch_rows=1):
    rows8 = TOKEN_BLOCK // 8

    def prev_map(b, s):
        return (b, jnp.maximum(block_of_step(s) * rows8 - 1, 0), 0)

    def next_map(b, s):
        return (b, jnp.minimum((block_of_step(s) + 1) * rows8, lt // 8 - 1), 0)

    return [pl.BlockSpec((batch_rows, 8, width), prev_map), pl.BlockSpec((batch_rows, 8, width), next_map)]


def _short_conv(cur, hp_ref, hn_ref, cw_ref, xp_ref, blk, nbl, width):
    seg_first = jnp.logical_or(blk == 0, blk == nbl)
    seg_last = jnp.logical_or(blk == nbl - 1, blk == nbl)
    xp_ref[0:8, :] = jnp.where(seg_first, 0.0, hp_ref[:, :width])
    xp_ref[8:8 + TOKEN_BLOCK, :] = cur
    xp_ref[8 + TOKEN_BLOCK:16 + TOKEN_BLOCK, :] = jnp.where(seg_last, 0.0, hn_ref[:, :width])
    cw = cw_ref[...]
    return (cw[0:1] * xp_ref[7:7 + TOKEN_BLOCK, :] + cw[1:2] * xp_ref[8:8 + TOKEN_BLOCK, :]
            + cw[2:3] * xp_ref[9:9 + TOKEN_BLOCK, :])


def _scan_tables(direction, chunk):
    i = jnp.arange(TOKEN_BLOCK, dtype=jnp.int32)[:, None]
    j = jnp.arange(TOKEN_BLOCK, dtype=jnp.int32)[None, :]
    same = (i // chunk) == (j // chunk)
    pos = (lambda t: t % chunk) if direction == 0 else (lambda t: chunk - 1 - t % chunk)
    pi, pj = pos(i), pos(j)
    incl = same & (pj <= pi)
    tables = [incl, (i // HEAD_DIM) == (j // HEAD_DIM), i == j]
    m = 1
    while m < chunk:
        boundary = (pi // (2 * m)) * (2 * m) + m - 1
        tables.append(incl.astype(F32) - (same & (pj <= boundary)).astype(F32))
        m *= 2
    return jnp.stack([a.astype(BF16) for a in tables])


def _group_rms(o, gmat, group, weight):
    ms = jnp.dot((o * o).astype(BF16), gmat, preferred_element_type=F32) * (1.0 / group)
    return o * lax.rsqrt(ms + EPS) * weight


def _gdn_kernel(*refs, direction, finish, nbl):
    if finish:
        (p_ref, hp_ref, hn_ref, cw_ref, alog_ref, dtb_ref, nw_ref, bm_ref, other_ref,
         o_ref, s_ref, xp_ref, ob_ref) = refs
    else:
        p_ref, hp_ref, hn_ref, cw_ref, alog_ref, dtb_ref, nw_ref, bm_ref, o_ref, s_ref, xp_ref, ob_ref = refs
    step = pl.program_id(1)
    blk = _scan_block(step, nbl, direction)

    @pl.when(step == 0)
    def _():
        s_ref[...] = jnp.zeros_like(s_ref)

    g64 = bm_ref[1]
    rows = range(SCAN_ROWS)
    p_rows, q_rows, k_rows, v_rows, kt_rows, zc_rows, zr_rows = [], [], [], [], [], [], []
    for r in rows:
        p = p_ref[r]
        xc = _silu(_short_conv(p[:, :3 * W_GROUP], hp_ref.at[r], hn_ref.at[r], cw_ref, xp_ref.at[r], blk, nbl,
                               3 * W_GROUP))
        q = xc[:, :W_GROUP]
        k = xc[:, W_GROUP:2 * W_GROUP]
        q = q * lax.rsqrt(_bdot(q * q, g64) + EPS) * (HEAD_DIM ** -0.5)
        k = k * lax.rsqrt(_bdot(k * k, g64) + EPS)
        ab = p[:, 4 * W_GROUP:4 * W_GROUP + 128]
        lane = lax.broadcasted_iota(jnp.int32, ab.shape, 1)
        g_all = -jnp.exp(alog_ref[...]) * _softplus(ab + dtb_ref[...])
        w_all = jnp.where(lane < 2 * N_HEADS, g_all, _sigmoid(ab))
        z_col = jnp.where(lane < 2 * N_HEADS, _exact_left(bm_ref[0], w_all), w_all)
        p_rows.append(p)
        q_rows.append(q)
        k_rows.append(k)
        v_rows.append(xc[:, 2 * W_GROUP:])
        kt_rows.append(_bdot_nt(bm_ref[2], k))
        zc_rows.append(z_col)
        zr_rows.append(z_col.T)

    n_chunks = TOKEN_BLOCK // CHUNK
    problems = [(r, h) for r in rows for h in range(N_HEADS)]
    heads = range(len(problems))
    lanes = [slice(h * HEAD_DIM, (h + 1) * HEAD_DIM) for _, h in problems]
    gidx = [N_HEADS * direction + h for _, h in problems]
    q = [q_rows[r] for r, _ in problems]
    k = [k_rows[r] for r, _ in problems]
    v = [v_rows[r] for r, _ in problems]
    k_t = [kt_rows[r] for r, _ in problems]
    z_col = [zc_rows[r] for r, _ in problems]
    z_row = [zr_rows[r] for r, _ in problems]
    ci_ = lax.broadcasted_iota(jnp.int32, (CHUNK, CHUNK), 0)
    cj_ = lax.broadcasted_iota(jnp.int32, (CHUNK, CHUNK), 1)
    pi_, pj_ = (ci_, cj_) if direction == 0 else (CHUNK - 1 - ci_, CHUNK - 1 - cj_)
    incl = pj_ <= pi_
    strict_f = (pj_ < pi_).astype(F32)
    eye_f = (ci_ == cj_).astype(F32)

    def half_f(m):
        return (((pi_ // (2 * m)) == (pj_ // (2 * m))) & ((pi_ // m) % 2 == 1) & ((pj_ // m) % 2 == 0)).astype(F32)

    chunked = lambda a: a.reshape(n_chunks, CHUNK, a.shape[1])
    bmm = lambda a, b: jnp.einsum("cij,cjk->cik", a.astype(BF16), b.astype(BF16), preferred_element_type=F32)
    bmm_nt = lambda a, b: jnp.einsum("cid,cjd->cij", a.astype(BF16), b.astype(BF16), preferred_element_type=F32)
    gc_c = [chunked(z_col[i][:, gi:gi + 1]) for i, gi in enumerate(gidx)]
    gc_r = [jnp.stack([z_row[i][gi:gi + 1, c * CHUNK:(c + 1) * CHUNK] for c in range(n_chunks)])
            for i, gi in enumerate(gidx)]
    beta = [chunked(z_col[i][:, 2 * N_HEADS + gi:2 * N_HEADS + gi + 1]) for i, gi in enumerate(gidx)]
    qc = [chunked(q[i][:, ls]) for i, ls in enumerate(lanes)]
    kc = [chunked(k[i][:, ls]) for i, ls in enumerate(lanes)]
    vc = [chunked(v[i][:, ls]) for i, ls in enumerate(lanes)]
    eg = [jnp.exp(g) for g in gc_c]
    decay = [jnp.exp(jnp.where(incl, gc_c[h] - gc_r[h], -jnp.inf)) for h in heads]
    kk = [bmm_nt(kc[h], kc[h]) for h in heads]
    qk = [bmm_nt(qc[h], kc[h]) for h in heads]
    n = [kk[h] * beta[h] * decay[h] * strict_f for h in heads]
    inv = [eye_f - n[h] * half_f(1) for h in heads]
    n_bf = [n[h].astype(BF16) for h in heads]
    m = 2
    while m < CHUNK:
        half = half_f(m).astype(BF16)
        x = [bmm(inv[h], n_bf[h] * half) for h in heads]
        inv = [inv[h] - bmm(x[h], inv[h]) for h in heads]
        m *= 2
    uw = [bmm(inv[h], jnp.concatenate([vc[h] * beta[h], kc[h] * (beta[h] * eg[h])], axis=2)) for h in heads]
    attn = [qk[h] * decay[h] for h in heads]
    qg = [qc[h] * eg[h] for h in heads]
    state = [s_ref[h] for h in heads]

    last = CHUNK - 1 if direction == 0 else 0
    for ci in range(n_chunks):
        c = ci if direction == 0 else n_chunks - 1 - ci
        rs = slice(c * CHUNK, (c + 1) * CHUNK)
        g_last = [gc_c[h][c, last:last + 1, :] for h in heads]
        v_new = [uw[h][c, :, :HEAD_DIM] - _bdot(uw[h][c, :, HEAD_DIM:], state[h]) for h in heads]
        for h, ls in enumerate(lanes):
            ob_ref[problems[h][0], rs, ls] = _bdot(qg[h][c], state[h]) + _bdot(attn[h][c], v_new[h])
        state = [state[h] * jnp.exp(g_last[h])
                 + _bdot(k_t[h][ls, rs] * jnp.exp(g_last[h] - gc_r[h][c]), v_new[h]) for h, ls in enumerate(lanes)]
    for h in heads:
        s_ref[h] = state[h]

    for r in rows:
        if finish:
            o = ob_ref[r] + other_ref[r]
            gate = _silu(p_rows[r][:, 3 * W_GROUP:4 * W_GROUP])
            o_ref[r] = _group_rms(o, g64, HEAD_DIM, nw_ref[...]) * gate
        else:
            o_ref[r] = ob_ref[r]


def _gdn_direction(p_gdn, conv_w, alog_row, dtb_row, nw_row, other, direction, nbl):
    b, lt, width = p_gdn.shape
    nb = nbl + 1
    blk_of = lambda s: _scan_block(s, nbl, direction)
    finish = other is not None
    bmask = _scan_tables(direction, CHUNK)[:3]
    assert b % SCAN_ROWS == 0
    kern = functools.partial(_gdn_kernel, direction=direction, finish=finish, nbl=nbl)
    in_specs = ([pl.BlockSpec((SCAN_ROWS, TOKEN_BLOCK, width), lambda i, s: (i, blk_of(s), 0))]
                + _halo_specs(3 * W_GROUP, nbl, blk_of, lt, SCAN_ROWS)
                + [pl.BlockSpec((3, 3 * W_GROUP), lambda i, s: (0, 0)),
                   pl.BlockSpec((1, 128), lambda i, s: (0, 0)),
                   pl.BlockSpec((1, 128), lambda i, s: (0, 0)),
                   pl.BlockSpec((1, W_GROUP), lambda i, s: (0, 0)),
                   pl.BlockSpec(bmask.shape, lambda i, s: (0, 0, 0))])
    args = [p_gdn, p_gdn, p_gdn, conv_w, alog_row, dtb_row, nw_row, bmask]
    if finish:
        in_specs.append(pl.BlockSpec((SCAN_ROWS, TOKEN_BLOCK, W_GROUP), lambda i, s: (i, blk_of(s), 0)))
        args.append(other)
    return pl.pallas_call(
        kern,
        out_shape=jax.ShapeDtypeStruct((b, lt, W_GROUP), F32),
        grid=(b // SCAN_ROWS, nb),
        in_specs=in_specs,
        out_specs=pl.BlockSpec((SCAN_ROWS, TOKEN_BLOCK, W_GROUP), lambda i, s: (i, blk_of(s), 0)),
        scratch_shapes=[pltpu.VMEM((SCAN_ROWS * N_HEADS, HEAD_DIM, HEAD_DIM), F32),
                        pltpu.VMEM((SCAN_ROWS, TOKEN_BLOCK + 16, 3 * W_GROUP), F32),
                        pltpu.VMEM((SCAN_ROWS, TOKEN_BLOCK, W_GROUP), F32)],
        compiler_params=_cparams(2), name=f"gdn_dir{direction}",
    )(*args)


def _hgrn_kernel(*refs, direction, finish, nbl):
    if finish:
        p_ref, lb_ref, nw_ref, bm_ref, cum_ref, other_ref, o_ref, s_ref, ob_ref = refs
    else:
        p_ref, lb_ref, nw_ref, bm_ref, cum_ref, o_ref, s_ref, ob_ref = refs
    step = pl.program_id(1)

    @pl.when(step == 0)
    def _():
        s_ref[...] = jnp.zeros_like(s_ref)

    lb = lb_ref[...]
    n_levels = cum_ref.shape[0] // TOKEN_BLOCK - 1
    rows = range(SCAN_ROWS)
    p_rows, q_rows, key_rows, v_rows, vt_rows, gcum_rows, qg_rows, lev_rows = [], [], [], [], [], [], [], []
    for r in rows:
        p = p_ref[r]
        q = _silu(p[:, :W_GROUP])
        v = p[:, W_GROUP:2 * W_GROUP]
        fl = p[:, (2 + direction) * W_GROUP:(3 + direction) * W_GROUP]
        a = jnp.log(lb)
        bb = jnp.log(1.0 - lb) + _log_sigmoid(fl)
        logf = jnp.maximum(a, bb) + jnp.log(1.0 + jnp.exp(-jnp.abs(a - bb)))
        logf_hi = logf.astype(BF16)
        logf_lo = (logf - logf_hi.astype(F32)).astype(BF16)
        cums = (jnp.dot(cum_ref[...], logf_hi, preferred_element_type=F32)
                + jnp.dot(cum_ref[...], logf_lo, preferred_element_type=F32))
        gcum = cums[:TOKEN_BLOCK]
        p_rows.append(p)
        q_rows.append(q)
        key_rows.append((1.0 - lb) * _sigmoid(-fl))
        v_rows.append(v)
        vt_rows.append(_bdot_nt(bm_ref[2], v))
        gcum_rows.append(gcum)
        qg_rows.append(q * jnp.exp(gcum))
        lev_rows.append([jnp.exp(-jnp.abs(cums[(lev + 1) * TOKEN_BLOCK:(lev + 2) * TOKEN_BLOCK]))
                         for lev in range(n_levels)])

    n_chunks = TOKEN_BLOCK // CHUNK
    problems = [(r, h) for r in rows for h in range(N_HEADS)]
    heads = range(len(problems))
    lanes = [slice(h * HEAD_DIM, (h + 1) * HEAD_DIM) for _, h in problems]
    ci_ = lax.broadcasted_iota(jnp.int32, (CHUNK, CHUNK), 0)
    cj_ = lax.broadcasted_iota(jnp.int32, (CHUNK, CHUNK), 1)
    pi_, pj_ = (ci_, cj_) if direction == 0 else (CHUNK - 1 - ci_, CHUNK - 1 - cj_)
    chunked = lambda a: a.reshape(n_chunks, CHUNK, a.shape[1])
    bmm = lambda a, b: jnp.einsum("cij,cjk->cik", a.astype(BF16), b.astype(BF16), preferred_element_type=F32)
    bmm_nt = lambda a, b: jnp.einsum("cid,cjd->cij", a.astype(BF16), b.astype(BF16), preferred_element_type=F32)
    qc = [chunked(q_rows[r][:, ls]) for (r, _), ls in zip(problems, lanes)]
    kc = [chunked(key_rows[r][:, ls]) for (r, _), ls in zip(problems, lanes)]
    amat = [bmm_nt(qc[h], kc[h]) * (ci_ == cj_).astype(F32) for h in heads]
    qc_bf = [a.astype(BF16) for a in qc]
    kc_bf = [a.astype(BF16) for a in kc]
    for lev in range(n_levels):
        m = 1 << lev
        half = (((pi_ // (2 * m)) == (pj_ // (2 * m))) & ((pi_ // m) % 2 == 1) & ((pj_ // m) % 2 == 0)).astype(F32)
        e_bf = [lev_rows[r][lev].astype(BF16) for r in rows]
        e = [chunked(e_bf[r][:, ls]) for (r, _), ls in zip(problems, lanes)]
        amat = [amat[h] + half * bmm_nt(qc_bf[h] * e[h], kc_bf[h] * e[h]) for h in heads]
    intra = [bmm(amat[h], chunked(v_rows[r][:, ls])) for h, ((r, _), ls) in enumerate(zip(problems, lanes))]
    state = [s_ref[h] for h in heads]

    last = CHUNK - 1 if direction == 0 else 0
    for ci in range(n_chunks):
        c = ci if direction == 0 else n_chunks - 1 - ci
        rs = slice(c * CHUNK, (c + 1) * CHUNK)
        g_end = [gcum_rows[r][c * CHUNK + last:c * CHUNK + last + 1, :] for r in rows]
        kdec = [key_rows[r][rs, :] * jnp.exp(g_end[r] - gcum_rows[r][rs, :]) for r in rows]
        decay_end = [jnp.exp(g_end[r]) for r in rows]
        for h, ((r, _), ls) in enumerate(zip(problems, lanes)):
            ob_ref[r, rs, ls] = _bdot_nt(qg_rows[r][rs, ls], state[h]) + intra[h][c]
        state = [state[h] * decay_end[r][:, ls] + _bdot(vt_rows[r][ls, rs], kdec[r][:, ls])
                 for h, ((r, _), ls) in enumerate(zip(problems, lanes))]
    for h in heads:
        s_ref[h] = state[h]

    for r in rows:
        if finish:
            o = ob_ref[r] + other_ref[r]
            gate = _silu(p_rows[r][:, 4 * W_GROUP:5 * W_GROUP])
            o_ref[r] = _group_rms(o, bm_ref[1], HEAD_DIM, nw_ref[...]) * gate
        else:
            o_ref[r] = ob_ref[r]


def _hgrn_direction(p_hg, lb_row, nw_row, other, direction, nbl):
    b, lt, width = p_hg.shape
    nb = nbl + 1
    blk_of = lambda s: _scan_block(s, nbl, direction)
    finish = other is not None
    tables = _scan_tables(direction, CHUNK)
    bmask = tables[:3]
    cum_tab = jnp.concatenate([tables[0:1], tables[3:]], axis=0).reshape(-1, TOKEN_BLOCK)
    assert b % SCAN_ROWS == 0
    kern = functools.partial(_hgrn_kernel, direction=direction, finish=finish, nbl=nbl)
    in_specs = [pl.BlockSpec((SCAN_ROWS, TOKEN_BLOCK, width), lambda i, s: (i, blk_of(s), 0)),
                pl.BlockSpec((1, W_GROUP), lambda i, s: (0, 0)),
                pl.BlockSpec((1, W_GROUP), lambda i, s: (0, 0)),
                pl.BlockSpec(bmask.shape, lambda i, s: (0, 0, 0)),
                pl.BlockSpec(cum_tab.shape, lambda i, s: (0, 0))]
    args = [p_hg, lb_row, nw_row, bmask, cum_tab]
    if finish:
        in_specs.append(pl.BlockSpec((SCAN_ROWS, TOKEN_BLOCK, W_GROUP), lambda i, s: (i, blk_of(s), 0)))
        args.append(other)
    return pl.pallas_call(
        kern,
        out_shape=jax.ShapeDtypeStruct((b, lt, W_GROUP), F32),
        grid=(b // SCAN_ROWS, nb),
        in_specs=in_specs,
        out_specs=pl.BlockSpec((SCAN_ROWS, TOKEN_BLOCK, W_GROUP), lambda i, s: (i, blk_of(s), 0)),
        scratch_shapes=[pltpu.VMEM((SCAN_ROWS * N_HEADS, HEAD_DIM, HEAD_DIM), F32),
                        pltpu.VMEM((SCAN_ROWS, TOKEN_BLOCK, W_GROUP), F32)],
        compiler_params=_cparams(2), name=f"hgrn_dir{direction}",
    )(*args)


def _da_prep(p, cos, sin, qw, kw):
    g32 = _group_ones(W_GROUP, DA_DK)
    lane = lax.broadcasted_iota(jnp.int32, (TOKEN_BLOCK, W_GROUP), 1)
    first_half = (lane % 16) < 8

    def norm_rope(x, w):
        ms = _bdot(x * x, g32) * (1.0 / DA_DK)
        y = x * lax.rsqrt(ms + EPS) * w
        partner = jnp.where(first_half, pltpu.roll(y, W_GROUP - 8, axis=1), pltpu.roll(y, 8, axis=1))
        return y * cos + partner * sin

    q = (norm_rope(p[:, :W_GROUP], qw) * (DA_DK ** -0.5)).astype(BF16)
    k = norm_rope(p[:, W_GROUP:2 * W_GROUP], kw).astype(BF16)
    dst = lax.broadcasted_iota(jnp.int32, (2 * W_GROUP, W_GROUP), 0)
    src = lax.broadcasted_iota(jnp.int32, (2 * W_GROUP, W_GROUP), 1)
    spread = (dst == (src // HEAD_DIM) * 128 + src % HEAD_DIM).astype(BF16)
    wide_t = _bdot_nt(spread, p[:, 2 * W_GROUP:])
    row_w = lax.broadcasted_iota(jnp.int32, wide_t.shape, 0)
    return q, k, jnp.where(row_w % 128 == HEAD_DIM, 1.0, wide_t).astype(BF16)


def _in_proj_kernel(x_ref, mod_ref, lnw_ref, w_ref, cos_ref, sin_ref, qw_ref, kw_ref,
                    gdn_ref, hy_ref, hg_ref, q_ref, k_ref, vt_ref, *, splits):
    x = x_ref[0]
    ms = jnp.mean(x * x, axis=-1, keepdims=True)
    y = x * lax.rsqrt(ms + EPS) * lnw_ref[...]
    mod = mod_ref[0]
    y = y * (1.0 + mod[1:2]) + mod[0:1]
    h = _bdot(y, w_ref[...])
    off = 0
    for o_ref, n in zip((gdn_ref, hy_ref, hg_ref), splits[:3]):
        o_ref[0] = h[:, off:off + n]
        off += n
    q, k, vt = _da_prep(h[:, off:], cos_ref[...], sin_ref[...], qw_ref[...], kw_ref[...])
    q_ref[0] = q
    k_ref[0] = k
    vt_ref[0] = vt


def _in_proj(xs, modtab, ln_w, w, splits, cos_tab, sin_tab, qw_row, kw_row, nbl):
    b, lt, d = xs.shape
    nb = lt // TOKEN_BLOCK
    n = w.shape[1]
    tok = lambda width: pl.BlockSpec((1, TOKEN_BLOCK, width), lambda i, t: (i, t, 0))
    row = lambda width: pl.BlockSpec((1, width), lambda i, t: (0, 0))
    tab = lambda: pl.BlockSpec((TOKEN_BLOCK, W_GROUP), lambda i, t: (t, 0))
    return pl.pallas_call(
        functools.partial(_in_proj_kernel, splits=splits),
        out_shape=[jax.ShapeDtypeStruct((b, lt, s), F32) for s in splits[:3]]
                  + [jax.ShapeDtypeStruct((b, lt, W_GROUP), BF16)] * 2
                  + [jax.ShapeDtypeStruct((b, 2 * W_GROUP, lt), BF16)],
        grid=(b, nb),
        in_specs=[tok(d), pl.BlockSpec((1, 8, d), _mod_index(nbl, b)), row(d),
                  pl.BlockSpec((d, n), lambda i, t: (0, 0)), tab(), tab(), row(W_GROUP), row(W_GROUP)],
        out_specs=[tok(s) for s in splits[:3]] + [tok(W_GROUP), tok(W_GROUP),
                   pl.BlockSpec((1, 2 * W_GROUP, TOKEN_BLOCK), lambda i, t: (i, 0, t))],
        compiler_params=_cparams(2), name="in_proj",
    )(xs, modtab, ln_w.reshape(1, d), w, cos_tab, sin_tab, qw_row, kw_row)


def _da_attn_kernel(q_ref, k_ref, v_ref, lam_ref, sw_ref, o_ref, *, lam_init):
    q = q_ref[0]
    k = k_ref[0]
    lp = lam_ref[...]
    lam = (jnp.exp(jnp.sum(lp[0:1] * lp[1:2], axis=1, keepdims=True))
           - jnp.exp(jnp.sum(lp[2:3] * lp[3:4], axis=1, keepdims=True)) + lam_init)
    lane = lax.broadcasted_iota(jnp.int32, q.shape, 1)

    def scores(m):
        lo = (m // 2) * HEAD_DIM + (m % 2) * DA_DK
        qm = jnp.where(jnp.logical_and(lane >= lo, lane < lo + DA_DK), q, jnp.zeros_like(q))
        return lax.dot_general(k, qm, (((1,), (1,)), ((), ())), preferred_element_type=F32).astype(BF16)

    n_maps = 2 * N_HEADS
    maps = []
    ahead = [scores(0), scores(1)]
    for m in range(n_maps):
        s = ahead.pop(0)
        if m + 2 < n_maps:
            ahead.append(scores(m + 2))
        e = jnp.exp(s - jnp.max(s, axis=0, keepdims=True))
        vt_h = v_ref[0, (m // 2) * 128:(m // 2 + 1) * 128, :]
        acc = jnp.dot(vt_h, e, preferred_element_type=F32)
        maps.append(acc / acc[HEAD_DIM:HEAD_DIM + 1, :])
    outs = [(maps[2 * h] - lam * maps[2 * h + 1])[:HEAD_DIM] for h in range(N_HEADS)]
    o = jnp.concatenate(outs, axis=0).T
    g64 = _group_ones(W_GROUP, HEAD_DIM)
    o_ref[0] = _group_rms(o, g64, HEAD_DIM, sw_ref[...]) * (1.0 - lam_init)


def _da_attention(qh, kh, vh, lam_p, sw_row, lam_init, q_tile, q_blocks, q_off, k_rows, k_blk, name):
    b, lt, _ = qh.shape
    kern = functools.partial(_da_attn_kernel, lam_init=lam_init)
    return pl.pallas_call(
        kern,
        out_shape=jax.ShapeDtypeStruct((b, q_blocks * q_tile, W_GROUP), F32),
        grid=(b, q_blocks),
        in_specs=[pl.BlockSpec((1, q_tile, W_GROUP), lambda i, t: (i, t + q_off, 0)),
                  pl.BlockSpec((1, k_rows, W_GROUP), lambda i, t: (i, k_blk, 0)),
                  pl.BlockSpec((1, 2 * W_GROUP, k_rows), lambda i, t: (i, 0, k_blk)),
                  pl.BlockSpec((4, DA_DK), lambda i, t: (0, 0)),
                  pl.BlockSpec((1, W_GROUP), lambda i, t: (0, 0))],
        out_specs=pl.BlockSpec((1, q_tile, W_GROUP), lambda i, t: (i, t, 0)),
        compiler_params=_cparams(2), name=name,
    )(qh, kh, vh, lam_p, sw_row)


def _hy_conv_kernel(p_ref, hp_ref, hn_ref, cw_ref, v_ref, x1_ref, x2_ref, xp_ref, *, nbl):
    blk = pl.program_id(1)
    xc = _short_conv(p_ref[0], hp_ref.at[0], hn_ref.at[0], cw_ref, xp_ref, blk, nbl, 3 * W_GROUP)
    v_ref[0] = xc[:, :W_GROUP].astype(v_ref.dtype)
    x1_ref[0] = xc[:, W_GROUP:2 * W_GROUP].astype(x1_ref.dtype)
    x2_ref[0] = xc[:, 2 * W_GROUP:].astype(x2_ref.dtype)


def _hy_short_conv(p_hy, conv_w, nbl):
    b, lt, width = p_hy.shape
    nb = nbl + 1
    blk = lambda: pl.BlockSpec((1, TOKEN_BLOCK, W_GROUP), lambda i, t: (i, t, 0))
    return pl.pallas_call(
        functools.partial(_hy_conv_kernel, nbl=nbl),
        out_shape=[jax.ShapeDtypeStruct((b, lt, W_GROUP), BF16)] * 3,
        grid=(b, nb),
        in_specs=[pl.BlockSpec((1, TOKEN_BLOCK, width), lambda i, t: (i, t, 0))]
                 + _halo_specs(width, nbl, lambda s: s, lt)
                 + [pl.BlockSpec((3, width), lambda i, t: (0, 0))],
        out_specs=[blk(), blk(), blk()],
        scratch_shapes=[pltpu.VMEM((TOKEN_BLOCK + 16, width), F32)],
        compiler_params=_cparams(2), name="hy_short_conv",
    )(p_hy, p_hy, p_hy, conv_w)


def _hy_filter_kernel(z_ref, w1_ref, b1_ref, f1_ref, w2_ref, b2_ref, f2_ref, w3_ref, dec_ref, o_ref):
    z = z_ref[...]
    h = jnp.sin(f1_ref[...] * (_hp_dot(z, w1_ref[...]) + b1_ref[...]))
    h = jnp.sin(f2_ref[...] * (_hp_dot(h, w2_ref[...]) + b2_ref[...]))
    h = _hp_dot(h, w3_ref[...])
    h = h * jnp.exp(-z[:, 0:1] * jnp.abs(dec_ref[...]))
    first = (pl.program_id(0) * h.shape[0] + lax.broadcasted_iota(jnp.int32, (h.shape[0], 1), 0)) == 0
    for plane in range(4):
        piece = h[:, plane * W_GROUP:(plane + 1) * W_GROUP]
        o_ref[plane] = jnp.where(first, 0.0, piece) if plane >= 2 else piece


def _hy_filters(zfeat, w1p, b1, f1, w2, b2, f2, w3, dec_row):
    n = zfeat.shape[0]
    tn = min(n, 512)
    full = lambda a: pl.BlockSpec(a.shape, lambda i: (0,) * a.ndim)
    args = [w1p, b1, f1, w2, b2, f2, w3, dec_row]
    return pl.pallas_call(
        _hy_filter_kernel,
        out_shape=jax.ShapeDtypeStruct((4, n, W_GROUP), F32),
        grid=(n // tn,),
        in_specs=[pl.BlockSpec((tn, 128), lambda i: (i, 0))] + [full(a) for a in args],
        out_specs=pl.BlockSpec((4, tn, W_GROUP), lambda i: (0, i, 0)),
        compiler_params=_cparams(1), name="hy_filters",
    )(zfeat, *args)


def _left_split_dot(a, v):
    a_hi = a.astype(BF16)
    a_lo = (a - a_hi.astype(F32)).astype(BF16)
    return jnp.dot(a_hi, v, preferred_element_type=F32) + jnp.dot(a_lo, v, preferred_element_type=F32)


def _fft_stage1_kernel(a_ref, u_ref, o_ref):
    u = u_ref[0]
    y = _left_split_dot(a_ref[...], u) if u.dtype == BF16 else _hp_dot(a_ref[...], u)
    o_ref[0] = y.astype(o_ref.dtype)


def _fft_stage1(a_mat, u_view, row_blk, col_tile):
    nbatch, _, cols = u_view.shape
    m, kdim = a_mat.shape
    return pl.pallas_call(
        _fft_stage1_kernel,
        out_shape=jax.ShapeDtypeStruct((nbatch, m, cols), BF16),
        grid=(nbatch, cols // col_tile),
        in_specs=[pl.BlockSpec((m, kdim), lambda i, j: (0, 0)),
                  pl.BlockSpec((1, kdim, col_tile), lambda i, j: (i, row_blk, j))],
        out_specs=pl.BlockSpec((1, m, col_tile), lambda i, j: (i, 0, j)),
        compiler_params=_cparams(2), name="fft_stage1",
    )(a_mat, u_view)


def _fft_mid_kernel(*refs, n2, multiply):
    if multiply:
        y_ref, g_ref, kf_ref, kb_ref, gi_ref, o_ref = refs
    else:
        y_ref, g_ref, o_ref = refs
    group = range(g_ref.shape[0])
    z = [_bdot(g_ref[j], y_ref[0, :, j].reshape(2 * n2, W_GROUP)) for j in group]
    if multiply:
        prod = []
        for j in group:
            zr, zi = z[j][:n2], z[j][n2:]
            kr = kf_ref[0, 0, j].astype(F32) + kb_ref[0, 0, j].astype(F32)
            ki = kf_ref[0, 1, j].astype(F32) - kb_ref[0, 1, j].astype(F32)
            prod.append(jnp.concatenate([zr * kr - zi * ki, zr * ki + zi * kr], axis=0))
        z = [_bdot(gi_ref[j], prod[j]) for j in group]
    for j in group:
        o_ref[0, :, j] = z[j].reshape(2, n2, W_GROUP).astype(o_ref.dtype)


def _fft_mid(y1, g_fwd, kf, filt, g_inv, k1p, n2):
    nbatch = y1.shape[0]
    multiply = kf is not None
    group = k1p // 2
    data = lambda plane: pl.BlockSpec((1, 2, group, n2, W_GROUP),
                                      (lambda k, i: (i, 0, k, 0, 0)) if plane is None
                                      else (lambda k, i: (plane, 0, k, 0, 0)))
    mat = lambda: pl.BlockSpec((group, 2 * n2, 2 * n2), lambda k, i: (k, 0, 0))
    in_specs, args = [data(None), mat()], [y1, g_fwd]
    if multiply:
        in_specs += [data(filt), data(2 + filt), mat()]
        args += [kf, kf, g_inv]
    return pl.pallas_call(
        functools.partial(_fft_mid_kernel, n2=n2, multiply=multiply),
        out_shape=jax.ShapeDtypeStruct((nbatch, 2, k1p, n2, W_GROUP), BF16),
        grid=(k1p // group, nbatch),
        in_specs=in_specs,
        out_specs=data(None),
        compiler_params=_cparams(2), name="fft_mid_mul" if multiply else "fft_mid",
    )(*args)


def _fft_stage4_kernel(a_ref, v_ref, u_ref, gate_ref, skip_ref, o_ref):
    y = _left_split_dot(a_ref[...], v_ref[0])
    o_ref[0] = (gate_ref[0].astype(F32) * (y + skip_ref[...] * u_ref[0].astype(F32))).astype(o_ref.dtype)


def _fft_stage4(a_mat, v_view, u_view, u_blk, gate_view, gate_blk, skip_row, col_tile):
    nbatch, kdim, cols = v_view.shape
    m = a_mat.shape[0]
    return pl.pallas_call(
        _fft_stage4_kernel,
        out_shape=jax.ShapeDtypeStruct((nbatch, m, cols), BF16),
        grid=(nbatch, cols // col_tile),
        in_specs=[pl.BlockSpec((m, kdim), lambda i, j: (0, 0)),
                  pl.BlockSpec((1, kdim, col_tile), lambda i, j: (i, 0, j)),
                  pl.BlockSpec((1, m, col_tile), lambda i, j: (i, u_blk, j)),
                  pl.BlockSpec((1, m, col_tile), lambda i, j: (i, gate_blk, j)),
                  pl.BlockSpec((1, col_tile), lambda i, j: (0, j))],
        out_specs=pl.BlockSpec((1, m, col_tile), lambda i, j: (i, 0, j)),
        compiler_params=_cparams(2), name="fft_stage4",
    )(a_mat, v_view, u_view, gate_view, skip_row)


def _fft_tables(n, n2):
    big = 2 * n
    n1 = big // n2
    two_pi = 2.0 * math.pi

    def cs(num, den):
        ang = (num % den).astype(F32) * (two_pi / den)
        return jnp.cos(ang), jnp.sin(ang)

    k1p = -(-(n1 // 2 + 1) // FFT_K1_GROUP) * FFT_K1_GROUP
    k1 = jnp.arange(k1p, dtype=jnp.int32)
    live = (k1 <= n1 // 2).astype(F32)
    weight = jnp.where((k1 == 0) | (k1 == n1 // 2), 1.0, 2.0) * live
    m1 = jnp.arange(n1, dtype=jnp.int32)
    c, s = cs(k1[:, None] * m1[None, :], n1)
    a1_full = jnp.concatenate([c * live[:, None], -s * live[:, None]], axis=0)
    k2 = jnp.arange(n2, dtype=jnp.int32)
    num = (k2[None, :, None] * k2[None, None, :] * n1 + k1[:, None, None] * k2[None, None, :]) % big
    c, s = cs(num, big)
    g_fwd = jnp.concatenate([jnp.concatenate([c, s], axis=2), jnp.concatenate([-s, c], axis=2)], axis=1)
    num = (k2[None, :, None] * k2[None, None, :] * n1 + k1[:, None, None] * k2[None, :, None]) % big
    c, s = cs(num, big)
    g_inv = jnp.concatenate([jnp.concatenate([c, -s], axis=2), jnp.concatenate([s, c], axis=2)], axis=1)
    t1 = jnp.arange(n1 // 2, dtype=jnp.int32)
    c, s = cs(t1[:, None] * k1[None, :], n1)
    a4 = jnp.concatenate([c * weight[None, :], -s * weight[None, :]], axis=1) * (1.0 / big)
    return n1, k1p, a1_full, g_fwd.astype(BF16), g_inv.astype(BF16), a4


def _hy_positional(n):
    pos = jnp.arange(n, dtype=F32)
    t01 = pos / max(n - 1, 1)
    bands = jnp.linspace(1e-4, HY_BANDS - 1, HY_BANDS, dtype=F32)
    ang = (2.0 * math.pi / n) * pos[:, None] * bands
    z = jnp.concatenate([t01[:, None], jnp.cos(ang), -jnp.sin(ang)], axis=-1)
    return jnp.pad(z, ((0, 0), (0, 128 - z.shape[1])))


def _hyena_segment(v_all, x1_all, x2_all, seg_start, n, n2, filt_args, skip):
    b, lt, c = v_all.shape
    n1, k1p, a1_full, g_fwd, g_inv, a4 = _fft_tables(n, n2)
    a1_half = a1_full[:, :n1 // 2]
    cols = n2 * c
    col_tile = min(cols, 16384)
    seg_blk = seg_start // n
    view = lambda a: a.reshape(b, lt // n2, cols)

    h = _hy_filters(_hy_positional(n), *filt_args)
    ky = _fft_stage1(a1_half, h.reshape(4, n1 // 2, cols), 0, col_tile)
    kf = _fft_mid(ky.reshape(4, 2, k1p, n2, c), g_fwd, None, 0, None, k1p, n2)

    def long_conv(u_view, u_blk, gate_all, filt):
        y1 = _fft_stage1(a1_half, u_view, u_blk, col_tile)
        vmid = _fft_mid(y1.reshape(b, 2, k1p, n2, c), g_fwd, kf, filt, g_inv, k1p, n2)
        skip_row = jnp.tile(skip[filt], n2).reshape(1, cols)
        return _fft_stage4(a4, vmid.reshape(b, 2 * k1p, cols), u_view, u_blk, view(gate_all), seg_blk,
                           skip_row, col_tile)

    z = long_conv(view(v_all), seg_blk, x1_all, 0)
    out = long_conv(z, 0, x2_all, 1)
    return out.reshape(b, n, c)


def _rope_tables(rows, lt):
    n_freq = DA_DK // 4
    inv = ROPE_THETA ** (-jnp.arange(n_freq, dtype=F32) / n_freq)
    r = jnp.repeat(jnp.arange(rows, dtype=F32), GRID_W)
    col = jnp.tile(jnp.arange(GRID_W, dtype=F32), rows)
    ang_r, ang_c = r[:, None] * inv, col[:, None] * inv
    cos32 = jnp.concatenate([jnp.cos(ang_r)] * 2 + [jnp.cos(ang_c)] * 2, axis=-1)
    sin32 = jnp.concatenate([-jnp.sin(ang_r), jnp.sin(ang_r), -jnp.sin(ang_c), jnp.sin(ang_c)], axis=-1)
    n = rows * GRID_W
    pad = lt - n
    cos = jnp.concatenate([jnp.tile(cos32, (1, W_GROUP // DA_DK)), jnp.ones((pad, W_GROUP), F32)], axis=0)
    sin = jnp.concatenate([jnp.tile(sin32, (1, W_GROUP // DA_DK)), jnp.zeros((pad, W_GROUP), F32)], axis=0)
    return cos, sin


def kernel(x, c, ctx, c_ctx, mod_w, mod_b, ln1_w, ln2_w, w_in, gdn_conv_w, gdn_a_log, gdn_dt_bias, gdn_norm_w, hy_conv_w, hy_w1, hy_b1, hy_f1, hy_w2, hy_b2, hy_f2, hy_w3, hy_decay, hy_bias, hg_lb_raw, hg_norm_w, da_q_norm, da_k_norm, da_lam, da_subln, w_out, mlp_w1, mlp_w2):
    batch, seq, d = x.shape
    n_ctx = ctx.shape[1]
    depth = mod_w.shape[0]
    assert n_ctx == TOKEN_BLOCK and seq % TOKEN_BLOCK == 0 and d == 4 * W_GROUP
    nbl = seq // TOKEN_BLOCK
    lt = seq + n_ctx

    xs = jnp.concatenate([x, ctx], axis=1)
    cond = jnp.concatenate([c, c_ctx[None, :], jnp.zeros((-(batch + 1) % 8, d), F32)], axis=0)
    mods = _modulation(cond, mod_w, mod_b)[:, :batch + 1].reshape(depth, batch + 1, 6, d)
    mods = jnp.pad(mods, ((0, 0), (0, 0), (0, 2), (0, 0)))

    lb_all = jnp.cumsum(jax.nn.softmax(hg_lb_raw.astype(F32), axis=0), axis=0)
    lb_all = lb_all - lb_all[0]
    cos_tab, sin_tab = _rope_tables(seq // GRID_W, lt)

    gdn_cols = 4 * W_GROUP + 4 * N_HEADS
    hy_cols, hg_cols, da_cols = 3 * W_GROUP, 5 * W_GROUP, 3 * W_GROUP
    splits = (GDN_PAD_COLS, hy_cols, hg_cols, da_cols)

    for l in range(depth):
        want_ctx = l < depth - 1
        modtab = mods[l]
        wl = w_in[l]
        w_pad = jnp.concatenate([wl[:, :gdn_cols], jnp.zeros((d, GDN_PAD_COLS - gdn_cols), F32),
                                 wl[:, gdn_cols:]], axis=1).astype(BF16)
        qw = jnp.tile(da_q_norm[l], W_GROUP // DA_DK).reshape(1, W_GROUP)
        kw = jnp.tile(da_k_norm[l], W_GROUP // DA_DK).reshape(1, W_GROUP)
        p_gdn, p_hy, p_hg, qh, kh, vh = _in_proj(xs, modtab, ln1_w[l], w_pad, splits, cos_tab, sin_tab, qw, kw, nbl)

        pad16 = lambda a: jnp.pad(a.reshape(1, 2 * N_HEADS), ((0, 0), (0, 128 - 2 * N_HEADS)))
        alog_row, dtb_row = pad16(gdn_a_log[l]), pad16(gdn_dt_bias[l])
        gnw = jnp.tile(gdn_norm_w[l], N_HEADS).reshape(1, W_GROUP)
        o_b = _gdn_direction(p_gdn, gdn_conv_w[l], alog_row, dtb_row, gnw, None, 1, nbl)
        o_gdn = _gdn_direction(p_gdn, gdn_conv_w[l], alog_row, dtb_row, gnw, o_b, 0, nbl)

        hv, hx1, hx2 = _hy_short_conv(p_hy, hy_conv_w[l], nbl)
        w1p = jnp.pad(hy_w1[l], ((0, 128 - hy_w1.shape[1]), (0, 0)))
        row = lambda a: a.reshape(1, -1)
        dec_row = jnp.tile(hy_decay[l].reshape(1, 2 * W_GROUP), (1, 2))
        filt_args = (w1p, row(hy_b1[l]), row(hy_f1[l]), hy_w2[l], row(hy_b2[l]), row(hy_f2[l]), hy_w3[l], dec_row)
        o_hy_lat = _hyena_segment(hv, hx1, hx2, 0, seq, FFT_N2_LAT, filt_args, hy_bias[l])
        o_hy_ctx = (_hyena_segment(hv, hx1, hx2, seq, n_ctx, FFT_N2_CTX, filt_args, hy_bias[l])
                    if want_ctx else None)

        lb_row = lb_all[l].reshape(1, W_GROUP)
        hnw = jnp.tile(hg_norm_w[l], N_HEADS).reshape(1, W_GROUP)
        o_hb = _hgrn_direction(p_hg, lb_row, hnw, None, 1, nbl)
        o_hg = _hgrn_direction(p_hg, lb_row, hnw, o_hb, 0, nbl)

        lam_init = 0.8 - 0.6 * math.exp(-0.3 * l)
        sw = jnp.tile(da_subln[l], N_HEADS).reshape(1, W_GROUP)
        o_da = _da_attention(qh, kh, vh, da_lam[l], sw, lam_init, DA_Q_TILE, seq // DA_Q_TILE, 0, lt, 0,
                             "da_attn_lat")
        o_da_ctx = (_da_attention(qh, kh, vh, da_lam[l], sw, lam_init, TOKEN_BLOCK, 1, nbl, TOKEN_BLOCK, nbl,
                                  "da_attn_ctx") if want_ctx else None)

        n_blocks = nbl + 1 if want_ctx else nbl
        xs_mid = _proj_residual(xs, modtab, w_out[l].astype(BF16),
                                [o_gdn, (o_hy_lat, o_hy_ctx), o_hg, (o_da, o_da_ctx)], 2, n_blocks, nbl, "out_proj")
        (hid,) = _adaln_proj(xs_mid, modtab, ln2_w[l], mlp_w1[l].astype(BF16), (mlp_w1.shape[2],), (BF16,),
                             3, 4, True, nbl, "mlp_up")
        xs = _proj_residual(xs_mid, modtab, mlp_w2[l].astype(BF16), [hid], 5, n_blocks, nbl, "mlp_down")
    return xs
```
